```python
import math
import jax
import jax.numpy as jnp
from jax import lax
import numpy as np

D_MODEL = 4096
BATCH = 16
SEQ = 256
DEPTH = 2
DEC_BATCH = 4
DEC_SEQ = 2048
PAST_LEN = 256

GRID_W = 64
N_MIX_GROUPS = 4
MIX_W = D_MODEL
GROUP_W = MIX_W // N_MIX_GROUPS

HEAD_DIM = 128
ATT_HEADS = GROUP_W // HEAD_DIM
ATT_KV_HEADS = 2
ROPE_THETA = 10000.0
Q_BLOCK = 128

SSD_HEADDIM = 64
SSD_HEADS = GROUP_W // SSD_HEADDIM
SSD_STATE = 128
SSD_GROUPS = 2
SSD_CHUNK = 64
CONV_W = 4

GLA_HEADS = 4
GLA_DK = GROUP_W // 2 // GLA_HEADS
GLA_DV = GROUP_W // GLA_HEADS
GLA_RANK = 16
GLA_NORMALIZER = 16.0
GLA_CHUNK = 64

LRU_W = GROUP_W
LRU_BLOCKS = 8
LRU_BW = LRU_W // LRU_BLOCKS
LRU_C = 8.0

D_FF = 11008
N_MOD = 9
LN_EPS = 1e-5
RMS_EPS = 1e-6

ATT_Q = ATT_HEADS * HEAD_DIM
ATT_KV = ATT_KV_HEADS * HEAD_DIM
SSD_BC = SSD_GROUPS * SSD_STATE
SSD_CONV_CH = GROUP_W + 2 * SSD_BC
GLA_K = GLA_HEADS * GLA_DK
GLA_V = GLA_HEADS * GLA_DV
IN_SPLITS = (ATT_Q, ATT_KV, ATT_KV,
             GROUP_W, GROUP_W, SSD_BC, SSD_BC, 2 * SSD_HEADS,
             GLA_K, GLA_K, GLA_V, 2 * GLA_RANK, GLA_V,
             LRU_W, LRU_W)
IN_W = sum(IN_SPLITS)
IN_OFFSETS = tuple(int(o) for o in np.cumsum(IN_SPLITS)[:-1])

kernel_name = 'hybrid_flow_trunk_ctx_and_denoise'


def layer_norm(x, g, b):
    xf = x.astype(jnp.float32)
    mu = jnp.mean(xf, axis=-1, keepdims=True)
    var = jnp.mean(jnp.square(xf - mu), axis=-1, keepdims=True)
    return ((xf - mu) * lax.rsqrt(var + LN_EPS) * g + b).astype(x.dtype)


def rms_norm(x, g):
    xf = x.astype(jnp.float32)
    return (xf * lax.rsqrt(jnp.mean(xf * xf, axis=-1, keepdims=True) + RMS_EPS) * g).astype(x.dtype)


def rev(t):
    return jnp.flip(t, axis=1)


def dwconv(x, w, b):
    n_tok = x.shape[1]
    left = CONV_W // 2
    xp = jnp.pad(x, ((0, 0), (left, CONV_W - 1 - left), (0, 0)))
    return sum(xp[:, j:j + n_tok] * w[j] for j in range(CONV_W)) + b


def swiglu(x, w_gate, w_up, w_down):
    return (jax.nn.silu(x @ w_gate) * (x @ w_up)) @ w_down


def axial_rope(n_tok):
    rows = n_tok // GRID_W
    row = jnp.repeat(jnp.arange(rows, dtype=jnp.float32), GRID_W)
    col = jnp.tile(jnp.arange(GRID_W, dtype=jnp.float32), rows)
    n_freq = HEAD_DIM // 4
    inv = ROPE_THETA ** (-jnp.arange(n_freq, dtype=jnp.float32) / n_freq)
    ang = jnp.stack([row[:, None] * inv, col[:, None] * inv], axis=1)
    return jnp.cos(ang), jnp.sin(ang)


def apply_axial_rope(x, cos, sin):
    xf = x.astype(jnp.float32).reshape(*x.shape[:-1], 2, 2, HEAD_DIM // 4)
    x1, x2 = xf[..., 0, :], xf[..., 1, :]
    cs, sn = cos[None, :, None], sin[None, :, None]
    out = jnp.stack([x1 * cs - x2 * sn, x2 * cs + x1 * sn], axis=-2)
    return out.reshape(x.shape).astype(x.dtype)


def block_attention(q, k, v):
    bsz, n_q = q.shape[:2]
    rep = ATT_HEADS // ATT_KV_HEADS
    qb = q.reshape(bsz, n_q // Q_BLOCK, Q_BLOCK, ATT_KV_HEADS, rep, HEAD_DIM)
    qb = jnp.moveaxis(qb, 1, 0)
    scale = HEAD_DIM ** -0.5

    def one_block(q_blk):
        s = jnp.einsum('bqgrd,bkgd->bgrqk', q_blk, k).astype(jnp.float32) * scale
        p = jax.nn.softmax(s, axis=-1).astype(v.dtype)
        return jnp.einsum('bgrqk,bkgd->bqgrd', p, v)

    o = lax.map(one_block, qb)
    return jnp.moveaxis(o, 0, 1).reshape(bsz, n_q, ATT_HEADS * HEAD_DIM)


def ssd_chunk_scan(x, dt, a, bm, cm, h0):
    bsz, n_tok = x.shape[:2]
    n_chunk = n_tok // SSD_CHUNK
    rep = SSD_HEADS // SSD_GROUPS
    bh = jnp.repeat(bm, rep, axis=2)
    ch = jnp.repeat(cm, rep, axis=2)
    xdt = x * dt[..., None].astype(x.dtype)
    la = dt * a
    mask = jnp.tril(jnp.ones((SSD_CHUNK, SSD_CHUNK), bool))[None, :, :, None]

    def chunks(t):
        return jnp.moveaxis(t.reshape(bsz, n_chunk, SSD_CHUNK, *t.shape[2:]), 1, 0)

    def step(h, inp):
        xdt_c, la_c, b_c, c_c = inp
        cum = jnp.cumsum(la_c, axis=1)
        seg = jnp.exp(jnp.where(mask, cum[:, :, None] - cum[:, None], -jnp.inf)).astype(x.dtype)
        scores = jnp.einsum('bihn,bjhn->bijh', c_c, b_c) * seg
        y = (jnp.einsum('bijh,bjhp->bihp', scores, xdt_c)
             + jnp.einsum('bihn,bhpn->bihp', c_c, h) * jnp.exp(cum)[..., None].astype(x.dtype))
        to_end = jnp.exp(cum[:, -1:] - cum)[..., None].astype(x.dtype)
        h_new = (h * jnp.exp(cum[:, -1])[:, :, None, None].astype(h.dtype)
                 + jnp.einsum('bjhn,bjhp->bhpn', b_c * to_end, xdt_c))
        return h_new.astype(h.dtype), y

    h_last, ys = lax.scan(step, h0, (chunks(xdt), chunks(la), chunks(bh), chunks(ch)))
    return jnp.moveaxis(ys, 0, 1).reshape(x.shape), h_last


def gla_chunk_scan(q, k, v, g, s0):
    bsz, n_tok = q.shape[:2]
    n_chunk = n_tok // GLA_CHUNK
    mask = jnp.tril(jnp.ones((GLA_CHUNK, GLA_CHUNK), bool))[None, :, :, None, None]

    def chunks(t):
        return jnp.moveaxis(t.reshape(bsz, n_chunk, GLA_CHUNK, *t.shape[2:]), 1, 0)

    def step(s, inp):
        q_c, k_c, v_c, g_c = inp
        b = jnp.cumsum(g_c, axis=1)
        dec = jnp.exp(jnp.where(mask, b[:, :, None] - b[:, None], -jnp.inf)).astype(q_c.dtype)
        att = jnp.einsum('bihk,bjhk,bijhk->bijh', q_c, k_c, dec)
        o = (jnp.einsum('bijh,bjhv->bihv', att, v_c)
             + jnp.einsum('bihk,bhkv->bihv', q_c * jnp.exp(b).astype(q_c.dtype), s))
        b_end = b[:, -1]
        s_new = (s * jnp.exp(b_end)[..., None].astype(s.dtype)
                 + jnp.einsum('bjhk,bjhv->bhkv', k_c * jnp.exp(b_end[:, None] - b).astype(k_c.dtype), v_c))
        return s_new.astype(s.dtype), o

    s_last, os_ = lax.scan(step, s0, (chunks(q), chunks(k), chunks(v), chunks(g)))
    return jnp.moveaxis(os_, 0, 1).reshape(bsz, n_tok, v.shape[2], v.shape[3]), s_last


def rg_lru(x, wa, ba, wx, bx, lam, h0):
    bsz, n_tok, ch = x.shape
    xb = x.reshape(bsz, n_tok, LRU_BLOCKS, LRU_BW)
    r = jax.nn.sigmoid((jnp.einsum('btnc,ncd->btnd', xb, wa).reshape(bsz, n_tok, ch) + ba).astype(jnp.float32))
    i = jax.nn.sigmoid((jnp.einsum('btnc,ncd->btnd', xb, wx).reshape(bsz, n_tok, ch) + bx).astype(jnp.float32))
    log_a = -LRU_C * r * jax.nn.softplus(-lam.astype(jnp.float32))
    a = jnp.exp(log_a)
    u = jnp.sqrt(-jnp.expm1(2.0 * log_a)) * i * x.astype(jnp.float32)
    u = u.at[:, 0].add(a[:, 0] * h0.astype(jnp.float32))

    def combine(left, right):
        return left[0] * right[0], right[0] * left[1] + right[1]

    _, h = lax.associative_scan(combine, (a, u), axis=1)
    return h.astype(x.dtype), h[:, -1].astype(h0.dtype)


def mixer(h, ctx, lp):
    bsz, n_tok, _ = h.shape
    (aq, ak, av, sx, sz, sb, sc, sdt, gq, gk, gv, glr, gg, lx, lg) = jnp.split(h @ lp['w_in'], IN_OFFSETS, axis=-1)

    q = rms_norm(aq.reshape(bsz, n_tok, ATT_HEADS, HEAD_DIM), lp['q_norm'])
    k = rms_norm(ak.reshape(bsz, n_tok, ATT_KV_HEADS, HEAD_DIM), lp['k_norm'])
    v = av.reshape(bsz, n_tok, ATT_KV_HEADS, HEAD_DIM)
    if ctx is None:
        o_att = block_attention(q, k, v)
        ssd0 = jnp.zeros((bsz, 2, SSD_HEADS, SSD_HEADDIM, SSD_STATE), h.dtype)
        gla0 = jnp.zeros((bsz, 2, GLA_HEADS, GLA_DK, GLA_DV), h.dtype)
        lru0 = jnp.zeros((bsz, 2, LRU_W), h.dtype)
    else:
        ctx_k, ctx_v, ssd0, gla0, lru0 = ctx
        cos, sin = axial_rope(n_tok)
        k_all = jnp.concatenate([apply_axial_rope(k, cos, sin), ctx_k], axis=1)
        v_all = jnp.concatenate([v, ctx_v], axis=1)
        o_att = block_attention(apply_axial_rope(q, cos, sin), k_all, v_all)

    xbc = jax.nn.silu(dwconv(jnp.concatenate([sx, sb, sc], axis=-1), lp['ssd_conv_w'], lp['ssd_conv_b']))
    xs = xbc[..., :GROUP_W].reshape(bsz, n_tok, SSD_HEADS, SSD_HEADDIM)
    bm = xbc[..., GROUP_W:GROUP_W + SSD_BC].reshape(bsz, n_tok, SSD_GROUPS, SSD_STATE)
    cm = xbc[..., GROUP_W + SSD_BC:].reshape(bsz, n_tok, SSD_GROUPS, SSD_STATE)
    dt = jax.nn.softplus(sdt.reshape(bsz, n_tok, 2, SSD_HEADS).astype(jnp.float32) + lp['ssd_dt_bias'])
    a = -jnp.exp(lp['ssd_a_log'].astype(jnp.float32))
    y_f, ssd_f = ssd_chunk_scan(xs, dt[:, :, 0], a[0], bm, cm, ssd0[:, 0])
    y_b, ssd_b = ssd_chunk_scan(rev(xs), rev(dt[:, :, 1]), a[1], rev(bm), rev(cm), ssd0[:, 1])
    y = y_f + rev(y_b) + xs * (lp['ssd_d'][0] + lp['ssd_d'][1])[:, None]
    o_ssd = rms_norm(y.reshape(bsz, n_tok, GROUP_W) * jax.nn.silu(sz), lp['ssd_norm_w'])

    gq = gq.reshape(bsz, n_tok, GLA_HEADS, GLA_DK) * (GLA_DK ** -0.5)
    gk = gk.reshape(bsz, n_tok, GLA_HEADS, GLA_DK)
    gv = gv.reshape(bsz, n_tok, GLA_HEADS, GLA_DV)
    gate = jnp.einsum('btdr,drk->btdk', glr.reshape(bsz, n_tok, 2, GLA_RANK), lp['gla_gate_w']) + lp['gla_gate_b']
    glog = (jax.nn.log_sigmoid(gate.astype(jnp.float32)) / GLA_NORMALIZER).reshape(bsz, n_tok, 2, GLA_HEADS, GLA_DK)
    o_f, gla_f = gla_chunk_scan(gq, gk, gv, glog[:, :, 0], gla0[:, 0])
    o_b, gla_b = gla_chunk_scan(rev(gq), rev(gk), rev(gv), rev(glog[:, :, 1]), gla0[:, 1])
    o_gla = rms_norm(o_f + rev(o_b), lp['gla_norm_w']).reshape(bsz, n_tok, GROUP_W) * jax.nn.silu(gg)

    xl = dwconv(lx, lp['lru_conv_w'], lp['lru_conv_b'])
    h_f, lru_f = rg_lru(xl, lp['lru_wa'][0], lp['lru_ba'][0], lp['lru_wx'][0], lp['lru_bx'][0], lp['lru_lam'][0], lru0[:, 0])
    h_b, lru_b = rg_lru(rev(xl), lp['lru_wa'][1], lp['lru_ba'][1], lp['lru_wx'][1], lp['lru_bx'][1], lp['lru_lam'][1], lru0[:, 1])
    o_lru = (h_f + rev(h_b)) * jax.nn.gelu(lg)

    out = jnp.concatenate([o_att, o_ssd, o_gla, o_lru], axis=-1) @ lp['w_out']
    if ctx is None:
        return out, (k, v, jnp.stack([ssd_f, ssd_b], axis=1), jnp.stack([gla_f, gla_b], axis=1),
                     jnp.stack([lru_f, lru_b], axis=1))
    return out, None


def trunk_layer(x, cond, ctx, lp, alpha):
    mod = (jax.nn.silu(cond) @ lp['mod_w'] + lp['mod_b']).reshape(cond.shape[0], 1, N_MOD, D_MODEL)
    sh1, sc1, g1, sh2, sc2, g2, sh3, sc3, g3 = (mod[:, :, i] for i in range(N_MOD))
    f = swiglu(x * (1 + sc1) + sh1, lp['ffn_w_gate'][0], lp['ffn_w_up'][0], lp['ffn_w_down'][0])
    x = layer_norm(alpha * x + 0.5 * g1 * f, lp['ln_g'][0], lp['ln_b'][0])
    m, new_ctx = mixer(x * (1 + sc2) + sh2, ctx, lp)
    x = layer_norm(alpha * x + g2 * m, lp['ln_g'][1], lp['ln_b'][1])
    f = swiglu(x * (1 + sc3) + sh3, lp['ffn_w_gate'][1], lp['ffn_w_up'][1], lp['ffn_w_down'][1])
    x = layer_norm(alpha * x + 0.5 * g3 * f, lp['ln_g'][2], lp['ln_b'][2])
    return x, new_ctx


def setup_inputs(seed: int = 0) -> dict:
    key = jax.random.key(seed)
    keys = iter(jax.random.split(key, 48))
    f32 = jnp.float32

    def nrm(shape, scale):
        return jax.random.normal(next(keys), shape, f32) * scale

    def unif(shape, lo, hi):
        return jax.random.uniform(next(keys), shape, f32, lo, hi)

    beta = (8.0 * DEPTH) ** -0.25
    dt_init = jnp.exp(unif((DEPTH, 2, SSD_HEADS), math.log(1e-3), math.log(1e-1)))
    a_init = unif((DEPTH, 2, LRU_W), 0.9, 0.999)
    return {
        'x_prompt': nrm((BATCH, SEQ, D_MODEL), 1.0),
        'x_sample': nrm((DEC_BATCH, DEC_SEQ, D_MODEL), 1.0),
        'c': nrm((DEC_BATCH, D_MODEL), 1.0),
        'cache_attn_k': nrm((DEC_BATCH, DEPTH, PAST_LEN, ATT_KV_HEADS, HEAD_DIM), 1.0),
        'cache_attn_v': nrm((DEC_BATCH, DEPTH, PAST_LEN, ATT_KV_HEADS, HEAD_DIM), 1.0),
        'state_ssd': nrm((DEC_BATCH, DEPTH, 2, SSD_HEADS, SSD_HEADDIM, SSD_STATE), 0.5),
        'state_gla': nrm((DEC_BATCH, DEPTH, 2, GLA_HEADS, GLA_DK, GLA_DV), 0.5),
        'state_lru': nrm((DEC_BATCH, DEPTH, 2, LRU_W), 0.5),
        'c_ctx': nrm((D_MODEL,), 1.0),
        'mod_w': nrm((DEPTH, D_MODEL, N_MOD * D_MODEL), D_MODEL ** -0.5),
        'mod_b': nrm((DEPTH, N_MOD * D_MODEL), 0.02),
        'ln_g': 1.0 + nrm((DEPTH, 3, D_MODEL), 0.02),
        'ln_b': nrm((DEPTH, 3, D_MODEL), 0.02),
        'ffn_w_gate': nrm((DEPTH, 2, D_MODEL, D_FF), D_MODEL ** -0.5),
        'ffn_w_up': nrm((DEPTH, 2, D_MODEL, D_FF), D_MODEL ** -0.5),
        'ffn_w_down': nrm((DEPTH, 2, D_FF, D_MODEL), beta * D_FF ** -0.5),
        'w_in': nrm((DEPTH, D_MODEL, IN_W), D_MODEL ** -0.5),
        'w_out': nrm((DEPTH, MIX_W, D_MODEL), beta * MIX_W ** -0.5),
        'q_norm': 1.0 + nrm((DEPTH, HEAD_DIM), 0.02),
        'k_norm': 1.0 + nrm((DEPTH, HEAD_DIM), 0.02),
        'ssd_conv_w': nrm((DEPTH, CONV_W, SSD_CONV_CH), CONV_W ** -0.5),
        'ssd_conv_b': nrm((DEPTH, SSD_CONV_CH), 0.02),
        'ssd_a_log': jnp.log(unif((DEPTH, 2, SSD_HEADS), 1.0, 16.0)),
        'ssd_dt_bias': dt_init + jnp.log(-jnp.expm1(-dt_init)),
        'ssd_d': 1.0 + nrm((DEPTH, 2, SSD_HEADS), 0.02),
        'ssd_norm_w': 1.0 + nrm((DEPTH, GROUP_W), 0.02),
        'gla_gate_w': nrm((DEPTH, 2, GLA_RANK, GLA_K), GLA_RANK ** -0.5),
        'gla_gate_b': nrm((DEPTH, 2, GLA_K), 0.02),
        'gla_norm_w': 1.0 + nrm((DEPTH, GLA_DV), 0.02),
        'lru_conv_w': nrm((DEPTH, CONV_W, LRU_W), CONV_W ** -0.5),
        'lru_conv_b': nrm((DEPTH, LRU_W), 0.02),
        'lru_wa': nrm((DEPTH, 2, LRU_BLOCKS, LRU_BW, LRU_BW), LRU_BW ** -0.5),
        'lru_ba': nrm((DEPTH, 2, LRU_W), 0.02),
        'lru_wx': nrm((DEPTH, 2, LRU_BLOCKS, LRU_BW, LRU_BW), LRU_BW ** -0.5),
        'lru_bx': nrm((DEPTH, 2, LRU_W), 0.02),
        'lru_lam': jnp.log(a_init) - jnp.log1p(-a_init),
    }


def reference(x_prompt, x_sample, c, cache_attn_k, cache_attn_v, state_ssd, state_gla, state_lru, c_ctx,
              mod_w, mod_b, ln_g, ln_b, ffn_w_gate, ffn_w_up, ffn_w_down, w_in, w_out, q_norm, k_norm,
              ssd_conv_w, ssd_conv_b, ssd_a_log, ssd_dt_bias, ssd_d, ssd_norm_w,
              gla_gate_w, gla_gate_b, gla_norm_w,
              lru_conv_w, lru_conv_b, lru_wa, lru_ba, lru_wx, lru_bx, lru_lam):
    alpha = (2.0 * DEPTH) ** 0.25
    layers = [dict(mod_w=mod_w[l], mod_b=mod_b[l], ln_g=ln_g[l], ln_b=ln_b[l],
                   ffn_w_gate=ffn_w_gate[l], ffn_w_up=ffn_w_up[l], ffn_w_down=ffn_w_down[l],
                   w_in=w_in[l], w_out=w_out[l], q_norm=q_norm[l], k_norm=k_norm[l],
                   ssd_conv_w=ssd_conv_w[l], ssd_conv_b=ssd_conv_b[l], ssd_a_log=ssd_a_log[l],
                   ssd_dt_bias=ssd_dt_bias[l], ssd_d=ssd_d[l], ssd_norm_w=ssd_norm_w[l],
                   gla_gate_w=gla_gate_w[l], gla_gate_b=gla_gate_b[l], gla_norm_w=gla_norm_w[l],
                   lru_conv_w=lru_conv_w[l], lru_conv_b=lru_conv_b[l], lru_wa=lru_wa[l], lru_ba=lru_ba[l],
                   lru_wx=lru_wx[l], lru_bx=lru_bx[l], lru_lam=lru_lam[l])
              for l in range(DEPTH)]

    y_prompt = x_prompt
    ctx_states = []
    for l in range(DEPTH):
        y_prompt, st = trunk_layer(y_prompt, c_ctx[None], None, layers[l], alpha)
        ctx_states.append(st)
    new_attn_k = jnp.stack([s[0] for s in ctx_states], axis=1)
    new_attn_v = jnp.stack([s[1] for s in ctx_states], axis=1)
    new_ssd = jnp.stack([s[2] for s in ctx_states], axis=1)
    new_gla = jnp.stack([s[3] for s in ctx_states], axis=1)
    new_lru = jnp.stack([s[4] for s in ctx_states], axis=1)

    y_sample = x_sample
    for l in range(DEPTH):
        cached = (cache_attn_k[:, l], cache_attn_v[:, l], state_ssd[:, l], state_gla[:, l], state_lru[:, l])
        y_sample, _ = trunk_layer(y_sample, c, cached, layers[l], alpha)

    return (y_prompt, y_sample, new_attn_k, new_attn_v, new_ssd, new_gla, new_lru)
```

```python
import functools
import math

import numpy as np
import jax
import jax.numpy as jnp
from jax import lax
from jax.experimental import pallas as pl
from jax.experimental.pallas import tpu as pltpu

D_MODEL = 4096
BATCH = 16
SEQ = 256
DEPTH = 2
DEC_BATCH = 4
DEC_SEQ = 2048
PAST_LEN = 256
GRID_W = 64
GROUP_W = 1024
HEAD_DIM = 128
ATT_HEADS = 8
ATT_KV_HEADS = 2
ROPE_THETA = 10000.0
SSD_HEADDIM = 64
SSD_HEADS = 16
SSD_STATE = 128
SSD_GROUPS = 2
CONV_W = 4
GLA_HEADS = 4
GLA_DK = 128
GLA_DV = 256
GLA_RANK = 16
GLA_NORMALIZER = 16.0
LRU_W = 1024
LRU_BLOCKS = 8
LRU_BW = 128
LRU_C = 8.0
D_FF = 11008
N_MOD = 9
LN_EPS = 1e-5
RMS_EPS = 1e-6
ALPHA = (2.0 * DEPTH) ** 0.25

M_PROMPT = BATCH * SEQ
M_SAMPLE = DEC_BATCH * DEC_SEQ
M_TOK = M_PROMPT + M_SAMPLE
N_COND = 1 + DEC_BATCH
COND_ROWS = 8

VMEM_LIMIT_V7X = 56 * 1024 * 1024

MXU_DTYPE = jnp.bfloat16

P_AQ, P_AK, P_AV = 0, 1024, 1280
P_SX, P_SZ, P_SB, P_SC = 1536, 2560, 3584, 3840
P_GQ, P_GK, P_GV, P_GG = 4096, 4608, 5120, 6144
P_LX, P_LG = 7168, 8192
P_W = 9216
S_W = 128


def _params(n_axes, vmem_bytes):
    return pltpu.CompilerParams(dimension_semantics=("arbitrary",) * n_axes,
                                vmem_limit_bytes=min(int(vmem_bytes), VMEM_LIMIT_V7X))


def _cond_row(tile_idx, tile_rows):
    row0 = tile_idx * tile_rows
    return jnp.where(row0 < M_PROMPT, 0, 1 + (row0 - M_PROMPT) // DEC_SEQ)


def _mod_kernel(c_ref, w_ref, b_ref, o_ref):
    c = c_ref[...]
    a = (c * jax.nn.sigmoid(c)).astype(MXU_DTYPE)
    w = w_ref[...].astype(MXU_DTYPE)
    o_ref[...] = jnp.dot(a, w, preferred_element_type=jnp.float32) + b_ref[...]


def _mod_table(cond, mod_w, mod_b):
    tn = 512
    n = N_MOD * D_MODEL
    return pl.pallas_call(
        _mod_kernel,
        grid=(DEPTH, n // tn),
        in_specs=[pl.BlockSpec((COND_ROWS, D_MODEL), lambda l, j: (0, 0)),
                  pl.BlockSpec((None, D_MODEL, tn), lambda l, j: (l, 0, j)),
                  pl.BlockSpec((None, 1, tn), lambda l, j: (l, 0, j))],
        out_specs=pl.BlockSpec((None, COND_ROWS, tn), lambda l, j: (l, 0, j)),
        out_shape=jax.ShapeDtypeStruct((DEPTH, COND_ROWS, n), jnp.float32),
        compiler_params=_params(2, 3 * D_MODEL * tn * 4 + (4 << 20)),
        name="mod_table",
    )(cond, mod_w, mod_b.reshape(DEPTH, 1, n))


def _mm_kernel(*refs, n_a):
    a_refs, w_ref, o_ref = refs[:n_a], refs[n_a], refs[n_a + 1]
    k0 = 0
    acc = None
    for a_ref in a_refs:
        kp = a_ref.shape[1]
        part = jnp.dot(a_ref[...], w_ref[k0:k0 + kp, :], preferred_element_type=jnp.float32)
        acc = part if acc is None else acc + part
        k0 += kp
    o_ref[...] = acc.astype(o_ref.dtype)


def _matmul(a_list, w, out_dtype, tm, tn, name):
    m = a_list[0].shape[0]
    k, n = w.shape
    assert sum(a.shape[1] for a in a_list) == k and m % tm == 0 and n % tn == 0
    in_specs = [pl.BlockSpec((tm, a.shape[1]), lambda i, j: (i, 0)) for a in a_list]
    in_specs.append(pl.BlockSpec((k, tn), lambda i, j: (0, j)))
    esz = jnp.dtype(MXU_DTYPE).itemsize
    vmem = 2 * (tm * k * esz + k * tn * esz + tm * tn * jnp.dtype(out_dtype).itemsize) + tm * tn * 8
    return pl.pallas_call(
        functools.partial(_mm_kernel, n_a=len(a_list)),
        grid=(m // tm, n // tn),
        in_specs=in_specs,
        out_specs=pl.BlockSpec((tm, tn), lambda i, j: (i, j)),
        out_shape=jax.ShapeDtypeStruct((m, n), out_dtype),
        compiler_params=_params(2, vmem + (4 << 20)),
        name=name,
    )(*a_list, w)


def _ffn_up_kernel(a_ref, wg_ref, wu_ref, o_ref):
    a = a_ref[...]
    g = jnp.dot(a, wg_ref[...], preferred_element_type=jnp.float32)
    u = jnp.dot(a, wu_ref[...], preferred_element_type=jnp.float32)
    o_ref[...] = (g * jax.nn.sigmoid(g) * u).astype(o_ref.dtype)


def _ffn_up(a, wg, wu):
    tm, tn = 2048, 256
    m = a.shape[0]
    esz = jnp.dtype(MXU_DTYPE).itemsize
    vmem = 2 * (tm * D_MODEL * esz + 2 * D_MODEL * tn * esz + tm * tn * esz) + 4 * tm * tn * 4
    return pl.pallas_call(
        _ffn_up_kernel,
        grid=(m // tm, D_FF // tn),
        in_specs=[pl.BlockSpec((tm, D_MODEL), lambda i, j: (i, 0)),
                  pl.BlockSpec((D_MODEL, tn), lambda i, j: (0, j)),
                  pl.BlockSpec((D_MODEL, tn), lambda i, j: (0, j))],
        out_specs=pl.BlockSpec((tm, tn), lambda i, j: (i, j)),
        out_shape=jax.ShapeDtypeStruct((m, D_FF), MXU_DTYPE),
        compiler_params=_params(2, vmem + (4 << 20)),
        name="ffn_up",
    )(a, wg, wu)


def _modulate_kernel(x_ref, sc_ref, sh_ref, o_ref, *, tm):
    r = _cond_row(pl.program_id(0), tm)
    sc = sc_ref[pl.ds(r, 1), :]
    sh = sh_ref[pl.ds(r, 1), :]
    o_ref[...] = (x_ref[...] * (1.0 + sc) + sh).astype(o_ref.dtype)


def _mod_spec(k):
    return pl.BlockSpec((COND_ROWS, D_MODEL), lambda i: (0, k))


def _modulate(x, mod, k_scale, k_shift):
    tm = 256
    m = x.shape[0]
    return pl.pallas_call(
        functools.partial(_modulate_kernel, tm=tm),
        grid=(m // tm,),
        in_specs=[pl.BlockSpec((tm, D_MODEL), lambda i: (i, 0)), _mod_spec(k_scale), _mod_spec(k_shift)],
        out_specs=pl.BlockSpec((tm, D_MODEL), lambda i: (i, 0)),
        out_shape=jax.ShapeDtypeStruct((m, D_MODEL), MXU_DTYPE),
        compiler_params=_params(1, 32 << 20),
        name="modulate",
    )(x, mod, mod)


def _resid_ln_kernel(*refs, tm, gate_scale, with_next):
    if with_next:
        x_ref, f_ref, g_ref, lg_ref, lb_ref, sc_ref, sh_ref, xo_ref, mo_ref = refs
    else:
        x_ref, f_ref, g_ref, lg_ref, lb_ref, xo_ref = refs
    r = _cond_row(pl.program_id(0), tm)
    g = g_ref[pl.ds(r, 1), :]
    y = ALPHA * x_ref[...] + (gate_scale * g) * f_ref[...]
    mu = jnp.mean(y, axis=-1, keepdims=True)
    yc = y - mu
    var = jnp.mean(yc * yc, axis=-1, keepdims=True)
    xn = yc * lax.rsqrt(var + LN_EPS) * lg_ref[...] + lb_ref[...]
    xo_ref[...] = xn
    if with_next:
        sc = sc_ref[pl.ds(r, 1), :]
        sh = sh_ref[pl.ds(r, 1), :]
        mo_ref[...] = (xn * (1.0 + sc) + sh).astype(mo_ref.dtype)


def _resid_ln(x, f, mod, k_gate, gate_scale, ln_g, ln_b, next_mod=None, k_scale=0, k_shift=0):
    tm = 256
    m = x.shape[0]
    with_next = next_mod is not None
    row = pl.BlockSpec((tm, D_MODEL), lambda i: (i, 0))
    vec = pl.BlockSpec((1, D_MODEL), lambda i: (0, 0))
    in_specs = [row, row, _mod_spec(k_gate), vec, vec]
    args = [x, f, mod, ln_g.reshape(1, D_MODEL), ln_b.reshape(1, D_MODEL)]
    out_specs = [row]
    out_shape = [jax.ShapeDtypeStruct((m, D_MODEL), jnp.float32)]
    if with_next:
        in_specs += [_mod_spec(k_scale), _mod_spec(k_shift)]
        args += [next_mod, next_mod]
        out_specs.append(row)
        out_shape.append(jax.ShapeDtypeStruct((m, D_MODEL), MXU_DTYPE))
    out = pl.pallas_call(
        functools.partial(_resid_ln_kernel, tm=tm, gate_scale=gate_scale, with_next=with_next),
        grid=(m // tm,),
        in_specs=in_specs,
        out_specs=out_specs,
        out_shape=out_shape,
        compiler_params=_params(1, 48 << 20),
        name="resid_ln",
    )(*args)
    return (out[0], out[1]) if with_next else (out[0], None)


def _rms(x, g):
    return x * lax.rsqrt(jnp.mean(x * x, axis=-1, keepdims=True) + RMS_EPS) * g


def _rev(t):
    return jnp.flip(t, axis=1)


def _dwconv(x, w, b):
    n_tok = x.shape[1]
    xp = jnp.pad(x, ((0, 0), (2, 1), (0, 0)))
    return sum(xp[:, j:j + n_tok] * w[j] for j in range(CONV_W)) + b


def _axial_rope(n_tok):
    rows = n_tok // GRID_W
    row = jnp.repeat(jnp.arange(rows, dtype=jnp.float32), GRID_W)
    col = jnp.tile(jnp.arange(GRID_W, dtype=jnp.float32), rows)
    n_freq = HEAD_DIM // 4
    inv = ROPE_THETA ** (-jnp.arange(n_freq, dtype=jnp.float32) / n_freq)
    ang = jnp.stack([row[:, None] * inv, col[:, None] * inv], axis=1)
    return jnp.cos(ang), jnp.sin(ang)


def _apply_rope(x, cos, sin):
    xf = x.reshape(*x.shape[:-1], 2, 2, HEAD_DIM // 4)
    x1, x2 = xf[..., 0, :], xf[..., 1, :]
    cs, sn = cos[None, :, None], sin[None, :, None]
    out = jnp.stack([x1 * cs - x2 * sn, x2 * cs + x1 * sn], axis=-2)
    return out.reshape(x.shape)


def _attention(q, k, v):
    bsz, n_q = q.shape[:2]
    rep = ATT_HEADS // ATT_KV_HEADS
    qb = q.reshape(bsz, n_q, ATT_KV_HEADS, rep, HEAD_DIM)
    s = jnp.einsum('bqgrd,bkgd->bgrqk', qb, k) * (HEAD_DIM ** -0.5)
    p = jax.nn.softmax(s, axis=-1)
    o = jnp.einsum('bgrqk,bkgd->bqgrd', p, v)
    return o.reshape(bsz, n_q, ATT_HEADS * HEAD_DIM)


def _ssd_scan(x, dt, a, bm, cm, h0):
    bsz, n_tok = x.shape[:2]
    L = 64
    n_chunk = n_tok // L
    rep = SSD_HEADS // SSD_GROUPS
    bh = jnp.repeat(bm, rep, axis=2)
    ch = jnp.repeat(cm, rep, axis=2)
    xdt = x * dt[..., None]
    la = dt * a
    mask = jnp.tril(jnp.ones((L, L), bool))[None, :, :, None]

    def chunks(t):
        return jnp.moveaxis(t.reshape(bsz, n_chunk, L, *t.shape[2:]), 1, 0)

    def step(h, inp):
        xdt_c, la_c, b_c, c_c = inp
        cum = jnp.cumsum(la_c, axis=1)
        seg = jnp.exp(jnp.where(mask, cum[:, :, None] - cum[:, None], -jnp.inf))
        scores = jnp.einsum('bihn,bjhn->bijh', c_c, b_c) * seg
        y = (jnp.einsum('bijh,bjhp->bihp', scores, xdt_c)
             + jnp.einsum('bihn,bhpn->bihp', c_c, h) * jnp.exp(cum)[..., None])
        to_end = jnp.exp(cum[:, -1:] - cum)[..., None]
        h_new = (h * jnp.exp(cum[:, -1])[:, :, None, None]
                 + jnp.einsum('bjhn,bjhp->bhpn', b_c * to_end, xdt_c))
        return h_new, y

    h_last, ys = lax.scan(step, h0, (chunks(xdt), chunks(la), chunks(bh), chunks(ch)))
    return jnp.moveaxis(ys, 0, 1).reshape(x.shape), h_last


def _gla_scan(q, k, v, g, s0):
    bsz, n_tok = q.shape[:2]
    L = 64
    n_chunk = n_tok // L
    mask = jnp.tril(jnp.ones((L, L), bool))[None, :, :, None, None]

    def chunks(t):
        return jnp.moveaxis(t.reshape(bsz, n_chunk, L, *t.shape[2:]), 1, 0)

    def step(s, inp):
        q_c, k_c, v_c, g_c = inp
        b = jnp.cumsum(g_c, axis=1)
        dec = jnp.exp(jnp.where(mask, b[:, :, None] - b[:, None], -jnp.inf))
        att = jnp.einsum('bihk,bjhk,bijhk->bijh', q_c, k_c, dec)
        o = (jnp.einsum('bijh,bjhv->bihv', att, v_c)
             + jnp.einsum('bihk,bhkv->bihv', q_c * jnp.exp(b), s))
        b_end = b[:, -1]
        s_new = (s * jnp.exp(b_end)[..., None]
                 + jnp.einsum('bjhk,bjhv->bhkv', k_c * jnp.exp(b_end[:, None] - b), v_c))
        return s_new, o

    s_last, os_ = lax.scan(step, s0, (chunks(q), chunks(k), chunks(v), chunks(g)))
    return jnp.moveaxis(os_, 0, 1).reshape(bsz, n_tok, v.shape[2], v.shape[3]), s_last


def _rg_lru(x, wa, ba, wx, bx, lam, h0):
    bsz, n_tok, ch = x.shape
    xb = x.reshape(bsz, n_tok, LRU_BLOCKS, LRU_BW)
    r = jax.nn.sigmoid(jnp.einsum('btnc,ncd->btnd', xb, wa).reshape(bsz, n_tok, ch) + ba)
    i = jax.nn.sigmoid(jnp.einsum('btnc,ncd->btnd', xb, wx).reshape(bsz, n_tok, ch) + bx)
    log_a = -LRU_C * r * jax.nn.softplus(-lam)
    a = jnp.exp(log_a)
    u = jnp.sqrt(-jnp.expm1(2.0 * log_a)) * i * x
    u = u.at[:, 0].add(a[:, 0] * h0)

    def combine(left, right):
        return left[0] * right[0], right[0] * left[1] + right[1]

    _, h = lax.associative_scan(combine, (a, u), axis=1)
    return h, h[:, -1]


def _jnp_mixers(p, s, bsz, n_tok, ctx, lp):
    p = p.reshape(bsz, n_tok, P_W)
    s = s.reshape(bsz, n_tok, S_W)

    def col(o, w):
        return p[..., o:o + w]

    aq, ak, av = col(P_AQ, 1024), col(P_AK, 256), col(P_AV, 256)
    sx, sz, sb, sc = col(P_SX, 1024), col(P_SZ, 1024), col(P_SB, 256), col(P_SC, 256)
    gq, gk, gv, gg = col(P_GQ, 512), col(P_GK, 512), col(P_GV, 1024), col(P_GG, 1024)
    lx, lg = col(P_LX, 1024), col(P_LG, 1024)
    sdt, glr = s[..., :32], s[..., 32:64]

    q = _rms(aq.reshape(bsz, n_tok, ATT_HEADS, HEAD_DIM), lp['q_norm'])
    k = _rms(ak.reshape(bsz, n_tok, ATT_KV_HEADS, HEAD_DIM), lp['k_norm'])
    v = av.reshape(bsz, n_tok, ATT_KV_HEADS, HEAD_DIM)
    if ctx is None:
        o_att = _attention(q, k, v)
        ssd0 = jnp.zeros((bsz, 2, SSD_HEADS, SSD_HEADDIM, SSD_STATE), jnp.float32)
        gla0 = jnp.zeros((bsz, 2, GLA_HEADS, GLA_DK, GLA_DV), jnp.float32)
        lru0 = jnp.zeros((bsz, 2, LRU_W), jnp.float32)
    else:
        ctx_k, ctx_v, ssd0, gla0, lru0 = ctx
        cos, sin = _axial_rope(n_tok)
        k_all = jnp.concatenate([_apply_rope(k, cos, sin), ctx_k], axis=1)
        v_all = jnp.concatenate([v, ctx_v], axis=1)
        o_att = _attention(_apply_rope(q, cos, sin), k_all, v_all)

    xbc = jax.nn.silu(_dwconv(jnp.concatenate([sx, sb, sc], axis=-1), lp['ssd_conv_w'], lp['ssd_conv_b']))
    xs = xbc[..., :GROUP_W].reshape(bsz, n_tok, SSD_HEADS, SSD_HEADDIM)
    bm = xbc[..., GROUP_W:GROUP_W + 256].reshape(bsz, n_tok, SSD_GROUPS, SSD_STATE)
    cm = xbc[..., GROUP_W + 256:].reshape(bsz, n_tok, SSD_GROUPS, SSD_STATE)
    dt = jax.nn.softplus(sdt.reshape(bsz, n_tok, 2, SSD_HEADS) + lp['ssd_dt_bias'])
    a = -jnp.exp(lp['ssd_a_log'])
    y_f, ssd_f = _ssd_scan(xs, dt[:, :, 0], a[0], bm, cm, ssd0[:, 0])
    y_b, ssd_b = _ssd_scan(_rev(xs), _rev(dt[:, :, 1]), a[1], _rev(bm), _rev(cm), ssd0[:, 1])
    y = y_f + _rev(y_b) + xs * (lp['ssd_d'][0] + lp['ssd_d'][1])[:, None]
    o_ssd = _rms(y.reshape(bsz, n_tok, GROUP_W) * jax.nn.silu(sz), lp['ssd_norm_w'])

    gq = gq.reshape(bsz, n_tok, GLA_HEADS, GLA_DK) * (GLA_DK ** -0.5)
    gk = gk.reshape(bsz, n_tok, GLA_HEADS, GLA_DK)
    gv = gv.reshape(bsz, n_tok, GLA_HEADS, GLA_DV)
    gate = jnp.einsum('btdr,drk->btdk', glr.reshape(bsz, n_tok, 2, GLA_RANK), lp['gla_gate_w']) + lp['gla_gate_b']
    glog = (jax.nn.log_sigmoid(gate) / GLA_NORMALIZER).reshape(bsz, n_tok, 2, GLA_HEADS, GLA_DK)
    o_f, gla_f = _gla_scan(gq, gk, gv, glog[:, :, 0], gla0[:, 0])
    o_b, gla_b = _gla_scan(_rev(gq), _rev(gk), _rev(gv), _rev(glog[:, :, 1]), gla0[:, 1])
    o_gla = _rms(o_f + _rev(o_b), lp['gla_norm_w']).reshape(bsz, n_tok, GROUP_W) * jax.nn.silu(gg)

    xl = _dwconv(lx, lp['lru_conv_w'], lp['lru_conv_b'])
    h_f, lru_f = _rg_lru(xl, lp['lru_wa'][0], lp['lru_ba'][0], lp['lru_wx'][0], lp['lru_bx'][0], lp['lru_lam'][0], lru0[:, 0])
    h_b, lru_b = _rg_lru(_rev(xl), lp['lru_wa'][1], lp['lru_ba'][1], lp['lru_wx'][1], lp['lru_bx'][1], lp['lru_lam'][1], lru0[:, 1])
    o_lru = (h_f + _rev(h_b)) * jax.nn.gelu(lg)

    outs = [o.reshape(bsz * n_tok, GROUP_W) for o in (o_att, o_ssd, o_gla, o_lru)]
    new_ctx = None
    if ctx is None:
        new_ctx = (k, v, jnp.stack([ssd_f, ssd_b], axis=1), jnp.stack([gla_f, gla_b], axis=1),
                   jnp.stack([lru_f, lru_b], axis=1))
    return outs, new_ctx


def _reorder_w_in(w):
    w = w.astype(MXU_DTYPE)
    main = jnp.concatenate([w[:, 0:4096], w[:, 4128:6176], w[:, 6208:9280]], axis=1)
    small = jnp.concatenate([w[:, 4096:4128], w[:, 6176:6208],
                             jnp.zeros((D_MODEL, S_W - 64), MXU_DTYPE)], axis=1)
    return main, small


def kernel(x_prompt, x_sample, c, cache_attn_k, cache_attn_v, state_ssd, state_gla, state_lru, c_ctx,
           mod_w, mod_b, ln_g, ln_b, ffn_w_gate, ffn_w_up, ffn_w_down, w_in, w_out, q_norm, k_norm,
           ssd_conv_w, ssd_conv_b, ssd_a_log, ssd_dt_bias, ssd_d, ssd_norm_w,
           gla_gate_w, gla_gate_b, gla_norm_w,
           lru_conv_w, lru_conv_b, lru_wa, lru_ba, lru_wx, lru_bx, lru_lam):
    cond = jnp.concatenate([c_ctx[None], c, jnp.zeros((COND_ROWS - N_COND, D_MODEL), jnp.float32)], axis=0)
    mod_all = _mod_table(cond, mod_w, mod_b)

    x = jnp.concatenate([x_prompt.reshape(M_PROMPT, D_MODEL), x_sample.reshape(M_SAMPLE, D_MODEL)], axis=0)
    xm = _modulate(x, mod_all[0], 1, 0)
    ctx_out = []
    for l in range(DEPTH):
        mod = mod_all[l]
        lp = dict(q_norm=q_norm[l], k_norm=k_norm[l], ssd_conv_w=ssd_conv_w[l], ssd_conv_b=ssd_conv_b[l],
                  ssd_a_log=ssd_a_log[l], ssd_dt_bias=ssd_dt_bias[l], ssd_d=ssd_d[l], ssd_norm_w=ssd_norm_w[l],
                  gla_gate_w=gla_gate_w[l], gla_gate_b=gla_gate_b[l], gla_norm_w=gla_norm_w[l],
                  lru_conv_w=lru_conv_w[l], lru_conv_b=lru_conv_b[l], lru_wa=lru_wa[l], lru_ba=lru_ba[l],
                  lru_wx=lru_wx[l], lru_bx=lru_bx[l], lru_lam=lru_lam[l])
        h = _ffn_up(xm, ffn_w_gate[l, 0].astype(MXU_DTYPE), ffn_w_up[l, 0].astype(MXU_DTYPE))
        f = _matmul([h], ffn_w_down[l, 0].astype(MXU_DTYPE), jnp.float32, 512, 512, "ffn_down")
        x, xm = _resid_ln(x, f, mod, 2, 0.5, ln_g[l, 0], ln_b[l, 0], mod, 4, 3)
        w_main, w_small = _reorder_w_in(w_in[l])
        p = _matmul([xm], w_main, jnp.float32, 1024, 1024, "proj_in")
        s = _matmul([xm], w_small, jnp.float32, 1024, S_W, "proj_in_small")
        outs_p, new_ctx = _jnp_mixers(p[:M_PROMPT], s[:M_PROMPT], BATCH, SEQ, None, lp)
        cached = (cache_attn_k[:, l], cache_attn_v[:, l], state_ssd[:, l], state_gla[:, l], state_lru[:, l])
        outs_s, _ = _jnp_mixers(p[M_PROMPT:], s[M_PROMPT:], DEC_BATCH, DEC_SEQ, cached, lp)
        ctx_out.append(new_ctx)
        mix = [jnp.concatenate([a, b], axis=0).astype(MXU_DTYPE) for a, b in zip(outs_p, outs_s)]
        m_out = _matmul(mix, w_out[l].astype(MXU_DTYPE), jnp.float32, 1024, 1024, "proj_out")
        x, xm = _resid_ln(x, m_out, mod, 5, 1.0, ln_g[l, 1], ln_b[l, 1], mod, 7, 6)
        h = _ffn_up(xm, ffn_w_gate[l, 1].astype(MXU_DTYPE), ffn_w_up[l, 1].astype(MXU_DTYPE))
        f = _matmul([h], ffn_w_down[l, 1].astype(MXU_DTYPE), jnp.float32, 512, 512, "ffn_down")
        if l + 1 < DEPTH:
            x, xm = _resid_ln(x, f, mod, 8, 0.5, ln_g[l, 2], ln_b[l, 2], mod_all[l + 1], 1, 0)
        else:
            x, _ = _resid_ln(x, f, mod, 8, 0.5, ln_g[l, 2], ln_b[l, 2])

    y_prompt = x[:M_PROMPT].reshape(BATCH, SEQ, D_MODEL)
    y_sample = x[M_PROMPT:].reshape(DEC_BATCH, DEC_SEQ, D_MODEL)
    new_k, new_v, new_ssd, new_gla, new_lru = (jnp.stack([s_[i] for s_ in ctx_out], axis=1) for i in range(5))
    return (y_prompt, y_sample, new_k, new_v, new_ssd, new_gla, new_lru)
```

```python
import functools
import math

import numpy as np
import jax
import jax.numpy as jnp
from jax import lax
from jax.experimental import pallas as pl
from jax.experimental.pallas import tpu as pltpu

D_MODEL = 4096
BATCH = 16
SEQ = 256
DEPTH = 2
DEC_BATCH = 4
DEC_SEQ = 2048
PAST_LEN = 256
GRID_W = 64
GROUP_W = 1024
HEAD_DIM = 128
ATT_HEADS = 8
ATT_KV_HEADS = 2
ROPE_THETA = 10000.0
SSD_HEADDIM = 64
SSD_HEADS = 16
SSD_STATE = 128
SSD_GROUPS = 2
CONV_W = 4
GLA_HEADS = 4
GLA_DK = 128
GLA_DV = 256
GLA_RANK = 16
GLA_NORMALIZER = 16.0
LRU_W = 1024
LRU_BLOCKS = 8
LRU_BW = 128
LRU_C = 8.0
D_FF = 11008
N_MOD = 9
LN_EPS = 1e-5
RMS_EPS = 1e-6
ALPHA = (2.0 * DEPTH) ** 0.25

M_PROMPT = BATCH * SEQ
M_SAMPLE = DEC_BATCH * DEC_SEQ
M_TOK = M_PROMPT + M_SAMPLE
N_COND = 1 + DEC_BATCH
COND_ROWS = 8

VMEM_LIMIT_V7X = 56 * 1024 * 1024

MXU_DTYPE = jnp.bfloat16

P_AQ, P_SX, P_SZ, P_GV, P_GG, P_LX, P_LG = 0, 1024, 2048, 3072, 4096, 5120, 6144
P_GQ, P_GK = 7168, 7680
P_AK, P_AV, P_SB, P_SC = 8192, 8448, 8704, 8960
P_W = 9216
S_W = 128


def _params(n_axes, vmem_bytes):
    return pltpu.CompilerParams(dimension_semantics=("arbitrary",) * n_axes,
                                vmem_limit_bytes=min(int(vmem_bytes), VMEM_LIMIT_V7X))


def _cond_row(tile_idx, tile_rows):
    row0 = tile_idx * tile_rows
    return jnp.where(row0 < M_PROMPT, 0, 1 + (row0 - M_PROMPT) // DEC_SEQ)


def _mod_kernel(c_ref, w_ref, b_ref, o_ref):
    c = c_ref[...]
    a = (c * jax.nn.sigmoid(c)).astype(MXU_DTYPE)
    w = w_ref[...].astype(MXU_DTYPE)
    o_ref[...] = jnp.dot(a, w, preferred_element_type=jnp.float32) + b_ref[...]


def _mod_table(cond, mod_w, mod_b):
    tn = 512
    n = N_MOD * D_MODEL
    return pl.pallas_call(
        _mod_kernel,
        grid=(DEPTH, n // tn),
        in_specs=[pl.BlockSpec((COND_ROWS, D_MODEL), lambda l, j: (0, 0)),
                  pl.BlockSpec((None, D_MODEL, tn), lambda l, j: (l, 0, j)),
                  pl.BlockSpec((None, 1, tn), lambda l, j: (l, 0, j))],
        out_specs=pl.BlockSpec((None, COND_ROWS, tn), lambda l, j: (l, 0, j)),
        out_shape=jax.ShapeDtypeStruct((DEPTH, COND_ROWS, n), jnp.float32),
        compiler_params=_params(2, 3 * D_MODEL * tn * 4 + (4 << 20)),
        name="mod_table",
    )(cond, mod_w, mod_b.reshape(DEPTH, 1, n))


def _mm_kernel(*refs, n_a):
    a_refs, w_ref, o_ref = refs[:n_a], refs[n_a], refs[n_a + 1]
    k0 = 0
    acc = None
    for a_ref in a_refs:
        kp = a_ref.shape[1]
        part = jnp.dot(a_ref[...], w_ref[k0:k0 + kp, :], preferred_element_type=jnp.float32)
        acc = part if acc is None else acc + part
        k0 += kp
    o_ref[...] = acc.astype(o_ref.dtype)


def _matmul(a_list, w, out_dtype, tm, tn, name):
    m = a_list[0].shape[0]
    k, n = w.shape
    assert sum(a.shape[1] for a in a_list) == k and m % tm == 0 and n % tn == 0
    in_specs = [pl.BlockSpec((tm, a.shape[1]), lambda i, j: (i, 0)) for a in a_list]
    in_specs.append(pl.BlockSpec((k, tn), lambda i, j: (0, j)))
    esz = jnp.dtype(MXU_DTYPE).itemsize
    vmem = 2 * (tm * k * esz + k * tn * esz + tm * tn * jnp.dtype(out_dtype).itemsize) + tm * tn * 8
    return pl.pallas_call(
        functools.partial(_mm_kernel, n_a=len(a_list)),
        grid=(m // tm, n // tn),
        in_specs=in_specs,
        out_specs=pl.BlockSpec((tm, tn), lambda i, j: (i, j)),
        out_shape=jax.ShapeDtypeStruct((m, n), out_dtype),
        compiler_params=_params(2, vmem + (4 << 20)),
        name=name,
    )(*a_list, w)


def _ffn_up_kernel(a_ref, wg_ref, wu_ref, o_ref):
    a = a_ref[...]
    g = jnp.dot(a, wg_ref[...], preferred_element_type=jnp.float32)
    u = jnp.dot(a, wu_ref[...], preferred_element_type=jnp.float32)
    o_ref[...] = (g * jax.nn.sigmoid(g) * u).astype(o_ref.dtype)


def _ffn_up(a, wg, wu):
    tm, tn = 2048, 256
    m = a.shape[0]
    esz = jnp.dtype(MXU_DTYPE).itemsize
    vmem = 2 * (tm * D_MODEL * esz + 2 * D_MODEL * tn * esz + tm * tn * esz) + 4 * tm * tn * 4
    return pl.pallas_call(
        _ffn_up_kernel,
        grid=(m // tm, D_FF // tn),
        in_specs=[pl.BlockSpec((tm, D_MODEL), lambda i, j: (i, 0)),
                  pl.BlockSpec((D_MODEL, tn), lambda i, j: (0, j)),
                  pl.BlockSpec((D_MODEL, tn), lambda i, j: (0, j))],
        out_specs=pl.BlockSpec((tm, tn), lambda i, j: (i, j)),
        out_shape=jax.ShapeDtypeStruct((m, D_FF), MXU_DTYPE),
        compiler_params=_params(2, vmem + (4 << 20)),
        name="ffn_up",
    )(a, wg, wu)


def _modulate_kernel(x_ref, sc_ref, sh_ref, o_ref, *, tm):
    r = _cond_row(pl.program_id(0), tm)
    sc = sc_ref[pl.ds(r, 1), :]
    sh = sh_ref[pl.ds(r, 1), :]
    o_ref[...] = (x_ref[...] * (1.0 + sc) + sh).astype(o_ref.dtype)


def _mod_spec(k):
    return pl.BlockSpec((COND_ROWS, D_MODEL), lambda i: (0, k))


def _modulate(x, mod, k_scale, k_shift):
    tm = 256
    m = x.shape[0]
    return pl.pallas_call(
        functools.partial(_modulate_kernel, tm=tm),
        grid=(m // tm,),
        in_specs=[pl.BlockSpec((tm, D_MODEL), lambda i: (i, 0)), _mod_spec(k_scale), _mod_spec(k_shift)],
        out_specs=pl.BlockSpec((tm, D_MODEL), lambda i: (i, 0)),
        out_shape=jax.ShapeDtypeStruct((m, D_MODEL), MXU_DTYPE),
        compiler_params=_params(1, 32 << 20),
        name="modulate",
    )(x, mod, mod)


def _resid_ln_kernel(*refs, tm, gate_scale, with_next):
    if with_next:
        x_ref, f_ref, g_ref, lg_ref, lb_ref, sc_ref, sh_ref, xo_ref, mo_ref = refs
    else:
        x_ref, f_ref, g_ref, lg_ref, lb_ref, xo_ref = refs
    r = _cond_row(pl.program_id(0), tm)
    g = g_ref[pl.ds(r, 1), :]
    y = ALPHA * x_ref[...] + (gate_scale * g) * f_ref[...]
    mu = jnp.mean(y, axis=-1, keepdims=True)
    yc = y - mu
    var = jnp.mean(yc * yc, axis=-1, keepdims=True)
    xn = yc * lax.rsqrt(var + LN_EPS) * lg_ref[...] + lb_ref[...]
    xo_ref[...] = xn
    if with_next:
        sc = sc_ref[pl.ds(r, 1), :]
        sh = sh_ref[pl.ds(r, 1), :]
        mo_ref[...] = (xn * (1.0 + sc) + sh).astype(mo_ref.dtype)


def _resid_ln(x, f, mod, k_gate, gate_scale, ln_g, ln_b, next_mod=None, k_scale=0, k_shift=0):
    tm = 256
    m = x.shape[0]
    with_next = next_mod is not None
    row = pl.BlockSpec((tm, D_MODEL), lambda i: (i, 0))
    vec = pl.BlockSpec((1, D_MODEL), lambda i: (0, 0))
    in_specs = [row, row, _mod_spec(k_gate), vec, vec]
    args = [x, f, mod, ln_g.reshape(1, D_MODEL), ln_b.reshape(1, D_MODEL)]
    out_specs = [row]
    out_shape = [jax.ShapeDtypeStruct((m, D_MODEL), jnp.float32)]
    if with_next:
        in_specs += [_mod_spec(k_scale), _mod_spec(k_shift)]
        args += [next_mod, next_mod]
        out_specs.append(row)
        out_shape.append(jax.ShapeDtypeStruct((m, D_MODEL), MXU_DTYPE))
    out = pl.pallas_call(
        functools.partial(_resid_ln_kernel, tm=tm, gate_scale=gate_scale, with_next=with_next),
        grid=(m // tm,),
        in_specs=in_specs,
        out_specs=out_specs,
        out_shape=out_shape,
        compiler_params=_params(1, 48 << 20),
        name="resid_ln",
    )(*args)
    return (out[0], out[1]) if with_next else (out[0], None)


def _row_iota(shape):
    return lax.broadcasted_iota(jnp.int32, shape, 0)


def _lane_iota(shape):
    return lax.broadcasted_iota(jnp.int32, shape, 1)


def _shift_rows(x, s, fill, up=False):
    n = x.shape[0]
    t = _row_iota(x.shape)
    if up:
        return jnp.where(t < n - s, pltpu.roll(x, n - s, 0), fill)
    return jnp.where(t >= s, pltpu.roll(x, s, 0), fill)


def _softplus(z):
    return jnp.maximum(z, 0.0) + jnp.log1p(jnp.exp(-jnp.abs(z)))


def _silu(z):
    return z * jax.nn.sigmoid(z)


def _dot(a, b):
    return jnp.dot(a, b, preferred_element_type=jnp.float32)


def _dot_nt(a, b):
    return lax.dot_general(a, b, (((1,), (1,)), ((), ())), preferred_element_type=jnp.float32)


def _dot_tn(a, b):
    return lax.dot_general(a, b, (((0,), (0,)), ((), ())), preferred_element_type=jnp.float32)


def _split3(x):
    hi = x.astype(MXU_DTYPE)
    r1 = x - hi.astype(jnp.float32)
    mid = r1.astype(MXU_DTYPE)
    lo = (r1 - mid.astype(jnp.float32)).astype(MXU_DTYPE)
    return hi, mid, lo


def _dot01_right(x, e):
    hi, mid, lo = _split3(x)
    return _dot(hi, e) + _dot(mid, e) + _dot(lo, e)


def _dot01_left(e, x):
    hi, mid, lo = _split3(x)
    return _dot(e, hi) + _dot(e, mid) + _dot(e, lo)


def _conv4(x, w_ref, b_ref):
    acc = _shift_rows(x, 2, 0.0) * w_ref[0:1, :]
    acc = acc + _shift_rows(x, 1, 0.0) * w_ref[1:2, :]
    acc = acc + x * w_ref[2:3, :]
    acc = acc + _shift_rows(x, 1, 0.0, up=True) * w_ref[3:4, :]
    return acc + b_ref[...]


def _rms_rows(x, w):
    return x * lax.rsqrt(jnp.mean(x * x, axis=-1, keepdims=True) + RMS_EPS) * w


def _rope_tables(n_tok):
    rows = n_tok // GRID_W
    row = jnp.repeat(jnp.arange(rows, dtype=jnp.float32), GRID_W)
    col = jnp.tile(jnp.arange(GRID_W, dtype=jnp.float32), rows)
    n_freq = HEAD_DIM // 4
    inv = ROPE_THETA ** (-jnp.arange(n_freq, dtype=jnp.float32) / n_freq)
    ar, ac = row[:, None] * inv, col[:, None] * inv
    cos = jnp.concatenate([jnp.cos(ar), jnp.cos(ar), jnp.cos(ac), jnp.cos(ac)], axis=1)
    sin = jnp.concatenate([-jnp.sin(ar), jnp.sin(ar), -jnp.sin(ac), jnp.sin(ac)], axis=1)
    return cos, sin


def _rope(x, cos, sin):
    quarter = HEAD_DIM // 4
    lane = _lane_iota(x.shape)
    partner = jnp.where(lane % (2 * quarter) < quarter,
                        pltpu.roll(x, HEAD_DIM - quarter, 1), pltpu.roll(x, quarter, 1))
    return x * cos + partner * sin


def _attn_kernel(*refs, n_tok, tq, n_ctx, rope):
    if rope:
        (q_ref, k_ref, v_ref, qn_ref, kn_ref, cq_ref, sq_ref, ck_ref, sk_ref, xk_ref, xv_ref,
         o_ref, ks_ref, vs_ref) = refs
    else:
        q_ref, k_ref, v_ref, qn_ref, kn_ref, o_ref, ko_ref, ks_ref, vs_ref = refs
    rep = ATT_HEADS // ATT_KV_HEADS

    @pl.when(pl.program_id(2) == 0)
    def _():
        kn = _rms_rows(k_ref[...], kn_ref[...])
        if rope:
            kn = _rope(kn, ck_ref[...], sk_ref[...])
            ks_ref[n_tok:n_tok + n_ctx, :] = xk_ref[...].astype(MXU_DTYPE)
            vs_ref[n_tok:n_tok + n_ctx, :] = xv_ref[...].astype(MXU_DTYPE)
        else:
            ko_ref[...] = kn
        ks_ref[0:n_tok, :] = kn.astype(MXU_DTYPE)
        vs_ref[0:n_tok, :] = v_ref[...].astype(MXU_DTYPE)

    heads = []
    for r in range(rep):
        qh = _rms_rows(q_ref[:, r * HEAD_DIM:(r + 1) * HEAD_DIM], qn_ref[...])
        if rope:
            qh = _rope(qh, cq_ref[...], sq_ref[...])
        heads.append(qh.astype(MXU_DTYPE))
    qs = jnp.concatenate(heads, axis=0)
    s = _dot_nt(qs, ks_ref[...]) * (HEAD_DIM ** -0.5)
    e = jnp.exp(s - jnp.max(s, axis=-1, keepdims=True))
    o = _dot(e.astype(MXU_DTYPE), vs_ref[...]) / jnp.sum(e, axis=-1, keepdims=True)
    for r in range(rep):
        o_ref[:, r * HEAD_DIM:(r + 1) * HEAD_DIM] = o[r * tq:(r + 1) * tq].astype(o_ref.dtype)


def _attention_call(p, row_off, bsz, n_tok, q_norm, k_norm, ctx_kv, out_prev):
    m = p.shape[0]
    rope = ctx_kv is not None
    tq = 128 if rope else n_tok
    nq = n_tok // tq
    rep = ATT_HEADS // ATT_KV_HEADS
    qw = rep * HEAD_DIM
    n_ctx = ctx_kv[0].shape[1] if rope else 0
    assert row_off % n_tok == 0 and n_tok % tq == 0
    rb, sb = row_off // tq, row_off // n_tok

    def seq_spec(col0):
        return pl.BlockSpec((n_tok, HEAD_DIM), lambda b, g, i: (sb + b, col0 // HEAD_DIM + g))

    vec = pl.BlockSpec((1, HEAD_DIM), lambda b, g, i: (0, 0))
    in_specs = [pl.BlockSpec((tq, qw), lambda b, g, i: (rb + b * nq + i, P_AQ // qw + g)),
                seq_spec(P_AK), seq_spec(P_AV), vec, vec]
    args = [p, p, p, q_norm.reshape(1, HEAD_DIM), k_norm.reshape(1, HEAD_DIM)]
    out_block = pl.BlockSpec((tq, qw), lambda b, g, i: (rb + b * nq + i, g))
    out_specs = [out_block]
    out_shape = [jax.ShapeDtypeStruct((m, GROUP_W), MXU_DTYPE)]
    aliases = {}
    if rope:
        cos, sin = _rope_tables(n_tok)
        in_specs += [pl.BlockSpec((tq, HEAD_DIM), lambda b, g, i: (i, 0))] * 2
        in_specs += [pl.BlockSpec((n_tok, HEAD_DIM), lambda b, g, i: (0, 0))] * 2
        in_specs += [pl.BlockSpec((None, n_ctx, HEAD_DIM), lambda b, g, i: (b, 0, g))] * 2
        args += [cos, sin, cos, sin, ctx_kv[0], ctx_kv[1]]
    else:
        out_specs.append(pl.BlockSpec((n_tok, HEAD_DIM), lambda b, g, i: (b, g)))
        out_shape.append(jax.ShapeDtypeStruct((bsz * n_tok, ATT_KV_HEADS * HEAD_DIM), jnp.float32))
    if out_prev is not None:
        in_specs.append(pl.BlockSpec(memory_space=pl.ANY))
        args.append(out_prev)
        aliases = {len(args) - 1: 0}
    n_keys = n_tok + n_ctx

    def body(*refs):
        if out_prev is not None:
            refs = refs[:len(args) - 1] + refs[len(args):]
        _attn_kernel(*refs, n_tok=n_tok, tq=tq, n_ctx=n_ctx, rope=rope)

    out = pl.pallas_call(
        body,
        grid=(bsz, ATT_KV_HEADS, nq),
        in_specs=in_specs,
        out_specs=out_specs,
        out_shape=out_shape,
        scratch_shapes=[pltpu.VMEM((n_keys, HEAD_DIM), MXU_DTYPE), pltpu.VMEM((n_keys, HEAD_DIM), MXU_DTYPE)],
        input_output_aliases=aliases,
        compiler_params=_params(3, 48 << 20),
        name="attention",
    )(*args)
    return out


def _lru_kernel(*refs, n_tok, with_state):
    if with_state:
        (x_ref, g_ref, cw_ref, cb_ref, wa_ref, ba_ref, wx_ref, bx_ref, lam_ref, h0_ref, o_ref) = refs
    else:
        (x_ref, g_ref, cw_ref, cb_ref, wa_ref, ba_ref, wx_ref, bx_ref, lam_ref, o_ref, hT_ref) = refs
    xl = _conv4(x_ref[...], cw_ref, cb_ref)
    xb = xl.astype(MXU_DTYPE)
    t = _row_iota(xl.shape)
    h_sum = None
    for d in range(2):
        up = d == 1
        r = jax.nn.sigmoid(_dot(xb, wa_ref[d]) + ba_ref[d])
        i = jax.nn.sigmoid(_dot(xb, wx_ref[d]) + bx_ref[d])
        log_a = -LRU_C * r * _softplus(-lam_ref[d])
        a = jnp.exp(log_a)
        th = jnp.tanh(log_a)
        u = jnp.sqrt(-2.0 * th / (1.0 - th)) * i * xl
        if with_state:
            first = (t == n_tok - 1) if up else (t == 0)
            u = jnp.where(first, u + a * h0_ref[d], u)
        s = 1
        while s < n_tok:
            u = u + a * _shift_rows(u, s, 0.0, up=up)
            a = a * _shift_rows(a, s, 1.0, up=up)
            s *= 2
        h_sum = u if h_sum is None else h_sum + u
        if not with_state:
            hT_ref[d] = u[0:1, :] if up else u[n_tok - 1:n_tok, :]
    o_ref[...] = (h_sum * jax.nn.gelu(g_ref[...])).astype(o_ref.dtype)


def _lru_call(p, row_off, bsz, n_tok, lp, h0, out_prev):
    m = p.shape[0]
    with_state = h0 is not None
    sb = row_off // n_tok
    assert row_off % n_tok == 0

    def seq_spec(col0):
        return pl.BlockSpec((n_tok, LRU_BW), lambda b, n: (sb + b, col0 // LRU_BW + n))

    def par2(shape):
        return pl.BlockSpec((2,) + shape + (LRU_BW,), lambda b, n: (0,) + (0,) * len(shape) + (n,))

    in_specs = [seq_spec(P_LX), seq_spec(P_LG),
                pl.BlockSpec((CONV_W, LRU_BW), lambda b, n: (0, n)),
                pl.BlockSpec((1, LRU_BW), lambda b, n: (0, n)),
                pl.BlockSpec((2, None, LRU_BW, LRU_BW), lambda b, n: (0, n, 0, 0)), par2((1,)),
                pl.BlockSpec((2, None, LRU_BW, LRU_BW), lambda b, n: (0, n, 0, 0)), par2((1,)), par2((1,))]
    args = [p, p, lp['lru_conv_w'], lp['lru_conv_b'].reshape(1, LRU_W),
            lp['lru_wa'].astype(MXU_DTYPE), lp['lru_ba'].reshape(2, 1, LRU_W),
            lp['lru_wx'].astype(MXU_DTYPE), lp['lru_bx'].reshape(2, 1, LRU_W), lp['lru_lam'].reshape(2, 1, LRU_W)]
    out_specs = [pl.BlockSpec((n_tok, LRU_BW), lambda b, n: (sb + b, n))]
    out_shape = [jax.ShapeDtypeStruct((m, GROUP_W), MXU_DTYPE)]
    if with_state:
        in_specs.append(pl.BlockSpec((None, 2, 1, LRU_BW), lambda b, n: (b, 0, 0, n)))
        args.append(h0.reshape(bsz, 2, 1, LRU_W))
    else:
        out_specs.append(pl.BlockSpec((None, 2, 1, LRU_BW), lambda b, n: (b, 0, 0, n)))
        out_shape.append(jax.ShapeDtypeStruct((bsz, 2, 1, LRU_W), jnp.float32))
    n_in = len(args)
    aliases = {}
    if out_prev is not None:
        in_specs.append(pl.BlockSpec(memory_space=pl.ANY))
        args.append(out_prev)
        aliases = {n_in: 0}

    def body(*refs):
        if out_prev is not None:
            refs = refs[:n_in] + refs[n_in + 1:]
        _lru_kernel(*refs, n_tok=n_tok, with_state=with_state)

    return pl.pallas_call(
        body,
        grid=(bsz, LRU_BLOCKS),
        in_specs=in_specs,
        out_specs=out_specs,
        out_shape=out_shape,
        input_output_aliases=aliases,
        compiler_params=_params(2, 48 << 20),
        name="rg_lru",
    )(*args)


SSD_L = 128
SSD_GH = SSD_HEADS // SSD_GROUPS
SSD_GW = SSD_GH * SSD_HEADDIM


def _ssd_kernel(*refs, n_tok, with_state):
    if with_state:
        (x_ref, b_ref, c_ref, s_ref, cwx_ref, cbx_ref, cwb_ref, cbb_ref, cwc_ref, cbc_ref, dtb_ref, alog_ref,
         dskip_ref, h0_ref, y_ref, xc_s, xdt_s, bc_s, cc_s, la_s, h_s) = refs
    else:
        (x_ref, b_ref, c_ref, s_ref, cwx_ref, cbx_ref, cwb_ref, cbb_ref, cwc_ref, cbc_ref, dtb_ref, alog_ref,
         dskip_ref, y_ref, hT_ref, xc_s, xdt_s, bc_s, cc_s, la_s, h_s) = refs
    L = SSD_L
    n_chunk = n_tok // L
    g = pl.program_id(1)

    xc_s[...] = _silu(_conv4(x_ref[...], cwx_ref, cbx_ref))
    bc_s[...] = _silu(_conv4(b_ref[...], cwb_ref, cbb_ref)).astype(MXU_DTYPE)
    cc_s[...] = _silu(_conv4(c_ref[...], cwc_ref, cbc_ref)).astype(MXU_DTYPE)
    dt = _softplus(s_ref[...] + dtb_ref[...])
    y_ref[...] = xc_s[...] * dskip_ref[...]

    e8 = (_row_iota((S_W, SSD_GW)) == _lane_iota((S_W, SSD_GW)) // SSD_HEADDIM).astype(MXU_DTYPE)
    ri, ci = _row_iota((L, L)), _lane_iota((L, L))
    lane_w = _lane_iota((L, 2 * SSD_HEADDIM))

    for d in range(2):
        up = d == 1
        sel = (_row_iota((S_W, S_W)) == _lane_iota((S_W, S_W)) + (d * SSD_HEADS + g * SSD_GH)).astype(MXU_DTYPE)
        sel = jnp.where(_lane_iota((S_W, S_W)) < SSD_GH, sel, jnp.zeros_like(sel))
        dt_sel = _dot01_right(dt, sel)
        a_row = -jnp.exp(_dot01_right(alog_ref[...], sel))
        la_s[...] = dt_sel * a_row
        xdt_s[...] = xc_s[...] * _dot01_right(dt_sel, e8)
        if with_state:
            h_s[...] = h0_ref[d].reshape(SSD_GW, SSD_STATE)
        else:
            h_s[...] = jnp.zeros_like(h_s)
        tri = ((ci >= ri) if up else (ci <= ri))
        tri_f = tri.astype(MXU_DTYPE)
        end = 0 if up else L - 1

        def chunk(i, carry):
            c = (n_chunk - 1 - i) if up else i
            t0 = pl.multiple_of(c * L, L)
            rows = pl.ds(t0, L)
            cum = _dot01_left(tri_f, la_s[rows, :])
            cum_t = cum.T
            cum_i = _dot01_right(cum, e8)
            cum_end = cum_i[end:end + 1, :]
            xdt = xdt_s[rows, :]
            bcv, ccv = bc_s[rows, :], cc_s[rows, :]
            cb = _dot_nt(ccv, bcv)
            h_b16 = h_s[...].astype(MXU_DTYPE)
            y = _dot_nt(ccv, h_b16) * jnp.exp(cum_i)
            xdt_b16 = xdt.astype(MXU_DTYPE)
            pieces = []
            for pair in range(SSD_GH // 2):
                xp = xdt_b16[:, pair * 2 * SSD_HEADDIM:(pair + 1) * 2 * SSD_HEADDIM]
                acc = None
                for sub in range(2):
                    hh = 2 * pair + sub
                    seg = jnp.where(tri, jnp.exp(cum[:, hh:hh + 1] - cum_t[hh:hh + 1, :]), 0.0)
                    sc = (cb * seg).astype(MXU_DTYPE)
                    half = (lane_w // SSD_HEADDIM) == sub
                    part = _dot(sc, jnp.where(half, xp, jnp.zeros_like(xp)))
                    acc = part if acc is None else acc + part
                pieces.append(acc)
            y = y + jnp.concatenate(pieces, axis=1)
            y_ref[rows, :] += y
            upd = _dot_tn((xdt * jnp.exp(cum_end - cum_i)).astype(MXU_DTYPE), bcv)
            for hh in range(SSD_GH):
                blk = slice(hh * SSD_HEADDIM, (hh + 1) * SSD_HEADDIM)
                h_s[blk, :] = h_s[blk, :] * jnp.exp(cum_t[hh:hh + 1, end:end + 1]) + upd[blk, :]
            return carry

        lax.fori_loop(0, n_chunk, chunk, 0)
        if not with_state:
            hT_ref[d] = h_s[...].reshape(SSD_GH, SSD_HEADDIM, SSD_STATE)


def _ssd_call(p, s, row_off, bsz, n_tok, lp, h0):
    with_state = h0 is not None
    sb = row_off // n_tok
    assert row_off % n_tok == 0 and n_tok % SSD_L == 0

    def seq_spec(col0, w):
        return pl.BlockSpec((n_tok, w), lambda b, g: (sb + b, col0 // w + g))

    def conv_specs(col0, w):
        return [pl.BlockSpec((CONV_W, w), lambda b, g: (0, col0 // w + g)),
                pl.BlockSpec((1, w), lambda b, g: (0, col0 // w + g))]

    small = pl.BlockSpec((1, S_W), lambda b, g: (0, 0))
    pad = jnp.zeros((S_W - 2 * SSD_HEADS,), jnp.float32)
    dtb = jnp.concatenate([lp['ssd_dt_bias'].reshape(-1), pad]).reshape(1, S_W)
    alog = jnp.concatenate([lp['ssd_a_log'].reshape(-1), pad]).reshape(1, S_W)
    dskip = jnp.repeat(lp['ssd_d'][0] + lp['ssd_d'][1], SSD_HEADDIM).reshape(1, GROUP_W)
    cw, cb = lp['ssd_conv_w'], lp['ssd_conv_b'].reshape(1, -1)
    in_specs = ([seq_spec(P_SX, SSD_GW), seq_spec(P_SB, SSD_STATE), seq_spec(P_SC, SSD_STATE),
                 pl.BlockSpec((n_tok, S_W), lambda b, g: (sb + b, 0))]
                + conv_specs(0, SSD_GW) + conv_specs(GROUP_W, SSD_STATE)
                + conv_specs(GROUP_W + SSD_GROUPS * SSD_STATE, SSD_STATE)
                + [small, small, pl.BlockSpec((1, SSD_GW), lambda b, g: (0, g))])
    args = [p, p, p, s, cw, cb, cw, cb, cw, cb, dtb, alog, dskip]
    state_block = pl.BlockSpec((None, 2, SSD_GH, SSD_HEADDIM, SSD_STATE), lambda b, g: (b, 0, g, 0, 0))
    out_specs = [pl.BlockSpec((n_tok, SSD_GW), lambda b, g: (b, g))]
    out_shape = [jax.ShapeDtypeStruct((bsz * n_tok, GROUP_W), jnp.float32)]
    if with_state:
        in_specs.append(state_block)
        args.append(h0)
    else:
        out_specs.append(state_block)
        out_shape.append(jax.ShapeDtypeStruct((bsz, 2, SSD_HEADS, SSD_HEADDIM, SSD_STATE), jnp.float32))
    return pl.pallas_call(
        functools.partial(_ssd_kernel, n_tok=n_tok, with_state=with_state),
        grid=(bsz, SSD_GROUPS),
        in_specs=in_specs,
        out_specs=out_specs,
        out_shape=out_shape,
        scratch_shapes=[pltpu.VMEM((n_tok, SSD_GW), jnp.float32), pltpu.VMEM((n_tok, SSD_GW), jnp.float32),
                        pltpu.VMEM((n_tok, SSD_STATE), MXU_DTYPE), pltpu.VMEM((n_tok, SSD_STATE), MXU_DTYPE),
                        pltpu.VMEM((n_tok, S_W), jnp.float32),
                        pltpu.VMEM((SSD_GW, SSD_STATE), jnp.float32)],
        compiler_params=_params(2, VMEM_LIMIT_V7X),
        name="ssd_scan",
    )(*args)


def _gated_norm_kernel(y_ref, z_ref, w_ref, o_ref):
    o_ref[...] = _rms_rows(y_ref[...] * _silu(z_ref[...]), w_ref[...]).astype(o_ref.dtype)


def _ssd_finish(y, p, row_off, norm_w, out_prev):
    tm = 256
    m, n = p.shape[0], y.shape[0]
    rb = row_off // tm
    in_specs = [pl.BlockSpec((tm, GROUP_W), lambda i: (i, 0)),
                pl.BlockSpec((tm, GROUP_W), lambda i: (rb + i, P_SZ // GROUP_W)),
                pl.BlockSpec((1, GROUP_W), lambda i: (0, 0))]
    args = [y, p, norm_w.reshape(1, GROUP_W)]
    aliases = {}
    if out_prev is not None:
        in_specs.append(pl.BlockSpec(memory_space=pl.ANY))
        args.append(out_prev)
        aliases = {3: 0}

    def body(y_ref, z_ref, w_ref, *rest):
        _gated_norm_kernel(y_ref, z_ref, w_ref, rest[-1])

    return pl.pallas_call(
        body,
        grid=(n // tm,),
        in_specs=in_specs,
        out_specs=pl.BlockSpec((tm, GROUP_W), lambda i: (rb + i, 0)),
        out_shape=jax.ShapeDtypeStruct((m, GROUP_W), MXU_DTYPE),
        input_output_aliases=aliases,
        compiler_params=_params(1, 32 << 20),
        name="ssd_gated_norm",
    )(*args)


GLA_L = 16


def _gla_kernel(*refs, n_tok, with_state):
    if with_state:
        (q_ref, k_ref, v_ref, gg_ref, s_ref, gw_ref, gb_ref, nw_ref, s0_ref, o_ref, qs_s, b_s, o_s, st_s) = refs
    else:
        (q_ref, k_ref, v_ref, gg_ref, s_ref, gw_ref, gb_ref, nw_ref, o_ref, sT_ref, qs_s, b_s, o_s, st_s) = refs
    L = GLA_L
    n_blk = n_tok // L
    qs_s[...] = q_ref[...] * (GLA_DK ** -0.5)
    sb16 = s_ref[...].astype(MXU_DTYPE)
    t = _row_iota((n_tok, GLA_DK))
    ri = _row_iota((L, GLA_DK))
    lane = _lane_iota((L, GLA_DK))

    for d in range(2):
        up = d == 1
        gate = _dot(sb16, gw_ref[d]) + gb_ref[d]
        b = -_softplus(-gate) / GLA_NORMALIZER
        s = 1
        while s < L:
            ok = (t % L < L - s) if up else (t % L >= s)
            b = b + jnp.where(ok, pltpu.roll(b, (n_tok - s) if up else s, 0), 0.0)
            s *= 2
        b_s[...] = b
        if with_state:
            st_s[...] = s0_ref[d].T
        else:
            st_s[...] = jnp.zeros_like(st_s)
        end = 0 if up else L - 1

        def block(i, carry):
            c = (n_blk - 1 - i) if up else i
            rows = pl.ds(pl.multiple_of(c * L, L), L)
            bb, qb, kb = b_s[rows, :], qs_s[rows, :], k_ref[rows, :]
            vb = v_ref[rows, :].astype(MXU_DTYPE)
            st = st_s[...]
            o = _dot_nt((qb * jnp.exp(bb)).astype(MXU_DTYPE), st.astype(MXU_DTYPE))
            att = jnp.zeros((L, GLA_DK), jnp.float32)
            for j in range(L):
                reach = (ri <= j) if up else (ri >= j)
                w = jnp.where(reach, jnp.exp(bb - bb[j:j + 1, :]), 0.0) * qb * kb[j:j + 1, :]
                att = jnp.where(lane == j, jnp.sum(w, axis=-1, keepdims=True), att)
            o = o + _dot(att[:, :L].astype(MXU_DTYPE), vb)
            if up:
                o_s[rows, :] += o
            else:
                o_s[rows, :] = o
            b_end = bb[end:end + 1, :]
            ke = (kb * jnp.exp(b_end - bb)).astype(MXU_DTYPE)
            st_s[...] = st * jnp.exp(b_end) + _dot_tn(vb, ke)
            return carry

        lax.fori_loop(0, n_blk, block, 0)
        if not with_state:
            sT_ref[d] = st_s[...].T
    o = _rms_rows(o_s[...], nw_ref[...]) * _silu(gg_ref[...])
    o_ref[...] = o.astype(o_ref.dtype)


def _gla_call(p, s, row_off, bsz, n_tok, lp, s0, out_prev):
    m = p.shape[0]
    with_state = s0 is not None
    sb = row_off // n_tok
    assert row_off % n_tok == 0 and n_tok % GLA_L == 0

    def seq_spec(col0, w):
        return pl.BlockSpec((n_tok, w), lambda b, h: (sb + b, col0 // w + h))

    gw = lp['gla_gate_w'].reshape(2, GLA_RANK, GLA_HEADS, GLA_DK).transpose(0, 2, 1, 3)
    gw_rows = jnp.zeros((2, GLA_HEADS, S_W, GLA_DK), jnp.float32)
    for d in range(2):
        r0 = 2 * SSD_HEADS + d * GLA_RANK
        gw_rows = gw_rows.at[d, :, r0:r0 + GLA_RANK, :].set(gw[d])
    in_specs = [seq_spec(P_GQ, GLA_DK), seq_spec(P_GK, GLA_DK), seq_spec(P_GV, GLA_DV), seq_spec(P_GG, GLA_DV),
                pl.BlockSpec((n_tok, S_W), lambda b, h: (sb + b, 0)),
                pl.BlockSpec((2, None, S_W, GLA_DK), lambda b, h: (0, h, 0, 0)),
                pl.BlockSpec((2, None, 1, GLA_DK), lambda b, h: (0, h, 0, 0)),
                pl.BlockSpec((1, GLA_DV), lambda b, h: (0, 0))]
    args = [p, p, p, p, s, gw_rows.astype(MXU_DTYPE), lp['gla_gate_b'].reshape(2, GLA_HEADS, 1, GLA_DK),
            lp['gla_norm_w'].reshape(1, GLA_DV)]
    state_block = pl.BlockSpec((None, 2, None, GLA_DK, GLA_DV), lambda b, h: (b, 0, h, 0, 0))
    out_specs = [pl.BlockSpec((n_tok, GLA_DV), lambda b, h: (sb + b, h))]
    out_shape = [jax.ShapeDtypeStruct((m, GROUP_W), MXU_DTYPE)]
    if with_state:
        in_specs.append(state_block)
        args.append(s0)
    else:
        out_specs.append(state_block)
        out_shape.append(jax.ShapeDtypeStruct((bsz, 2, GLA_HEADS, GLA_DK, GLA_DV), jnp.float32))
    n_in = len(args)
    aliases = {}
    if out_prev is not None:
        in_specs.append(pl.BlockSpec(memory_space=pl.ANY))
        args.append(out_prev)
        aliases = {n_in: 0}

    def body(*refs):
        if out_prev is not None:
            refs = refs[:n_in] + refs[n_in + 1:]
        _gla_kernel(*refs, n_tok=n_tok, with_state=with_state)

    return pl.pallas_call(
        body,
        grid=(bsz, GLA_HEADS),
        in_specs=in_specs,
        out_specs=out_specs,
        out_shape=out_shape,
        scratch_shapes=[pltpu.VMEM((n_tok, GLA_DK), jnp.float32), pltpu.VMEM((n_tok, GLA_DK), jnp.float32),
                        pltpu.VMEM((n_tok, GLA_DV), jnp.float32), pltpu.VMEM((GLA_DV, GLA_DK), jnp.float32)],
        input_output_aliases=aliases,
        compiler_params=_params(2, 48 << 20),
        name="gla_scan",
    )(*args)


def _mixers(p, s, lp, cached):
    ctx_k, ctx_v, ssd0, gla0, lru0 = cached
    kv_w = ATT_KV_HEADS * HEAD_DIM
    att, k_new = _attention_call(p, 0, BATCH, SEQ, lp['q_norm'], lp['k_norm'], None, None)
    att, = _attention_call(p, M_PROMPT, DEC_BATCH, DEC_SEQ, lp['q_norm'], lp['k_norm'],
                           (ctx_k.reshape(DEC_BATCH, PAST_LEN, kv_w), ctx_v.reshape(DEC_BATCH, PAST_LEN, kv_w)), att)
    v_new = p[:M_PROMPT, P_AV:P_AV + kv_w]

    y_p, ssd_new = _ssd_call(p, s, 0, BATCH, SEQ, lp, None)
    y_s, = _ssd_call(p, s, M_PROMPT, DEC_BATCH, DEC_SEQ, lp, ssd0)
    ssd = _ssd_finish(y_p, p, 0, lp['ssd_norm_w'], None)
    ssd = _ssd_finish(y_s, p, M_PROMPT, lp['ssd_norm_w'], ssd)

    gla, gla_new = _gla_call(p, s, 0, BATCH, SEQ, lp, None, None)
    gla, = _gla_call(p, s, M_PROMPT, DEC_BATCH, DEC_SEQ, lp, gla0, gla)

    lru, lru_new = _lru_call(p, 0, BATCH, SEQ, lp, None, None)
    lru, = _lru_call(p, M_PROMPT, DEC_BATCH, DEC_SEQ, lp, lru0, lru)

    new_ctx = (k_new.reshape(BATCH, SEQ, ATT_KV_HEADS, HEAD_DIM), v_new.reshape(BATCH, SEQ, ATT_KV_HEADS, HEAD_DIM),
               ssd_new, gla_new, lru_new.reshape(BATCH, 2, LRU_W))
    return [att, ssd, gla, lru], new_ctx


def _reorder_w_in(w):
    w = w.astype(MXU_DTYPE)
    aq, ak, av, sx, sz = w[:, 0:1024], w[:, 1024:1280], w[:, 1280:1536], w[:, 1536:2560], w[:, 2560:3584]
    sb, sc, gq, gk, gv = w[:, 3584:3840], w[:, 3840:4096], w[:, 4128:4640], w[:, 4640:5152], w[:, 5152:6176]
    gg, lx, lg = w[:, 6208:7232], w[:, 7232:8256], w[:, 8256:9280]
    main = jnp.concatenate([aq, sx, sz, gv, gg, lx, lg, gq, gk, ak, av, sb, sc], axis=1)
    small = jnp.concatenate([w[:, 4096:4128], w[:, 6176:6208],
                             jnp.zeros((D_MODEL, S_W - 64), MXU_DTYPE)], axis=1)
    return main, small


def kernel(x_prompt, x_sample, c, cache_attn_k, cache_attn_v, state_ssd, state_gla, state_lru, c_ctx,
           mod_w, mod_b, ln_g, ln_b, ffn_w_gate, ffn_w_up, ffn_w_down, w_in, w_out, q_norm, k_norm,
           ssd_conv_w, ssd_conv_b, ssd_a_log, ssd_dt_bias, ssd_d, ssd_norm_w,
           gla_gate_w, gla_gate_b, gla_norm_w,
           lru_conv_w, lru_conv_b, lru_wa, lru_ba, lru_wx, lru_bx, lru_lam):
    cond = jnp.concatenate([c_ctx[None], c, jnp.zeros((COND_ROWS - N_COND, D_MODEL), jnp.float32)], axis=0)
    mod_all = _mod_table(cond, mod_w, mod_b)

    x = jnp.concatenate([x_prompt.reshape(M_PROMPT, D_MODEL), x_sample.reshape(M_SAMPLE, D_MODEL)], axis=0)
    xm = _modulate(x, mod_all[0], 1, 0)
    ctx_out = []
    for l in range(DEPTH):
        mod = mod_all[l]
        lp = dict(q_norm=q_norm[l], k_norm=k_norm[l], ssd_conv_w=ssd_conv_w[l], ssd_conv_b=ssd_conv_b[l],
                  ssd_a_log=ssd_a_log[l], ssd_dt_bias=ssd_dt_bias[l], ssd_d=ssd_d[l], ssd_norm_w=ssd_norm_w[l],
                  gla_gate_w=gla_gate_w[l], gla_gate_b=gla_gate_b[l], gla_norm_w=gla_norm_w[l],
                  lru_conv_w=lru_conv_w[l], lru_conv_b=lru_conv_b[l], lru_wa=lru_wa[l], lru_ba=lru_ba[l],
                  lru_wx=lru_wx[l], lru_bx=lru_bx[l], lru_lam=lru_lam[l])
        h = _ffn_up(xm, ffn_w_gate[l, 0].astype(MXU_DTYPE), ffn_w_up[l, 0].astype(MXU_DTYPE))
        f = _matmul([h], ffn_w_down[l, 0].astype(MXU_DTYPE), jnp.float32, 512, 512, "ffn_down")
        x, xm = _resid_ln(x, f, mod, 2, 0.5, ln_g[l, 0], ln_b[l, 0], mod, 4, 3)
        w_main, w_small = _reorder_w_in(w_in[l])
        p = _matmul([xm], w_main, jnp.float32, 1024, 1024, "proj_in")
        s = _matmul([xm], w_small, jnp.float32, 1024, S_W, "proj_in_small")
        cached = (cache_attn_k[:, l], cache_attn_v[:, l], state_ssd[:, l], state_gla[:, l], state_lru[:, l])
        mix, new_ctx = _mixers(p, s, lp, cached)
        ctx_out.append(new_ctx)
        m_out = _matmul(mix, w_out[l].astype(MXU_DTYPE), jnp.float32, 1024, 1024, "proj_out")
        x, xm = _resid_ln(x, m_out, mod, 5, 1.0, ln_g[l, 1], ln_b[l, 1], mod, 7, 6)
        h = _ffn_up(xm, ffn_w_gate[l, 1].astype(MXU_DTYPE), ffn_w_up[l, 1].astype(MXU_DTYPE))
        f = _matmul([h], ffn_w_down[l, 1].astype(MXU_DTYPE), jnp.float32, 512, 512, "ffn_down")
        if l + 1 < DEPTH:
            x, xm = _resid_ln(x, f, mod, 8, 0.5, ln_g[l, 2], ln_b[l, 2], mod_all[l + 1], 1, 0)
        else:
            x, _ = _resid_ln(x, f, mod, 8, 0.5, ln_g[l, 2], ln_b[l, 2])

    y_prompt = x[:M_PROMPT].reshape(BATCH, SEQ, D_MODEL)
    y_sample = x[M_PROMPT:].reshape(DEC_BATCH, DEC_SEQ, D_MODEL)
    new_k, new_v, new_ssd, new_gla, new_lru = (jnp.stack([s_[i] for s_ in ctx_out], axis=1) for i in range(5))
    return (y_prompt, y_sample, new_k, new_v, new_ssd, new_gla, new_lru)
```

```python
import functools
import math

import numpy as np
import jax
import jax.numpy as jnp
from jax import lax
from jax.experimental import pallas as pl
from jax.experimental.pallas import tpu as pltpu

D_MODEL = 4096
BATCH = 16
SEQ = 256
DEPTH = 2
DEC_BATCH = 4
DEC_SEQ = 2048
PAST_LEN = 256
GRID_W = 64
GROUP_W = 1024
HEAD_DIM = 128
ATT_HEADS = 8
ATT_KV_HEADS = 2
ROPE_THETA = 10000.0
SSD_HEADDIM = 64
SSD_HEADS = 16
SSD_STATE = 128
SSD_GROUPS = 2
CONV_W = 4
GLA_HEADS = 4
GLA_DK = 128
GLA_DV = 256
GLA_RANK = 16
GLA_NORMALIZER = 16.0
LRU_W = 1024
LRU_BLOCKS = 8
LRU_BW = 128
LRU_C = 8.0
D_FF = 11008
N_MOD = 9
LN_EPS = 1e-5
RMS_EPS = 1e-6
ALPHA = (2.0 * DEPTH) ** 0.25

M_PROMPT = BATCH * SEQ
M_SAMPLE = DEC_BATCH * DEC_SEQ
M_TOK = M_PROMPT + M_SAMPLE
N_COND = 1 + DEC_BATCH
COND_ROWS = 8

VMEM_LIMIT_V7X = 56 * 1024 * 1024

MXU_DTYPE = jnp.bfloat16

P_AQ, P_SX, P_SZ, P_GV, P_GG, P_LX, P_LG = 0, 1024, 2048, 3072, 4096, 5120, 6144
P_GQ, P_GK = 7168, 7680
P_AK, P_AV, P_SB, P_SC = 8192, 8448, 8704, 8960
P_W = 9216
S_W = 128


def _params(n_axes, vmem_bytes):
    return pltpu.CompilerParams(dimension_semantics=("arbitrary",) * n_axes,
                                vmem_limit_bytes=min(int(vmem_bytes), VMEM_LIMIT_V7X))


def _cond_row(tile_idx, tile_rows):
    row0 = tile_idx * tile_rows
    return jnp.where(row0 < M_PROMPT, 0, 1 + (row0 - M_PROMPT) // DEC_SEQ)


def _mod_kernel(c_ref, w_ref, b_ref, o_ref):
    c = c_ref[...]
    a = (c * jax.nn.sigmoid(c)).astype(MXU_DTYPE)
    w = w_ref[...].astype(MXU_DTYPE)
    o_ref[...] = jnp.dot(a, w, preferred_element_type=jnp.float32) + b_ref[...]


def _mod_table(cond, mod_w, mod_b):
    tn = 512
    n = N_MOD * D_MODEL
    return pl.pallas_call(
        _mod_kernel,
        grid=(DEPTH, n // tn),
        in_specs=[pl.BlockSpec((COND_ROWS, D_MODEL), lambda l, j: (0, 0)),
                  pl.BlockSpec((None, D_MODEL, tn), lambda l, j: (l, 0, j)),
                  pl.BlockSpec((None, 1, tn), lambda l, j: (l, 0, j))],
        out_specs=pl.BlockSpec((None, COND_ROWS, tn), lambda l, j: (l, 0, j)),
        out_shape=jax.ShapeDtypeStruct((DEPTH, COND_ROWS, n), jnp.float32),
        compiler_params=_params(2, 3 * D_MODEL * tn * 4 + (4 << 20)),
        name="mod_table",
    )(cond, mod_w, mod_b.reshape(DEPTH, 1, n))


def _mm_kernel(*refs, n_a):
    a_refs, w_ref, o_ref = refs[:n_a], refs[n_a], refs[n_a + 1]
    k0 = 0
    acc = None
    for a_ref in a_refs:
        kp = a_ref.shape[1]
        part = jnp.dot(a_ref[...], w_ref[k0:k0 + kp, :], preferred_element_type=jnp.float32)
        acc = part if acc is None else acc + part
        k0 += kp
    o_ref[...] = acc.astype(o_ref.dtype)


def _matmul(a_list, w, out_dtype, tm, tn, name):
    m = a_list[0].shape[0]
    k, n = w.shape
    assert sum(a.shape[1] for a in a_list) == k and m % tm == 0 and n % tn == 0
    in_specs = [pl.BlockSpec((tm, a.shape[1]), lambda i, j: (i, 0)) for a in a_list]
    in_specs.append(pl.BlockSpec((k, tn), lambda i, j: (0, j)))
    esz = jnp.dtype(MXU_DTYPE).itemsize
    vmem = 2 * (tm * k * esz + k * tn * esz + tm * tn * jnp.dtype(out_dtype).itemsize) + tm * tn * 8
    return pl.pallas_call(
        functools.partial(_mm_kernel, n_a=len(a_list)),
        grid=(m // tm, n // tn),
        in_specs=in_specs,
        out_specs=pl.BlockSpec((tm, tn), lambda i, j: (i, j)),
        out_shape=jax.ShapeDtypeStruct((m, n), out_dtype),
        compiler_params=_params(2, vmem + (4 << 20)),
        name=name,
    )(*a_list, w)


def _ffn_up_kernel(a_ref, wg_ref, wu_ref, o_ref):
    a = a_ref[...]
    g = jnp.dot(a, wg_ref[...], preferred_element_type=jnp.float32)
    u = jnp.dot(a, wu_ref[...], preferred_element_type=jnp.float32)
    o_ref[...] = (g * jax.nn.sigmoid(g) * u).astype(o_ref.dtype)


def _ffn_up(a, wg, wu):
    tm, tn = 2048, 256
    m = a.shape[0]
    esz = jnp.dtype(MXU_DTYPE).itemsize
    vmem = 2 * (tm * D_MODEL * esz + 2 * D_MODEL * tn * esz + tm * tn * esz) + 4 * tm * tn * 4
    return pl.pallas_call(
        _ffn_up_kernel,
        grid=(m // tm, D_FF // tn),
        in_specs=[pl.BlockSpec((tm, D_MODEL), lambda i, j: (i, 0)),
                  pl.BlockSpec((D_MODEL, tn), lambda i, j: (0, j)),
                  pl.BlockSpec((D_MODEL, tn), lambda i, j: (0, j))],
        out_specs=pl.BlockSpec((tm, tn), lambda i, j: (i, j)),
        out_shape=jax.ShapeDtypeStruct((m, D_FF), MXU_DTYPE),
        compiler_params=_params(2, vmem + (4 << 20)),
        name="ffn_up",
    )(a, wg, wu)


def _modulate_kernel(x_ref, sc_ref, sh_ref, o_ref, *, tm):
    r = _cond_row(pl.program_id(0), tm)
    sc = sc_ref[pl.ds(r, 1), :]
    sh = sh_ref[pl.ds(r, 1), :]
    o_ref[...] = (x_ref[...] * (1.0 + sc) + sh).astype(o_ref.dtype)


def _mod_spec(k):
    return pl.BlockSpec((COND_ROWS, D_MODEL), lambda i: (0, k))


def _modulate(x, mod, k_scale, k_shift):
    tm = 256
    m = x.shape[0]
    return pl.pallas_call(
        functools.partial(_modulate_kernel, tm=tm),
        grid=(m // tm,),
        in_specs=[pl.BlockSpec((tm, D_MODEL), lambda i: (i, 0)), _mod_spec(k_scale), _mod_spec(k_shift)],
        out_specs=pl.BlockSpec((tm, D_MODEL), lambda i: (i, 0)),
        out_shape=jax.ShapeDtypeStruct((m, D_MODEL), MXU_DTYPE),
        compiler_params=_params(1, 32 << 20),
        name="modulate",
    )(x, mod, mod)


def _resid_ln_kernel(*refs, tm, gate_scale, with_next):
    if with_next:
        x_ref, f_ref, g_ref, lg_ref, lb_ref, sc_ref, sh_ref, xo_ref, mo_ref = refs
    else:
        x_ref, f_ref, g_ref, lg_ref, lb_ref, xo_ref = refs
    r = _cond_row(pl.program_id(0), tm)
    g = g_ref[pl.ds(r, 1), :]
    y = ALPHA * x_ref[...] + (gate_scale * g) * f_ref[...]
    mu = jnp.mean(y, axis=-1, keepdims=True)
    yc = y - mu
    var = jnp.mean(yc * yc, axis=-1, keepdims=True)
    xn = yc * lax.rsqrt(var + LN_EPS) * lg_ref[...] + lb_ref[...]
    xo_ref[...] = xn
    if with_next:
        sc = sc_ref[pl.ds(r, 1), :]
        sh = sh_ref[pl.ds(r, 1), :]
        mo_ref[...] = (xn * (1.0 + sc) + sh).astype(mo_ref.dtype)


def _resid_ln(x, f, mod, k_gate, gate_scale, ln_g, ln_b, next_mod=None, k_scale=0, k_shift=0):
    tm = 256
    m = x.shape[0]
    with_next = next_mod is not None
    row = pl.BlockSpec((tm, D_MODEL), lambda i: (i, 0))
    vec = pl.BlockSpec((1, D_MODEL), lambda i: (0, 0))
    in_specs = [row, row, _mod_spec(k_gate), vec, vec]
    args = [x, f, mod, ln_g.reshape(1, D_MODEL), ln_b.reshape(1, D_MODEL)]
    out_specs = [row]
    out_shape = [jax.ShapeDtypeStruct((m, D_MODEL), jnp.float32)]
    if with_next:
        in_specs += [_mod_spec(k_scale), _mod_spec(k_shift)]
        args += [next_mod, next_mod]
        out_specs.append(row)
        out_shape.append(jax.ShapeDtypeStruct((m, D_MODEL), MXU_DTYPE))
    out = pl.pallas_call(
        functools.partial(_resid_ln_kernel, tm=tm, gate_scale=gate_scale, with_next=with_next),
        grid=(m // tm,),
        in_specs=in_specs,
        out_specs=out_specs,
        out_shape=out_shape,
        compiler_params=_params(1, 48 << 20),
        name="resid_ln",
    )(*args)
    return (out[0], out[1]) if with_next else (out[0], None)


def _row_iota(shape):
    return lax.broadcasted_iota(jnp.int32, shape, 0)


def _lane_iota(shape):
    return lax.broadcasted_iota(jnp.int32, shape, 1)


def _shift_rows(x, s, fill, up=False):
    n = x.shape[0]
    t = _row_iota(x.shape)
    if up:
        return jnp.where(t < n - s, pltpu.roll(x, n - s, 0), fill)
    return jnp.where(t >= s, pltpu.roll(x, s, 0), fill)


def _softplus(z):
    return jnp.maximum(z, 0.0) + jnp.log1p(jnp.exp(-jnp.abs(z)))


def _silu(z):
    return z * jax.nn.sigmoid(z)


def _dot(a, b):
    return jnp.dot(a, b, preferred_element_type=jnp.float32)


def _dot_nt(a, b):
    return lax.dot_general(a, b, (((1,), (1,)), ((), ())), preferred_element_type=jnp.float32)


def _dot_tn(a, b):
    return lax.dot_general(a, b, (((0,), (0,)), ((), ())), preferred_element_type=jnp.float32)


def _split3(x):
    hi = x.astype(MXU_DTYPE)
    r1 = x - hi.astype(jnp.float32)
    mid = r1.astype(MXU_DTYPE)
    lo = (r1 - mid.astype(jnp.float32)).astype(MXU_DTYPE)
    return hi, mid, lo


def _dot01_right(x, e):
    hi, mid, lo = _split3(x)
    return _dot(hi, e) + _dot(mid, e) + _dot(lo, e)


def _dot01_left(e, x):
    hi, mid, lo = _split3(x)
    return _dot(e, hi) + _dot(e, mid) + _dot(e, lo)


def _conv4(x, w_ref, b_ref):
    acc = _shift_rows(x, 2, 0.0) * w_ref[0:1, :]
    acc = acc + _shift_rows(x, 1, 0.0) * w_ref[1:2, :]
    acc = acc + x * w_ref[2:3, :]
    acc = acc + _shift_rows(x, 1, 0.0, up=True) * w_ref[3:4, :]
    return acc + b_ref[...]


def _rms_rows(x, w):
    return x * lax.rsqrt(jnp.mean(x * x, axis=-1, keepdims=True) + RMS_EPS) * w


def _rope_tables(n_tok):
    rows = n_tok // GRID_W
    row = jnp.repeat(jnp.arange(rows, dtype=jnp.float32), GRID_W)
    col = jnp.tile(jnp.arange(GRID_W, dtype=jnp.float32), rows)
    n_freq = HEAD_DIM // 4
    inv = ROPE_THETA ** (-jnp.arange(n_freq, dtype=jnp.float32) / n_freq)
    ar, ac = row[:, None] * inv, col[:, None] * inv
    cos = jnp.concatenate([jnp.cos(ar), jnp.cos(ar), jnp.cos(ac), jnp.cos(ac)], axis=1)
    sin = jnp.concatenate([-jnp.sin(ar), jnp.sin(ar), -jnp.sin(ac), jnp.sin(ac)], axis=1)
    return cos, sin


def _rope(x, cos, sin):
    quarter = HEAD_DIM // 4
    lane = _lane_iota(x.shape)
    partner = jnp.where(lane % (2 * quarter) < quarter,
                        pltpu.roll(x, HEAD_DIM - quarter, 1), pltpu.roll(x, quarter, 1))
    return x * cos + partner * sin


def _attn_kernel(*refs, n_tok, tq, n_ctx, rope):
    if rope:
        (q_ref, k_ref, v_ref, qn_ref, kn_ref, cq_ref, sq_ref, ck_ref, sk_ref, xk_ref, xv_ref,
         o_ref, ks_ref, vs_ref) = refs
    else:
        q_ref, k_ref, v_ref, qn_ref, kn_ref, o_ref, ko_ref, ks_ref, vs_ref = refs
    rep = ATT_HEADS // ATT_KV_HEADS

    @pl.when(pl.program_id(2) == 0)
    def _():
        kn = _rms_rows(k_ref[...], kn_ref[...])
        if rope:
            kn = _rope(kn, ck_ref[...], sk_ref[...])
            ks_ref[n_tok:n_tok + n_ctx, :] = xk_ref[...].astype(MXU_DTYPE)
            vs_ref[n_tok:n_tok + n_ctx, :] = xv_ref[...].astype(MXU_DTYPE)
        else:
            ko_ref[...] = kn
        ks_ref[0:n_tok, :] = kn.astype(MXU_DTYPE)
        vs_ref[0:n_tok, :] = v_ref[...].astype(MXU_DTYPE)

    heads = []
    for r in range(rep):
        qh = _rms_rows(q_ref[:, r * HEAD_DIM:(r + 1) * HEAD_DIM], qn_ref[...])
        if rope:
            qh = _rope(qh, cq_ref[...], sq_ref[...])
        heads.append(qh.astype(MXU_DTYPE))
    qs = jnp.concatenate(heads, axis=0)
    s = _dot_nt(qs, ks_ref[...]) * (HEAD_DIM ** -0.5)
    e = jnp.exp(s - jnp.max(s, axis=-1, keepdims=True))
    o = _dot(e.astype(MXU_DTYPE), vs_ref[...]) / jnp.sum(e, axis=-1, keepdims=True)
    for r in range(rep):
        o_ref[:, r * HEAD_DIM:(r + 1) * HEAD_DIM] = o[r * tq:(r + 1) * tq].astype(o_ref.dtype)


def _attention_call(p, row_off, bsz, n_tok, q_norm, k_norm, ctx_kv, out_prev):
    m = p.shape[0]
    rope = ctx_kv is not None
    tq = 128 if rope else n_tok
    nq = n_tok // tq
    rep = ATT_HEADS // ATT_KV_HEADS
    qw = rep * HEAD_DIM
    n_ctx = ctx_kv[0].shape[1] if rope else 0
    assert row_off % n_tok == 0 and n_tok % tq == 0
    rb, sb = row_off // tq, row_off // n_tok

    def seq_spec(col0):
        return pl.BlockSpec((n_tok, HEAD_DIM), lambda b, g, i: (sb + b, col0 // HEAD_DIM + g))

    vec = pl.BlockSpec((1, HEAD_DIM), lambda b, g, i: (0, 0))
    in_specs = [pl.BlockSpec((tq, qw), lambda b, g, i: (rb + b * nq + i, P_AQ // qw + g)),
                seq_spec(P_AK), seq_spec(P_AV), vec, vec]
    args = [p, p, p, q_norm.reshape(1, HEAD_DIM), k_norm.reshape(1, HEAD_DIM)]
    out_block = pl.BlockSpec((tq, qw), lambda b, g, i: (rb + b * nq + i, g))
    out_specs = [out_block]
    out_shape = [jax.ShapeDtypeStruct((m, GROUP_W), MXU_DTYPE)]
    aliases = {}
    if rope:
        cos, sin = _rope_tables(n_tok)
        in_specs += [pl.BlockSpec((tq, HEAD_DIM), lambda b, g, i: (i, 0))] * 2
        in_specs += [pl.BlockSpec((n_tok, HEAD_DIM), lambda b, g, i: (0, 0))] * 2
        in_specs += [pl.BlockSpec((None, n_ctx, HEAD_DIM), lambda b, g, i: (b, 0, g))] * 2
        args += [cos, sin, cos, sin, ctx_kv[0], ctx_kv[1]]
    else:
        out_specs.append(pl.BlockSpec((n_tok, HEAD_DIM), lambda b, g, i: (b, g)))
        out_shape.append(jax.ShapeDtypeStruct((bsz * n_tok, ATT_KV_HEADS * HEAD_DIM), jnp.float32))
    if out_prev is not None:
        in_specs.append(pl.BlockSpec(memory_space=pl.ANY))
        args.append(out_prev)
        aliases = {len(args) - 1: 0}
    n_keys = n_tok + n_ctx

    def body(*refs):
        if out_prev is not None:
            refs = refs[:len(args) - 1] + refs[len(args):]
        _attn_kernel(*refs, n_tok=n_tok, tq=tq, n_ctx=n_ctx, rope=rope)

    out = pl.pallas_call(
        body,
        grid=(bsz, ATT_KV_HEADS, nq),
        in_specs=in_specs,
        out_specs=out_specs,
        out_shape=out_shape,
        scratch_shapes=[pltpu.VMEM((n_keys, HEAD_DIM), MXU_DTYPE), pltpu.VMEM((n_keys, HEAD_DIM), MXU_DTYPE)],
        input_output_aliases=aliases,
        compiler_params=_params(3, 48 << 20),
        name="attention",
    )(*args)
    return out


def _lru_kernel(*refs, n_tok, with_state):
    if with_state:
        (x_ref, g_ref, cw_ref, cb_ref, wa_ref, ba_ref, wx_ref, bx_ref, lam_ref, h0_ref, o_ref) = refs
    else:
        (x_ref, g_ref, cw_ref, cb_ref, wa_ref, ba_ref, wx_ref, bx_ref, lam_ref, o_ref, hT_ref) = refs
    xl = _conv4(x_ref[...], cw_ref, cb_ref)
    xb = xl.astype(MXU_DTYPE)
    t = _row_iota(xl.shape)
    h_sum = None
    for d in range(2):
        up = d == 1
        r = jax.nn.sigmoid(_dot(xb, wa_ref[d]) + ba_ref[d])
        i = jax.nn.sigmoid(_dot(xb, wx_ref[d]) + bx_ref[d])
        log_a = -LRU_C * r * _softplus(-lam_ref[d])
        a = jnp.exp(log_a)
        th = jnp.tanh(log_a)
        u = jnp.sqrt(-2.0 * th / (1.0 - th)) * i * xl
        if with_state:
            first = (t == n_tok - 1) if up else (t == 0)
            u = jnp.where(first, u + a * h0_ref[d], u)
        s = 1
        while s < n_tok:
            u = u + a * _shift_rows(u, s, 0.0, up=up)
            a = a * _shift_rows(a, s, 1.0, up=up)
            s *= 2
        h_sum = u if h_sum is None else h_sum + u
        if not with_state:
            hT_ref[d] = u[0:1, :] if up else u[n_tok - 1:n_tok, :]
    o_ref[...] = (h_sum * jax.nn.gelu(g_ref[...])).astype(o_ref.dtype)


def _lru_call(p, row_off, bsz, n_tok, lp, h0, out_prev):
    m = p.shape[0]
    with_state = h0 is not None
    sb = row_off // n_tok
    assert row_off % n_tok == 0

    def seq_spec(col0):
        return pl.BlockSpec((n_tok, LRU_BW), lambda b, n: (sb + b, col0 // LRU_BW + n))

    def par2(shape):
        return pl.BlockSpec((2,) + shape + (LRU_BW,), lambda b, n: (0,) + (0,) * len(shape) + (n,))

    in_specs = [seq_spec(P_LX), seq_spec(P_LG),
                pl.BlockSpec((CONV_W, LRU_BW), lambda b, n: (0, n)),
                pl.BlockSpec((1, LRU_BW), lambda b, n: (0, n)),
                pl.BlockSpec((2, None, LRU_BW, LRU_BW), lambda b, n: (0, n, 0, 0)), par2((1,)),
                pl.BlockSpec((2, None, LRU_BW, LRU_BW), lambda b, n: (0, n, 0, 0)), par2((1,)), par2((1,))]
    args = [p, p, lp['lru_conv_w'], lp['lru_conv_b'].reshape(1, LRU_W),
            lp['lru_wa'].astype(MXU_DTYPE), lp['lru_ba'].reshape(2, 1, LRU_W),
            lp['lru_wx'].astype(MXU_DTYPE), lp['lru_bx'].reshape(2, 1, LRU_W), lp['lru_lam'].reshape(2, 1, LRU_W)]
    out_specs = [pl.BlockSpec((n_tok, LRU_BW), lambda b, n: (sb + b, n))]
    out_shape = [jax.ShapeDtypeStruct((m, GROUP_W), MXU_DTYPE)]
    if with_state:
        in_specs.append(pl.BlockSpec((None, 2, 1, LRU_BW), lambda b, n: (b, 0, 0, n)))
        args.append(h0.reshape(bsz, 2, 1, LRU_W))
    else:
        out_specs.append(pl.BlockSpec((None, 2, 1, LRU_BW), lambda b, n: (b, 0, 0, n)))
        out_shape.append(jax.ShapeDtypeStruct((bsz, 2, 1, LRU_W), jnp.float32))
    n_in = len(args)
    aliases = {}
    if out_prev is not None:
        in_specs.append(pl.BlockSpec(memory_space=pl.ANY))
        args.append(out_prev)
        aliases = {n_in: 0}

    def body(*refs):
        if out_prev is not None:
            refs = refs[:n_in] + refs[n_in + 1:]
        _lru_kernel(*refs, n_tok=n_tok, with_state=with_state)

    return pl.pallas_call(
        body,
        grid=(bsz, LRU_BLOCKS),
        in_specs=in_specs,
        out_specs=out_specs,
        out_shape=out_shape,
        input_output_aliases=aliases,
        compiler_params=_params(2, 48 << 20),
        name="rg_lru",
    )(*args)


SSD_L = 128
SSD_GH = SSD_HEADS // SSD_GROUPS
SSD_GW = SSD_GH * SSD_HEADDIM


def _ssd_kernel(*refs, n_tok, with_state):
    if with_state:
        (x_ref, b_ref, c_ref, s_ref, cwx_ref, cbx_ref, cwb_ref, cbb_ref, cwc_ref, cbc_ref, dtb_ref, alog_ref,
         dskip_ref, h0_ref, y_ref, xc_s, xdt_s, bc_s, cc_s, la_s, h_s) = refs
    else:
        (x_ref, b_ref, c_ref, s_ref, cwx_ref, cbx_ref, cwb_ref, cbb_ref, cwc_ref, cbc_ref, dtb_ref, alog_ref,
         dskip_ref, y_ref, hT_ref, xc_s, xdt_s, bc_s, cc_s, la_s, h_s) = refs
    L = SSD_L
    n_chunk = n_tok // L
    g = pl.program_id(1)

    xc_s[...] = _silu(_conv4(x_ref[...], cwx_ref, cbx_ref))
    bc_s[...] = _silu(_conv4(b_ref[...], cwb_ref, cbb_ref)).astype(MXU_DTYPE)
    cc_s[...] = _silu(_conv4(c_ref[...], cwc_ref, cbc_ref)).astype(MXU_DTYPE)
    dt = _softplus(s_ref[...] + dtb_ref[...])
    y_ref[...] = xc_s[...] * dskip_ref[...]

    e8 = (_row_iota((S_W, SSD_GW)) == _lane_iota((S_W, SSD_GW)) // SSD_HEADDIM).astype(MXU_DTYPE)
    ri, ci = _row_iota((L, L)), _lane_iota((L, L))
    lane_w = _lane_iota((L, 2 * SSD_HEADDIM))

    for d in range(2):
        up = d == 1
        sel = (_row_iota((S_W, S_W)) == _lane_iota((S_W, S_W)) + (d * SSD_HEADS + g * SSD_GH)).astype(MXU_DTYPE)
        sel = jnp.where(_lane_iota((S_W, S_W)) < SSD_GH, sel, jnp.zeros_like(sel))
        dt_sel = _dot01_right(dt, sel)
        a_row = -jnp.exp(_dot01_right(alog_ref[...], sel))
        la_s[...] = dt_sel * a_row
        xdt_s[...] = xc_s[...] * _dot01_right(dt_sel, e8)
        if with_state:
            h_s[...] = h0_ref[d].reshape(SSD_GW, SSD_STATE)
        else:
            h_s[...] = jnp.zeros_like(h_s)
        tri = ((ci >= ri) if up else (ci <= ri))
        tri_f = tri.astype(MXU_DTYPE)
        end = 0 if up else L - 1

        def chunk(i, carry):
            c = (n_chunk - 1 - i) if up else i
            t0 = pl.multiple_of(c * L, L)
            rows = pl.ds(t0, L)
            cum = _dot01_left(tri_f, la_s[rows, :])
            cum_t = cum.T
            cum_i = _dot01_right(cum, e8)
            cum_end = cum_i[end:end + 1, :]
            xdt = xdt_s[rows, :]
            bcv, ccv = bc_s[rows, :], cc_s[rows, :]
            cb = _dot_nt(ccv, bcv)
            h_b16 = h_s[...].astype(MXU_DTYPE)
            y = _dot_nt(ccv, h_b16) * jnp.exp(cum_i)
            xdt_b16 = xdt.astype(MXU_DTYPE)
            pieces = []
            for pair in range(SSD_GH // 2):
                xp = xdt_b16[:, pair * 2 * SSD_HEADDIM:(pair + 1) * 2 * SSD_HEADDIM]
                acc = None
                for sub in range(2):
                    hh = 2 * pair + sub
                    seg = jnp.where(tri, jnp.exp(cum[:, hh:hh + 1] - cum_t[hh:hh + 1, :]), 0.0)
                    sc = (cb * seg).astype(MXU_DTYPE)
                    half = (lane_w // SSD_HEADDIM) == sub
                    part = _dot(sc, jnp.where(half, xp, jnp.zeros_like(xp)))
                    acc = part if acc is None else acc + part
                pieces.append(acc)
            y = y + jnp.concatenate(pieces, axis=1)
            y_ref[rows, :] += y
            upd = _dot_tn((xdt * jnp.exp(cum_end - cum_i)).astype(MXU_DTYPE), bcv)
            for hh in range(SSD_GH):
                blk = slice(hh * SSD_HEADDIM, (hh + 1) * SSD_HEADDIM)
                h_s[blk, :] = h_s[blk, :] * jnp.exp(cum_t[hh:hh + 1, end:end + 1]) + upd[blk, :]
            return carry

        lax.fori_loop(0, n_chunk, chunk, 0)
        if not with_state:
            hT_ref[d] = h_s[...].reshape(SSD_GH, SSD_HEADDIM, SSD_STATE)


def _ssd_call(p, s, row_off, bsz, n_tok, lp, h0):
    with_state = h0 is not None
    sb = row_off // n_tok
    assert row_off % n_tok == 0 and n_tok % SSD_L == 0

    def seq_spec(col0, w):
        return pl.BlockSpec((n_tok, w), lambda b, g: (sb + b, col0 // w + g))

    def conv_specs(col0, w):
        return [pl.BlockSpec((CONV_W, w), lambda b, g: (0, col0 // w + g)),
                pl.BlockSpec((1, w), lambda b, g: (0, col0 // w + g))]

    small = pl.BlockSpec((1, S_W), lambda b, g: (0, 0))
    pad = jnp.zeros((S_W - 2 * SSD_HEADS,), jnp.float32)
    dtb = jnp.concatenate([lp['ssd_dt_bias'].reshape(-1), pad]).reshape(1, S_W)
    alog = jnp.concatenate([lp['ssd_a_log'].reshape(-1), pad]).reshape(1, S_W)
    dskip = jnp.repeat(lp['ssd_d'][0] + lp['ssd_d'][1], SSD_HEADDIM).reshape(1, GROUP_W)
    cw, cb = lp['ssd_conv_w'], lp['ssd_conv_b'].reshape(1, -1)
    in_specs = ([seq_spec(P_SX, SSD_GW), seq_spec(P_SB, SSD_STATE), seq_spec(P_SC, SSD_STATE),
                 pl.BlockSpec((n_tok, S_W), lambda b, g: (sb + b, 0))]
                + conv_specs(0, SSD_GW) + conv_specs(GROUP_W, SSD_STATE)
                + conv_specs(GROUP_W + SSD_GROUPS * SSD_STATE, SSD_STATE)
                + [small, small, pl.BlockSpec((1, SSD_GW), lambda b, g: (0, g))])
    args = [p, p, p, s, cw, cb, cw, cb, cw, cb, dtb, alog, dskip]
    state_block = pl.BlockSpec((None, 2, SSD_GH, SSD_HEADDIM, SSD_STATE), lambda b, g: (b, 0, g, 0, 0))
    out_specs = [pl.BlockSpec((n_tok, SSD_GW), lambda b, g: (b, g))]
    out_shape = [jax.ShapeDtypeStruct((bsz * n_tok, GROUP_W), jnp.float32)]
    if with_state:
        in_specs.append(state_block)
        args.append(h0)
    else:
        out_specs.append(state_block)
        out_shape.append(jax.ShapeDtypeStruct((bsz, 2, SSD_HEADS, SSD_HEADDIM, SSD_STATE), jnp.float32))
    return pl.pallas_call(
        functools.partial(_ssd_kernel, n_tok=n_tok, with_state=with_state),
        grid=(bsz, SSD_GROUPS),
        in_specs=in_specs,
        out_specs=out_specs,
        out_shape=out_shape,
        scratch_shapes=[pltpu.VMEM((n_tok, SSD_GW), jnp.float32), pltpu.VMEM((n_tok, SSD_GW), jnp.float32),
                        pltpu.VMEM((n_tok, SSD_STATE), MXU_DTYPE), pltpu.VMEM((n_tok, SSD_STATE), MXU_DTYPE),
                        pltpu.VMEM((n_tok, S_W), jnp.float32),
                        pltpu.VMEM((SSD_GW, SSD_STATE), jnp.float32)],
        compiler_params=_params(2, VMEM_LIMIT_V7X),
        name="ssd_scan",
    )(*args)


def _gated_norm_kernel(y_ref, z_ref, w_ref, o_ref):
    o_ref[...] = _rms_rows(y_ref[...] * _silu(z_ref[...]), w_ref[...]).astype(o_ref.dtype)


def _ssd_finish(y, p, row_off, norm_w, out_prev):
    tm = 256
    m, n = p.shape[0], y.shape[0]
    rb = row_off // tm
    in_specs = [pl.BlockSpec((tm, GROUP_W), lambda i: (i, 0)),
                pl.BlockSpec((tm, GROUP_W), lambda i: (rb + i, P_SZ // GROUP_W)),
                pl.BlockSpec((1, GROUP_W), lambda i: (0, 0))]
    args = [y, p, norm_w.reshape(1, GROUP_W)]
    aliases = {}
    if out_prev is not None:
        in_specs.append(pl.BlockSpec(memory_space=pl.ANY))
        args.append(out_prev)
        aliases = {3: 0}

    def body(y_ref, z_ref, w_ref, *rest):
        _gated_norm_kernel(y_ref, z_ref, w_ref, rest[-1])

    return pl.pallas_call(
        body,
        grid=(n // tm,),
        in_specs=in_specs,
        out_specs=pl.BlockSpec((tm, GROUP_W), lambda i: (rb + i, 0)),
        out_shape=jax.ShapeDtypeStruct((m, GROUP_W), MXU_DTYPE),
        input_output_aliases=aliases,
        compiler_params=_params(1, 32 << 20),
        name="ssd_gated_norm",
    )(*args)


GLA_L = 16
GLA_GROUP = 16


def _gla_kernel(*refs, n_tok, with_state):
    if with_state:
        (q_ref, k_ref, v_ref, gg_ref, s_ref, gw_ref, gb_ref, nw_ref, s0_ref, o_ref,
         dec_s, qe_s, ke_s, att_s, o_s, st_s) = refs
    else:
        (q_ref, k_ref, v_ref, gg_ref, s_ref, gw_ref, gb_ref, nw_ref, o_ref, sT_ref,
         dec_s, qe_s, ke_s, att_s, o_s, st_s) = refs
    L = GLA_L
    n_blk = n_tok // L
    blk3 = (n_blk, L, GLA_DK)
    q3 = (q_ref[...] * (GLA_DK ** -0.5)).reshape(blk3)
    k3 = k_ref[...].reshape(blk3)
    sb16 = s_ref[...].astype(MXU_DTYPE)
    t = _row_iota((n_tok, GLA_DK))
    ri3 = lax.broadcasted_iota(jnp.int32, blk3, 1)
    lane3 = lax.broadcasted_iota(jnp.int32, blk3, 2)

    for d in range(2):
        up = d == 1
        end = 0 if up else L - 1
        gate = _dot(sb16, gw_ref[d]) + gb_ref[d]
        b = -_softplus(-gate) / GLA_NORMALIZER
        s = 1
        while s < L:
            ok = (t % L < L - s) if up else (t % L >= s)
            b = b + jnp.where(ok, pltpu.roll(b, (n_tok - s) if up else s, 0), 0.0)
            s *= 2
        b3 = b.reshape(blk3)
        b_end = b3[:, end:end + 1, :]
        dec_s[...] = jnp.exp(jnp.broadcast_to(b_end, blk3)).reshape(n_tok, GLA_DK)
        qe_s[...] = (q3 * jnp.exp(b3)).reshape(n_tok, GLA_DK).astype(MXU_DTYPE)
        ke_s[...] = (k3 * jnp.exp(b_end - b3)).reshape(n_tok, GLA_DK).astype(MXU_DTYPE)
        att = jnp.zeros(blk3, jnp.float32)
        for j in range(L):
            reach = (ri3 <= j) if up else (ri3 >= j)
            w = jnp.where(reach, jnp.exp(b3 - b3[:, j:j + 1, :]), 0.0) * q3 * k3[:, j:j + 1, :]
            att = jnp.where(lane3 == j, jnp.sum(w, axis=-1, keepdims=True), att)
        att_s[...] = att.reshape(n_tok, GLA_DK).astype(MXU_DTYPE)
        if with_state:
            st_s[...] = s0_ref[d].T
        else:
            st_s[...] = jnp.zeros_like(st_s)

        def group(i, carry):
            c = (n_grp - 1 - i) if up else i
            base = pl.multiple_of(c * (GLA_GROUP * L), GLA_GROUP * L)
            vg = v_ref[pl.ds(base, GLA_GROUP * L), :].astype(MXU_DTYPE)
            keg = ke_s[pl.ds(base, GLA_GROUP * L), :]
            attg = att_s[pl.ds(base, GLA_GROUP * L), 0:L]
            qeg = qe_s[pl.ds(base, GLA_GROUP * L), :]
            decg = dec_s[pl.ds(base, GLA_GROUP * L), :]
            order = range(GLA_GROUP - 1, -1, -1) if up else range(GLA_GROUP)
            upd = {k: _dot_tn(vg[k * L:(k + 1) * L], keg[k * L:(k + 1) * L]) for k in order}
            intra = {k: _dot(attg[k * L:(k + 1) * L], vg[k * L:(k + 1) * L]) for k in order}
            st = st_s[...]
            outs = {}
            for k in order:
                outs[k] = _dot_nt(qeg[k * L:(k + 1) * L], st.astype(MXU_DTYPE)) + intra[k]
                st = st * decg[k * L:k * L + 1, :] + upd[k]
            st_s[...] = st
            o = jnp.concatenate([outs[k] for k in range(GLA_GROUP)], axis=0)
            if up:
                o_s[pl.ds(base, GLA_GROUP * L), :] += o
            else:
                o_s[pl.ds(base, GLA_GROUP * L), :] = o
            return carry

        n_grp = n_blk // GLA_GROUP
        lax.fori_loop(0, n_grp, group, 0)
        if not with_state:
            sT_ref[d] = st_s[...].T
    o = _rms_rows(o_s[...], nw_ref[...]) * _silu(gg_ref[...])
    o_ref[...] = o.astype(o_ref.dtype)


def _gla_call(p, s, row_off, bsz, n_tok, lp, s0, out_prev):
    m = p.shape[0]
    with_state = s0 is not None
    sb = row_off // n_tok
    assert row_off % n_tok == 0 and n_tok % GLA_L == 0

    def seq_spec(col0, w):
        return pl.BlockSpec((n_tok, w), lambda b, h: (sb + b, col0 // w + h))

    gw = lp['gla_gate_w'].reshape(2, GLA_RANK, GLA_HEADS, GLA_DK).transpose(0, 2, 1, 3)
    gw_rows = jnp.zeros((2, GLA_HEADS, S_W, GLA_DK), jnp.float32)
    for d in range(2):
        r0 = 2 * SSD_HEADS + d * GLA_RANK
        gw_rows = gw_rows.at[d, :, r0:r0 + GLA_RANK, :].set(gw[d])
    in_specs = [seq_spec(P_GQ, GLA_DK), seq_spec(P_GK, GLA_DK), seq_spec(P_GV, GLA_DV), seq_spec(P_GG, GLA_DV),
                pl.BlockSpec((n_tok, S_W), lambda b, h: (sb + b, 0)),
                pl.BlockSpec((2, None, S_W, GLA_DK), lambda b, h: (0, h, 0, 0)),
                pl.BlockSpec((2, None, 1, GLA_DK), lambda b, h: (0, h, 0, 0)),
                pl.BlockSpec((1, GLA_DV), lambda b, h: (0, 0))]
    args = [p, p, p, p, s, gw_rows.astype(MXU_DTYPE), lp['gla_gate_b'].reshape(2, GLA_HEADS, 1, GLA_DK),
            lp['gla_norm_w'].reshape(1, GLA_DV)]
    state_block = pl.BlockSpec((None, 2, None, GLA_DK, GLA_DV), lambda b, h: (b, 0, h, 0, 0))
    out_specs = [pl.BlockSpec((n_tok, GLA_DV), lambda b, h: (sb + b, h))]
    out_shape = [jax.ShapeDtypeStruct((m, GROUP_W), MXU_DTYPE)]
    if with_state:
        in_specs.append(state_block)
        args.append(s0)
    else:
        out_specs.append(state_block)
        out_shape.append(jax.ShapeDtypeStruct((bsz, 2, GLA_HEADS, GLA_DK, GLA_DV), jnp.float32))
    n_in = len(args)
    aliases = {}
    if out_prev is not None:
        in_specs.append(pl.BlockSpec(memory_space=pl.ANY))
        args.append(out_prev)
        aliases = {n_in: 0}

    def body(*refs):
        if out_prev is not None:
            refs = refs[:n_in] + refs[n_in + 1:]
        _gla_kernel(*refs, n_tok=n_tok, with_state=with_state)

    return pl.pallas_call(
        body,
        grid=(bsz, GLA_HEADS),
        in_specs=in_specs,
        out_specs=out_specs,
        out_shape=out_shape,
        scratch_shapes=[pltpu.VMEM((n_tok, GLA_DK), jnp.float32), pltpu.VMEM((n_tok, GLA_DK), MXU_DTYPE),
                        pltpu.VMEM((n_tok, GLA_DK), MXU_DTYPE), pltpu.VMEM((n_tok, GLA_DK), MXU_DTYPE),
                        pltpu.VMEM((n_tok, GLA_DV), jnp.float32), pltpu.VMEM((GLA_DV, GLA_DK), jnp.float32)],
        input_output_aliases=aliases,
        compiler_params=_params(2, 48 << 20),
        name="gla_scan",
    )(*args)


def _mixers(p, s, lp, cached):
    ctx_k, ctx_v, ssd0, gla0, lru0 = cached
    kv_w = ATT_KV_HEADS * HEAD_DIM
    att, k_new = _attention_call(p, 0, BATCH, SEQ, lp['q_norm'], lp['k_norm'], None, None)
    att, = _attention_call(p, M_PROMPT, DEC_BATCH, DEC_SEQ, lp['q_norm'], lp['k_norm'],
                           (ctx_k.reshape(DEC_BATCH, PAST_LEN, kv_w), ctx_v.reshape(DEC_BATCH, PAST_LEN, kv_w)), att)
    v_new = p[:M_PROMPT, P_AV:P_AV + kv_w]

    y_p, ssd_new = _ssd_call(p, s, 0, BATCH, SEQ, lp, None)
    y_s, = _ssd_call(p, s, M_PROMPT, DEC_BATCH, DEC_SEQ, lp, ssd0)
    ssd = _ssd_finish(y_p, p, 0, lp['ssd_norm_w'], None)
    ssd = _ssd_finish(y_s, p, M_PROMPT, lp['ssd_norm_w'], ssd)

    gla, gla_new = _gla_call(p, s, 0, BATCH, SEQ, lp, None, None)
    gla, = _gla_call(p, s, M_PROMPT, DEC_BATCH, DEC_SEQ, lp, gla0, gla)

    lru, lru_new = _lru_call(p, 0, BATCH, SEQ, lp, None, None)
    lru, = _lru_call(p, M_PROMPT, DEC_BATCH, DEC_SEQ, lp, lru0, lru)

    new_ctx = (k_new.reshape(BATCH, SEQ, ATT_KV_HEADS, HEAD_DIM), v_new.reshape(BATCH, SEQ, ATT_KV_HEADS, HEAD_DIM),
               ssd_new, gla_new, lru_new.reshape(BATCH, 2, LRU_W))
    return [att, ssd, gla, lru], new_ctx


IN_W = 9280
_W_IN_MAIN = ((0, P_AQ, 1024), (1024, P_AK, 256), (1280, P_AV, 256), (1536, P_SX, 1024), (2560, P_SZ, 1024),
              (3584, P_SB, 256), (3840, P_SC, 256), (4128, P_GQ, 512), (4640, P_GK, 512), (5152, P_GV, 1024),
              (6208, P_GG, 1024), (7232, P_LX, 1024), (8256, P_LG, 1024))
_W_IN_SMALL = ((4096, 0, 2 * SSD_HEADS), (6176, 2 * SSD_HEADS, 2 * GLA_RANK))


def _w_in_kernel(w_ref, main_ref, small_ref):
    for src, dst, width in _W_IN_MAIN:
        main_ref[:, dst:dst + width] = w_ref[:, src:src + width].astype(main_ref.dtype)
    small_ref[...] = jnp.zeros_like(small_ref)
    for src, dst, width in _W_IN_SMALL:
        small_ref[:, dst:dst + width] = w_ref[:, src:src + width].astype(small_ref.dtype)


def _reorder_w_in(w):
    tk = 256
    return pl.pallas_call(
        _w_in_kernel,
        grid=(D_MODEL // tk,),
        in_specs=[pl.BlockSpec((tk, IN_W), lambda i: (i, 0))],
        out_specs=[pl.BlockSpec((tk, P_W), lambda i: (i, 0)), pl.BlockSpec((tk, S_W), lambda i: (i, 0))],
        out_shape=[jax.ShapeDtypeStruct((D_MODEL, P_W), MXU_DTYPE), jax.ShapeDtypeStruct((D_MODEL, S_W), MXU_DTYPE)],
        compiler_params=_params(1, 48 << 20),
        name="w_in_regroup",
    )(w)


def kernel(x_prompt, x_sample, c, cache_attn_k, cache_attn_v, state_ssd, state_gla, state_lru, c_ctx,
           mod_w, mod_b, ln_g, ln_b, ffn_w_gate, ffn_w_up, ffn_w_down, w_in, w_out, q_norm, k_norm,
           ssd_conv_w, ssd_conv_b, ssd_a_log, ssd_dt_bias, ssd_d, ssd_norm_w,
           gla_gate_w, gla_gate_b, gla_norm_w,
           lru_conv_w, lru_conv_b, lru_wa, lru_ba, lru_wx, lru_bx, lru_lam):
    cond = jnp.concatenate([c_ctx[None], c, jnp.zeros((COND_ROWS - N_COND, D_MODEL), jnp.float32)], axis=0)
    mod_all = _mod_table(cond, mod_w, mod_b)

    x = jnp.concatenate([x_prompt.reshape(M_PROMPT, D_MODEL), x_sample.reshape(M_SAMPLE, D_MODEL)], axis=0)
    xm = _modulate(x, mod_all[0], 1, 0)
    ctx_out = []
    for l in range(DEPTH):
        mod = mod_all[l]
        lp = dict(q_norm=q_norm[l], k_norm=k_norm[l], ssd_conv_w=ssd_conv_w[l], ssd_conv_b=ssd_conv_b[l],
                  ssd_a_log=ssd_a_log[l], ssd_dt_bias=ssd_dt_bias[l], ssd_d=ssd_d[l], ssd_norm_w=ssd_norm_w[l],
                  gla_gate_w=gla_gate_w[l], gla_gate_b=gla_gate_b[l], gla_norm_w=gla_norm_w[l],
                  lru_conv_w=lru_conv_w[l], lru_conv_b=lru_conv_b[l], lru_wa=lru_wa[l], lru_ba=lru_ba[l],
                  lru_wx=lru_wx[l], lru_bx=lru_bx[l], lru_lam=lru_lam[l])
        h = _ffn_up(xm, ffn_w_gate[l, 0].astype(MXU_DTYPE), ffn_w_up[l, 0].astype(MXU_DTYPE))
        f = _matmul([h], ffn_w_down[l, 0].astype(MXU_DTYPE), jnp.float32, 512, 512, "ffn_down")
        x, xm = _resid_ln(x, f, mod, 2, 0.5, ln_g[l, 0], ln_b[l, 0], mod, 4, 3)
        w_main, w_small = _reorder_w_in(w_in[l])
        p = _matmul([xm], w_main, jnp.float32, 1024, 1024, "proj_in")
        s = _matmul([xm], w_small, jnp.float32, 1024, S_W, "proj_in_small")
        cached = (cache_attn_k[:, l], cache_attn_v[:, l], state_ssd[:, l], state_gla[:, l], state_lru[:, l])
        mix, new_ctx = _mixers(p, s, lp, cached)
        ctx_out.append(new_ctx)
        m_out = _matmul(mix, w_out[l].astype(MXU_DTYPE), jnp.float32, 1024, 1024, "proj_out")
        x, xm = _resid_ln(x, m_out, mod, 5, 1.0, ln_g[l, 1], ln_b[l, 1], mod, 7, 6)
        h = _ffn_up(xm, ffn_w_gate[l, 1].astype(MXU_DTYPE), ffn_w_up[l, 1].astype(MXU_DTYPE))
        f = _matmul([h], ffn_w_down[l, 1].astype(MXU_DTYPE), jnp.float32, 512, 512, "ffn_down")
        if l + 1 < DEPTH:
            x, xm = _resid_ln(x, f, mod, 8, 0.5, ln_g[l, 2], ln_b[l, 2], mod_all[l + 1], 1, 0)
        else:
            x, _ = _resid_ln(x, f, mod, 8, 0.5, ln_g[l, 2], ln_b[l, 2])

    y_prompt = x[:M_PROMPT].reshape(BATCH, SEQ, D_MODEL)
    y_sample = x[M_PROMPT:].reshape(DEC_BATCH, DEC_SEQ, D_MODEL)
    new_k, new_v, new_ssd, new_gla, new_lru = (jnp.stack([s_[i] for s_ in ctx_out], axis=1) for i in range(5))
    return (y_prompt, y_sample, new_k, new_v, new_ssd, new_gla, new_lru)
```

```python
import functools
import math

import numpy as np
import jax
import jax.numpy as jnp
from jax import lax
from jax.experimental import pallas as pl
from jax.experimental.pallas import tpu as pltpu

D_MODEL = 4096
BATCH = 16
SEQ = 256
DEPTH = 2
DEC_BATCH = 4
DEC_SEQ = 2048
PAST_LEN = 256
GRID_W = 64
GROUP_W = 1024
HEAD_DIM = 128
ATT_HEADS = 8
ATT_KV_HEADS = 2
ROPE_THETA = 10000.0
SSD_HEADDIM = 64
SSD_HEADS = 16
SSD_STATE = 128
SSD_GROUPS = 2
CONV_W = 4
GLA_HEADS = 4
GLA_DK = 128
GLA_DV = 256
GLA_RANK = 16
GLA_NORMALIZER = 16.0
LRU_W = 1024
LRU_BLOCKS = 8
LRU_BW = 128
LRU_C = 8.0
D_FF = 11008
N_MOD = 9
LN_EPS = 1e-5
RMS_EPS = 1e-6
ALPHA = (2.0 * DEPTH) ** 0.25

M_PROMPT = BATCH * SEQ
M_SAMPLE = DEC_BATCH * DEC_SEQ
M_TOK = M_PROMPT + M_SAMPLE
N_COND = 1 + DEC_BATCH
COND_ROWS = 8

VMEM_LIMIT_V7X = 56 * 1024 * 1024

MXU_DTYPE = jnp.bfloat16

P_AQ, P_SX, P_SZ, P_GV, P_GG, P_LX, P_LG = 0, 1024, 2048, 3072, 4096, 5120, 6144
P_GQ, P_GK = 7168, 7680
P_AK, P_AV, P_SB, P_SC = 8192, 8448, 8704, 8960
P_W = 9216
S_W = 128


def _params(n_axes, vmem_bytes):
    return pltpu.CompilerParams(dimension_semantics=("arbitrary",) * n_axes,
                                vmem_limit_bytes=min(int(vmem_bytes), VMEM_LIMIT_V7X))


def _cond_row(tile_idx, tile_rows):
    row0 = tile_idx * tile_rows
    return jnp.where(row0 < M_PROMPT, 0, 1 + (row0 - M_PROMPT) // DEC_SEQ)


def _mod_kernel(c_ref, w_ref, b_ref, o_ref):
    c = c_ref[...]
    a = (c * jax.nn.sigmoid(c)).astype(MXU_DTYPE)
    w = w_ref[...].astype(MXU_DTYPE)
    o_ref[...] = jnp.dot(a, w, preferred_element_type=jnp.float32) + b_ref[...]


def _mod_table(cond, mod_w, mod_b):
    tn = 512
    n = N_MOD * D_MODEL
    return pl.pallas_call(
        _mod_kernel,
        grid=(DEPTH, n // tn),
        in_specs=[pl.BlockSpec((COND_ROWS, D_MODEL), lambda l, j: (0, 0)),
                  pl.BlockSpec((None, D_MODEL, tn), lambda l, j: (l, 0, j)),
                  pl.BlockSpec((None, 1, tn), lambda l, j: (l, 0, j))],
        out_specs=pl.BlockSpec((None, COND_ROWS, tn), lambda l, j: (l, 0, j)),
        out_shape=jax.ShapeDtypeStruct((DEPTH, COND_ROWS, n), jnp.float32),
        compiler_params=_params(2, 3 * D_MODEL * tn * 4 + (4 << 20)),
        name="mod_table",
    )(cond, mod_w, mod_b.reshape(DEPTH, 1, n))


def _dense_kernel(*refs, n_a, n_w, cast, swiglu):
    a_refs, w_refs, o_ref = refs[:n_a], refs[n_a:n_a + n_w], refs[n_a + n_w]
    cast_refs = refs[n_a + n_w + 1:n_a + 2 * n_w + 1] if cast else ()
    accs = []
    for wi, w_ref in enumerate(w_refs):
        if cast:
            w_narrow = w_ref[...].astype(MXU_DTYPE)
            cast_refs[wi][...] = w_narrow
        k0, acc = 0, None
        for a_ref in a_refs:
            kp = a_ref.shape[1]
            w_rows = w_narrow[k0:k0 + kp, :] if cast else w_ref[k0:k0 + kp, :]
            part = jnp.dot(a_ref[...], w_rows, preferred_element_type=jnp.float32)
            acc = part if acc is None else acc + part
            k0 += kp
        accs.append(acc)
    out = accs[0] * jax.nn.sigmoid(accs[0]) * accs[1] if swiglu else accs[0]
    o_ref[...] = out.astype(o_ref.dtype)


def _dense(a_list, w_list, w_index, *, tile0, n_tiles, tm, tn, out_dtype, name, swiglu=False, cast=False,
           out_prev=None):
    m = a_list[0].shape[0]
    k, n = w_list[0].shape[-2:]
    n_a, n_w, lead = len(a_list), len(w_list), len(w_index)
    assert sum(a.shape[1] for a in a_list) == k and n % tn == 0 and (tile0 + n_tiles) * tm <= m
    a_mode = dict(pipeline_mode=pl.Buffered(1)) if n_tiles == 1 else {}
    in_specs = [pl.BlockSpec((tm, a.shape[1]), lambda i, j: (tile0 + i, 0), **a_mode) for a in a_list]
    in_specs += [pl.BlockSpec((None,) * lead + (k, tn), lambda i, j: tuple(w_index) + (0, j)) for _ in w_list]
    args = list(a_list) + list(w_list)
    out_specs = [pl.BlockSpec((tm, tn), lambda i, j: (tile0 + i, j))]
    out_shape = [jax.ShapeDtypeStruct((m, n), out_dtype)]
    if cast:
        out_specs += [pl.BlockSpec((k, tn), lambda i, j: (0, j)) for _ in w_list]
        out_shape += [jax.ShapeDtypeStruct((k, n), MXU_DTYPE) for _ in w_list]
    aliases = {}
    if out_prev is not None:
        in_specs.append(pl.BlockSpec(memory_space=pl.ANY))
        args.append(out_prev)
        aliases = {n_a + n_w: 0}
    esz, wsz, osz = jnp.dtype(MXU_DTYPE).itemsize, w_list[0].dtype.itemsize, jnp.dtype(out_dtype).itemsize
    vmem = ((1 if n_tiles == 1 else 2) * tm * k * esz + 2 * n_w * k * tn * wsz + 2 * tm * tn * osz
            + (n_w + 1) * tm * tn * 4 + (3 * n_w * k * tn * esz if cast else 0))

    def body(*refs):
        if out_prev is not None:
            refs = refs[:n_a + n_w] + refs[n_a + n_w + 1:]
        _dense_kernel(*refs, n_a=n_a, n_w=n_w, cast=cast, swiglu=swiglu)

    out = pl.pallas_call(
        body,
        grid=(n_tiles, n // tn),
        in_specs=in_specs,
        out_specs=out_specs,
        out_shape=out_shape,
        input_output_aliases=aliases,
        compiler_params=_params(2, vmem + (4 << 20)),
        name=name,
    )(*args)
    return out if cast else out[0]


def _dense_cast_first(a_list, w_list, w_index, *, tm_first, tn_first, tm, tn, out_dtype, name, swiglu=False):
    m = a_list[0].shape[0]
    first = _dense(a_list, w_list, w_index, tile0=0, n_tiles=1, tm=tm_first, tn=tn_first, out_dtype=out_dtype,
                   name=name + "_first", swiglu=swiglu, cast=True)
    assert tm_first % tm == 0 or tm_first == tm
    return _dense(a_list, first[1:], (), tile0=tm_first // tm, n_tiles=(m - tm_first) // tm, tm=tm, tn=tn,
                  out_dtype=out_dtype, name=name, swiglu=swiglu, out_prev=first[0])


def _modulate_kernel(xp_ref, xs_ref, sc_ref, sh_ref, x_ref, o_ref, *, tm):
    i = pl.program_id(0)
    r = _cond_row(i, tm)
    sc = sc_ref[pl.ds(r, 1), :]
    sh = sh_ref[pl.ds(r, 1), :]

    def emit(x):
        x_ref[...] = x
        o_ref[...] = (x * (1.0 + sc) + sh).astype(o_ref.dtype)

    pl.when(i < M_PROMPT // tm)(lambda: emit(xp_ref[...]))
    pl.when(i >= M_PROMPT // tm)(lambda: emit(xs_ref[...]))


def _mod_spec(k):
    return pl.BlockSpec((COND_ROWS, D_MODEL), lambda i: (0, k))


def _modulate(x_prompt, x_sample, mod, k_scale, k_shift):
    tm = 256
    n_p = M_PROMPT // tm
    row = pl.BlockSpec((tm, D_MODEL), lambda i: (i, 0))
    return pl.pallas_call(
        functools.partial(_modulate_kernel, tm=tm),
        grid=(M_TOK // tm,),
        in_specs=[pl.BlockSpec((tm, D_MODEL), lambda i: (jnp.minimum(i, n_p - 1), 0)),
                  pl.BlockSpec((tm, D_MODEL), lambda i: (jnp.maximum(i - n_p, 0), 0)),
                  _mod_spec(k_scale), _mod_spec(k_shift)],
        out_specs=[row, row],
        out_shape=[jax.ShapeDtypeStruct((M_TOK, D_MODEL), jnp.float32),
                   jax.ShapeDtypeStruct((M_TOK, D_MODEL), MXU_DTYPE)],
        compiler_params=_params(1, 32 << 20),
        name="modulate",
    )(x_prompt.reshape(M_PROMPT, D_MODEL), x_sample.reshape(M_SAMPLE, D_MODEL), mod, mod)


def _resid_ln_kernel(*refs, tm, gate_scale, with_next):
    if with_next:
        x_ref, f_ref, g_ref, lg_ref, lb_ref, sc_ref, sh_ref, xo_ref, mo_ref = refs
    else:
        x_ref, f_ref, g_ref, lg_ref, lb_ref, yp_ref, ys_ref = refs
    i = pl.program_id(0)
    r = _cond_row(i, tm)
    g = g_ref[pl.ds(r, 1), :]
    y = ALPHA * x_ref[...] + (gate_scale * g) * f_ref[...]
    mu = jnp.mean(y, axis=-1, keepdims=True)
    yc = y - mu
    var = jnp.mean(yc * yc, axis=-1, keepdims=True)
    xn = yc * lax.rsqrt(var + LN_EPS) * lg_ref[...] + lb_ref[...]
    if with_next:
        xo_ref[...] = xn
        sc = sc_ref[pl.ds(r, 1), :]
        sh = sh_ref[pl.ds(r, 1), :]
        mo_ref[...] = (xn * (1.0 + sc) + sh).astype(mo_ref.dtype)
    else:
        @pl.when(i < M_PROMPT // tm)
        def _():
            yp_ref[...] = xn

        @pl.when(i >= M_PROMPT // tm)
        def _():
            ys_ref[...] = xn


def _resid_ln(x, f, mod, k_gate, gate_scale, ln_g, ln_b, next_mod=None, k_scale=0, k_shift=0):
    tm = 256
    m = x.shape[0]
    n_p = M_PROMPT // tm
    with_next = next_mod is not None
    row = pl.BlockSpec((tm, D_MODEL), lambda i: (i, 0))
    vec = pl.BlockSpec((1, D_MODEL), lambda i: (0, 0))
    in_specs = [row, row, _mod_spec(k_gate), vec, vec]
    args = [x, f, mod, ln_g.reshape(1, D_MODEL), ln_b.reshape(1, D_MODEL)]
    if with_next:
        in_specs += [_mod_spec(k_scale), _mod_spec(k_shift)]
        args += [next_mod, next_mod]
        out_specs = [row, row]
        out_shape = [jax.ShapeDtypeStruct((m, D_MODEL), jnp.float32), jax.ShapeDtypeStruct((m, D_MODEL), MXU_DTYPE)]
    else:
        out_specs = [pl.BlockSpec((tm, D_MODEL), lambda i: (jnp.minimum(i, n_p - 1), 0)),
                     pl.BlockSpec((tm, D_MODEL), lambda i: (jnp.maximum(i - n_p, 0), 0))]
        out_shape = [jax.ShapeDtypeStruct((M_PROMPT, D_MODEL), jnp.float32),
                     jax.ShapeDtypeStruct((M_SAMPLE, D_MODEL), jnp.float32)]
    out = pl.pallas_call(
        functools.partial(_resid_ln_kernel, tm=tm, gate_scale=gate_scale, with_next=with_next),
        grid=(m // tm,),
        in_specs=in_specs,
        out_specs=out_specs,
        out_shape=out_shape,
        compiler_params=_params(1, 48 << 20),
        name="resid_ln",
    )(*args)
    return out[0], out[1]


def _row_iota(shape):
    return lax.broadcasted_iota(jnp.int32, shape, 0)


def _lane_iota(shape):
    return lax.broadcasted_iota(jnp.int32, shape, 1)


def _shift_rows(x, s, fill, up=False):
    n = x.shape[0]
    t = _row_iota(x.shape)
    if up:
        return jnp.where(t < n - s, pltpu.roll(x, n - s, 0), fill)
    return jnp.where(t >= s, pltpu.roll(x, s, 0), fill)


def _softplus(z):
    return jnp.maximum(z, 0.0) + jnp.log1p(jnp.exp(-jnp.abs(z)))


def _silu(z):
    return z * jax.nn.sigmoid(z)


def _dot(a, b):
    return jnp.dot(a, b, preferred_element_type=jnp.float32)


def _dot_nt(a, b):
    return lax.dot_general(a, b, (((1,), (1,)), ((), ())), preferred_element_type=jnp.float32)


def _dot_tn(a, b):
    return lax.dot_general(a, b, (((0,), (0,)), ((), ())), preferred_element_type=jnp.float32)


def _split3(x):
    hi = x.astype(MXU_DTYPE)
    r1 = x - hi.astype(jnp.float32)
    mid = r1.astype(MXU_DTYPE)
    lo = (r1 - mid.astype(jnp.float32)).astype(MXU_DTYPE)
    return hi, mid, lo


def _dot01_right(x, e):
    hi, mid, lo = _split3(x)
    return _dot(hi, e) + _dot(mid, e) + _dot(lo, e)


def _dot01_left(e, x):
    hi, mid, lo = _split3(x)
    return _dot(e, hi) + _dot(e, mid) + _dot(e, lo)


def _conv4(x, w_ref, b_ref):
    acc = _shift_rows(x, 2, 0.0) * w_ref[0:1, :]
    acc = acc + _shift_rows(x, 1, 0.0) * w_ref[1:2, :]
    acc = acc + x * w_ref[2:3, :]
    acc = acc + _shift_rows(x, 1, 0.0, up=True) * w_ref[3:4, :]
    return acc + b_ref[...]


def _rms_rows(x, w):
    return x * lax.rsqrt(jnp.mean(x * x, axis=-1, keepdims=True) + RMS_EPS) * w


def _rope_tables(n_tok):
    rows = n_tok // GRID_W
    row = jnp.repeat(jnp.arange(rows, dtype=jnp.float32), GRID_W)
    col = jnp.tile(jnp.arange(GRID_W, dtype=jnp.float32), rows)
    n_freq = HEAD_DIM // 4
    inv = ROPE_THETA ** (-jnp.arange(n_freq, dtype=jnp.float32) / n_freq)
    ar, ac = row[:, None] * inv, col[:, None] * inv
    cos = jnp.concatenate([jnp.cos(ar), jnp.cos(ar), jnp.cos(ac), jnp.cos(ac)], axis=1)
    sin = jnp.concatenate([-jnp.sin(ar), jnp.sin(ar), -jnp.sin(ac), jnp.sin(ac)], axis=1)
    return cos, sin


def _rope(x, cos, sin):
    quarter = HEAD_DIM // 4
    lane = _lane_iota(x.shape)
    partner = jnp.where(lane % (2 * quarter) < quarter,
                        pltpu.roll(x, HEAD_DIM - quarter, 1), pltpu.roll(x, quarter, 1))
    return x * cos + partner * sin


def _attn_kernel(*refs, n_tok, tq, n_ctx, rope):
    if rope:
        (q_ref, k_ref, v_ref, qn_ref, kn_ref, cq_ref, sq_ref, ck_ref, sk_ref, xk_ref, xv_ref,
         o_ref, ks_ref, vs_ref) = refs
    else:
        q_ref, k_ref, v_ref, qn_ref, kn_ref, o_ref, ko_ref, ks_ref, vs_ref = refs
    rep = ATT_HEADS // ATT_KV_HEADS

    @pl.when(pl.program_id(2) == 0)
    def _():
        kn = _rms_rows(k_ref[...], kn_ref[...])
        if rope:
            kn = _rope(kn, ck_ref[...], sk_ref[...])
            ks_ref[n_tok:n_tok + n_ctx, :] = xk_ref[...].astype(MXU_DTYPE)
            vs_ref[n_tok:n_tok + n_ctx, :] = xv_ref[...].astype(MXU_DTYPE)
        else:
            ko_ref[...] = kn
        ks_ref[0:n_tok, :] = kn.astype(MXU_DTYPE)
        vs_ref[0:n_tok, :] = v_ref[...].astype(MXU_DTYPE)

    heads = []
    for r in range(rep):
        qh = _rms_rows(q_ref[:, r * HEAD_DIM:(r + 1) * HEAD_DIM], qn_ref[...])
        if rope:
            qh = _rope(qh, cq_ref[...], sq_ref[...])
        heads.append(qh.astype(MXU_DTYPE))
    qs = jnp.concatenate(heads, axis=0)
    s = _dot_nt(qs, ks_ref[...]) * (HEAD_DIM ** -0.5)
    e = jnp.exp(s - jnp.max(s, axis=-1, keepdims=True))
    o = _dot(e.astype(MXU_DTYPE), vs_ref[...]) / jnp.sum(e, axis=-1, keepdims=True)
    for r in range(rep):
        o_ref[:, r * HEAD_DIM:(r + 1) * HEAD_DIM] = o[r * tq:(r + 1) * tq].astype(o_ref.dtype)


def _attention_call(p, row_off, bsz, n_tok, q_norm, k_norm, ctx_kv, out_prev):
    m = p.shape[0]
    rope = ctx_kv is not None
    tq = 128 if rope else n_tok
    nq = n_tok // tq
    rep = ATT_HEADS // ATT_KV_HEADS
    qw = rep * HEAD_DIM
    n_ctx = ctx_kv[0].shape[1] if rope else 0
    assert row_off % n_tok == 0 and n_tok % tq == 0
    rb, sb = row_off // tq, row_off // n_tok

    def seq_spec(col0):
        return pl.BlockSpec((n_tok, HEAD_DIM), lambda b, g, i: (sb + b, col0 // HEAD_DIM + g))

    vec = pl.BlockSpec((1, HEAD_DIM), lambda b, g, i: (0, 0))
    in_specs = [pl.BlockSpec((tq, qw), lambda b, g, i: (rb + b * nq + i, P_AQ // qw + g)),
                seq_spec(P_AK), seq_spec(P_AV), vec, vec]
    args = [p, p, p, q_norm.reshape(1, HEAD_DIM), k_norm.reshape(1, HEAD_DIM)]
    out_block = pl.BlockSpec((tq, qw), lambda b, g, i: (rb + b * nq + i, g))
    out_specs = [out_block]
    out_shape = [jax.ShapeDtypeStruct((m, GROUP_W), MXU_DTYPE)]
    aliases = {}
    if rope:
        cos, sin = _rope_tables(n_tok)
        in_specs += [pl.BlockSpec((tq, HEAD_DIM), lambda b, g, i: (i, 0))] * 2
        in_specs += [pl.BlockSpec((n_tok, HEAD_DIM), lambda b, g, i: (0, 0))] * 2
        in_specs += [pl.BlockSpec((None, n_ctx, HEAD_DIM), lambda b, g, i: (b, 0, g))] * 2
        args += [cos, sin, cos, sin, ctx_kv[0], ctx_kv[1]]
    else:
        out_specs.append(pl.BlockSpec((n_tok, HEAD_DIM), lambda b, g, i: (b, g)))
        out_shape.append(jax.ShapeDtypeStruct((bsz * n_tok, ATT_KV_HEADS * HEAD_DIM), jnp.float32))
    if out_prev is not None:
        in_specs.append(pl.BlockSpec(memory_space=pl.ANY))
        args.append(out_prev)
        aliases = {len(args) - 1: 0}
    n_keys = n_tok + n_ctx

    def body(*refs):
        if out_prev is not None:
            refs = refs[:len(args) - 1] + refs[len(args):]
        _attn_kernel(*refs, n_tok=n_tok, tq=tq, n_ctx=n_ctx, rope=rope)

    out = pl.pallas_call(
        body,
        grid=(bsz, ATT_KV_HEADS, nq),
        in_specs=in_specs,
        out_specs=out_specs,
        out_shape=out_shape,
        scratch_shapes=[pltpu.VMEM((n_keys, HEAD_DIM), MXU_DTYPE), pltpu.VMEM((n_keys, HEAD_DIM), MXU_DTYPE)],
        input_output_aliases=aliases,
        compiler_params=_params(3, 48 << 20),
        name="attention",
    )(*args)
    return out


def _lru_kernel(*refs, n_tok, with_state):
    if with_state:
        (x_ref, g_ref, cw_ref, cb_ref, wa_ref, ba_ref, wx_ref, bx_ref, lam_ref, h0_ref, o_ref) = refs
    else:
        (x_ref, g_ref, cw_ref, cb_ref, wa_ref, ba_ref, wx_ref, bx_ref, lam_ref, o_ref, hT_ref) = refs
    xl = _conv4(x_ref[...], cw_ref, cb_ref)
    xb = xl.astype(MXU_DTYPE)
    t = _row_iota(xl.shape)
    h_sum = None
    for d in range(2):
        up = d == 1
        r = jax.nn.sigmoid(_dot(xb, wa_ref[d]) + ba_ref[d])
        i = jax.nn.sigmoid(_dot(xb, wx_ref[d]) + bx_ref[d])
        log_a = -LRU_C * r * _softplus(-lam_ref[d])
        a = jnp.exp(log_a)
        th = jnp.tanh(log_a)
        u = jnp.sqrt(-2.0 * th / (1.0 - th)) * i * xl
        if with_state:
            first = (t == n_tok - 1) if up else (t == 0)
            u = jnp.where(first, u + a * h0_ref[d], u)
        s = 1
        while s < n_tok:
            u = u + a * _shift_rows(u, s, 0.0, up=up)
            a = a * _shift_rows(a, s, 1.0, up=up)
            s *= 2
        h_sum = u if h_sum is None else h_sum + u
        if not with_state:
            hT_ref[d] = u[0:1, :] if up else u[n_tok - 1:n_tok, :]
    o_ref[...] = (h_sum * jax.nn.gelu(g_ref[...])).astype(o_ref.dtype)


def _lru_call(p, row_off, bsz, n_tok, lp, h0, out_prev):
    m = p.shape[0]
    with_state = h0 is not None
    sb = row_off // n_tok
    assert row_off % n_tok == 0

    def seq_spec(col0):
        return pl.BlockSpec((n_tok, LRU_BW), lambda b, n: (sb + b, col0 // LRU_BW + n))

    def par2(shape):
        return pl.BlockSpec((2,) + shape + (LRU_BW,), lambda b, n: (0,) + (0,) * len(shape) + (n,))

    in_specs = [seq_spec(P_LX), seq_spec(P_LG),
                pl.BlockSpec((CONV_W, LRU_BW), lambda b, n: (0, n)),
                pl.BlockSpec((1, LRU_BW), lambda b, n: (0, n)),
                pl.BlockSpec((2, None, LRU_BW, LRU_BW), lambda b, n: (0, n, 0, 0)), par2((1,)),
                pl.BlockSpec((2, None, LRU_BW, LRU_BW), lambda b, n: (0, n, 0, 0)), par2((1,)), par2((1,))]
    args = [p, p, lp['lru_conv_w'], lp['lru_conv_b'].reshape(1, LRU_W),
            lp['lru_wa'].astype(MXU_DTYPE), lp['lru_ba'].reshape(2, 1, LRU_W),
            lp['lru_wx'].astype(MXU_DTYPE), lp['lru_bx'].reshape(2, 1, LRU_W), lp['lru_lam'].reshape(2, 1, LRU_W)]
    out_specs = [pl.BlockSpec((n_tok, LRU_BW), lambda b, n: (sb + b, n))]
    out_shape = [jax.ShapeDtypeStruct((m, GROUP_W), MXU_DTYPE)]
    if with_state:
        in_specs.append(pl.BlockSpec((None, 2, 1, LRU_BW), lambda b, n: (b, 0, 0, n)))
        args.append(h0.reshape(bsz, 2, 1, LRU_W))
    else:
        out_specs.append(pl.BlockSpec((None, 2, 1, LRU_BW), lambda b, n: (b, 0, 0, n)))
        out_shape.append(jax.ShapeDtypeStruct((bsz, 2, 1, LRU_W), jnp.float32))
    n_in = len(args)
    aliases = {}
    if out_prev is not None:
        in_specs.append(pl.BlockSpec(memory_space=pl.ANY))
        args.append(out_prev)
        aliases = {n_in: 0}

    def body(*refs):
        if out_prev is not None:
            refs = refs[:n_in] + refs[n_in + 1:]
        _lru_kernel(*refs, n_tok=n_tok, with_state=with_state)

    return pl.pallas_call(
        body,
        grid=(bsz, LRU_BLOCKS),
        in_specs=in_specs,
        out_specs=out_specs,
        out_shape=out_shape,
        input_output_aliases=aliases,
        compiler_params=_params(2, 48 << 20),
        name="rg_lru",
    )(*args)


SSD_L = 128
SSD_GH = SSD_HEADS // SSD_GROUPS
SSD_GW = SSD_GH * SSD_HEADDIM


def _ssd_kernel(*refs, n_tok, with_state):
    if with_state:
        (x_ref, b_ref, c_ref, s_ref, cwx_ref, cbx_ref, cwb_ref, cbb_ref, cwc_ref, cbc_ref, dtb_ref, alog_ref,
         dskip_ref, h0_ref, y_ref, xc_s, xdt_s, bc_s, cc_s, la_s, h_s) = refs
    else:
        (x_ref, b_ref, c_ref, s_ref, cwx_ref, cbx_ref, cwb_ref, cbb_ref, cwc_ref, cbc_ref, dtb_ref, alog_ref,
         dskip_ref, y_ref, hT_ref, xc_s, xdt_s, bc_s, cc_s, la_s, h_s) = refs
    L = SSD_L
    n_chunk = n_tok // L
    g = pl.program_id(1)

    xc_s[...] = _silu(_conv4(x_ref[...], cwx_ref, cbx_ref))
    bc_s[...] = _silu(_conv4(b_ref[...], cwb_ref, cbb_ref)).astype(MXU_DTYPE)
    cc_s[...] = _silu(_conv4(c_ref[...], cwc_ref, cbc_ref)).astype(MXU_DTYPE)
    dt = _softplus(s_ref[...] + dtb_ref[...])
    y_ref[...] = xc_s[...] * dskip_ref[...]

    e8 = (_row_iota((S_W, SSD_GW)) == _lane_iota((S_W, SSD_GW)) // SSD_HEADDIM).astype(MXU_DTYPE)
    ri, ci = _row_iota((L, L)), _lane_iota((L, L))
    lane_w = _lane_iota((L, 2 * SSD_HEADDIM))

    for d in range(2):
        up = d == 1
        sel = (_row_iota((S_W, S_W)) == _lane_iota((S_W, S_W)) + (d * SSD_HEADS + g * SSD_GH)).astype(MXU_DTYPE)
        sel = jnp.where(_lane_iota((S_W, S_W)) < SSD_GH, sel, jnp.zeros_like(sel))
        dt_sel = _dot01_right(dt, sel)
        a_row = -jnp.exp(_dot01_right(alog_ref[...], sel))
        la_s[...] = dt_sel * a_row
        xdt_s[...] = xc_s[...] * _dot01_right(dt_sel, e8)
        if with_state:
            h_s[...] = h0_ref[d].reshape(SSD_GW, SSD_STATE)
        else:
            h_s[...] = jnp.zeros_like(h_s)
        tri = ((ci >= ri) if up else (ci <= ri))
        tri_f = tri.astype(MXU_DTYPE)
        end = 0 if up else L - 1

        def chunk(i, carry):
            c = (n_chunk - 1 - i) if up else i
            t0 = pl.multiple_of(c * L, L)
            rows = pl.ds(t0, L)
            cum = _dot01_left(tri_f, la_s[rows, :])
            cum_t = cum.T
            cum_i = _dot01_right(cum, e8)
            cum_end = cum_i[end:end + 1, :]
            xdt = xdt_s[rows, :]
            bcv, ccv = bc_s[rows, :], cc_s[rows, :]
            cb = _dot_nt(ccv, bcv)
            h_b16 = h_s[...].astype(MXU_DTYPE)
            y = _dot_nt(ccv, h_b16) * jnp.exp(cum_i)
            xdt_b16 = xdt.astype(MXU_DTYPE)
            pieces = []
            for pair in range(SSD_GH // 2):
                xp = xdt_b16[:, pair * 2 * SSD_HEADDIM:(pair + 1) * 2 * SSD_HEADDIM]
                acc = None
                for sub in range(2):
                    hh = 2 * pair + sub
                    seg = jnp.where(tri, jnp.exp(cum[:, hh:hh + 1] - cum_t[hh:hh + 1, :]), 0.0)
                    sc = (cb * seg).astype(MXU_DTYPE)
                    half = (lane_w // SSD_HEADDIM) == sub
                    part = _dot(sc, jnp.where(half, xp, jnp.zeros_like(xp)))
                    acc = part if acc is None else acc + part
                pieces.append(acc)
            y = y + jnp.concatenate(pieces, axis=1)
            y_ref[rows, :] += y
            upd = _dot_tn((xdt * jnp.exp(cum_end - cum_i)).astype(MXU_DTYPE), bcv)
            for hh in range(SSD_GH):
                blk = slice(hh * SSD_HEADDIM, (hh + 1) * SSD_HEADDIM)
                h_s[blk, :] = h_s[blk, :] * jnp.exp(cum_t[hh:hh + 1, end:end + 1]) + upd[blk, :]
            return carry

        lax.fori_loop(0, n_chunk, chunk, 0)
        if not with_state:
            hT_ref[d] = h_s[...].reshape(SSD_GH, SSD_HEADDIM, SSD_STATE)


def _ssd_call(p, s, row_off, bsz, n_tok, lp, h0):
    with_state = h0 is not None
    sb = row_off // n_tok
    assert row_off % n_tok == 0 and n_tok % SSD_L == 0

    def seq_spec(col0, w):
        return pl.BlockSpec((n_tok, w), lambda b, g: (sb + b, col0 // w + g))

    def conv_specs(col0, w):
        return [pl.BlockSpec((CONV_W, w), lambda b, g: (0, col0 // w + g)),
                pl.BlockSpec((1, w), lambda b, g: (0, col0 // w + g))]

    small = pl.BlockSpec((1, S_W), lambda b, g: (0, 0))
    pad = jnp.zeros((S_W - 2 * SSD_HEADS,), jnp.float32)
    dtb = jnp.concatenate([lp['ssd_dt_bias'].reshape(-1), pad]).reshape(1, S_W)
    alog = jnp.concatenate([lp['ssd_a_log'].reshape(-1), pad]).reshape(1, S_W)
    dskip = jnp.repeat(lp['ssd_d'][0] + lp['ssd_d'][1], SSD_HEADDIM).reshape(1, GROUP_W)
    cw, cb = lp['ssd_conv_w'], lp['ssd_conv_b'].reshape(1, -1)
    in_specs = ([seq_spec(P_SX, SSD_GW), seq_spec(P_SB, SSD_STATE), seq_spec(P_SC, SSD_STATE),
                 pl.BlockSpec((n_tok, S_W), lambda b, g: (sb + b, 0))]
                + conv_specs(0, SSD_GW) + conv_specs(GROUP_W, SSD_STATE)
                + conv_specs(GROUP_W + SSD_GROUPS * SSD_STATE, SSD_STATE)
                + [small, small, pl.BlockSpec((1, SSD_GW), lambda b, g: (0, g))])
    args = [p, p, p, s, cw, cb, cw, cb, cw, cb, dtb, alog, dskip]
    state_block = pl.BlockSpec((None, 2, SSD_GH, SSD_HEADDIM, SSD_STATE), lambda b, g: (b, 0, g, 0, 0))
    out_specs = [pl.BlockSpec((n_tok, SSD_GW), lambda b, g: (b, g))]
    out_shape = [jax.ShapeDtypeStruct((bsz * n_tok, GROUP_W), jnp.float32)]
    if with_state:
        in_specs.append(state_block)
        args.append(h0)
    else:
        out_specs.append(state_block)
        out_shape.append(jax.ShapeDtypeStruct((bsz, 2, SSD_HEADS, SSD_HEADDIM, SSD_STATE), jnp.float32))
    return pl.pallas_call(
        functools.partial(_ssd_kernel, n_tok=n_tok, with_state=with_state),
        grid=(bsz, SSD_GROUPS),
        in_specs=in_specs,
        out_specs=out_specs,
        out_shape=out_shape,
        scratch_shapes=[pltpu.VMEM((n_tok, SSD_GW), jnp.float32), pltpu.VMEM((n_tok, SSD_GW), jnp.float32),
                        pltpu.VMEM((n_tok, SSD_STATE), MXU_DTYPE), pltpu.VMEM((n_tok, SSD_STATE), MXU_DTYPE),
                        pltpu.VMEM((n_tok, S_W), jnp.float32),
                        pltpu.VMEM((SSD_GW, SSD_STATE), jnp.float32)],
        compiler_params=_params(2, VMEM_LIMIT_V7X),
        name="ssd_scan",
    )(*args)


def _gated_norm_kernel(y_ref, z_ref, w_ref, o_ref):
    o_ref[...] = _rms_rows(y_ref[...] * _silu(z_ref[...]), w_ref[...]).astype(o_ref.dtype)


def _ssd_finish(y, p, row_off, norm_w, out_prev):
    tm = 256
    m, n = p.shape[0], y.shape[0]
    rb = row_off // tm
    in_specs = [pl.BlockSpec((tm, GROUP_W), lambda i: (i, 0)),
                pl.BlockSpec((tm, GROUP_W), lambda i: (rb + i, P_SZ // GROUP_W)),
                pl.BlockSpec((1, GROUP_W), lambda i: (0, 0))]
    args = [y, p, norm_w.reshape(1, GROUP_W)]
    aliases = {}
    if out_prev is not None:
        in_specs.append(pl.BlockSpec(memory_space=pl.ANY))
        args.append(out_prev)
        aliases = {3: 0}

    def body(y_ref, z_ref, w_ref, *rest):
        _gated_norm_kernel(y_ref, z_ref, w_ref, rest[-1])

    return pl.pallas_call(
        body,
        grid=(n // tm,),
        in_specs=in_specs,
        out_specs=pl.BlockSpec((tm, GROUP_W), lambda i: (rb + i, 0)),
        out_shape=jax.ShapeDtypeStruct((m, GROUP_W), MXU_DTYPE),
        input_output_aliases=aliases,
        compiler_params=_params(1, 32 << 20),
        name="ssd_gated_norm",
    )(*args)


GLA_L = 16
GLA_GROUP = 16


def _gla_kernel(*refs, n_tok, with_state):
    if with_state:
        (q_ref, k_ref, v_ref, gg_ref, s_ref, gw_ref, gb_ref, nw_ref, s0_ref, o_ref,
         dec_s, qe_s, ke_s, att_s, o_s, st_s) = refs
    else:
        (q_ref, k_ref, v_ref, gg_ref, s_ref, gw_ref, gb_ref, nw_ref, o_ref, sT_ref,
         dec_s, qe_s, ke_s, att_s, o_s, st_s) = refs
    L = GLA_L
    n_blk = n_tok // L
    blk3 = (n_blk, L, GLA_DK)
    q3 = (q_ref[...] * (GLA_DK ** -0.5)).reshape(blk3)
    k3 = k_ref[...].reshape(blk3)
    sb16 = s_ref[...].astype(MXU_DTYPE)
    t = _row_iota((n_tok, GLA_DK))
    ri3 = lax.broadcasted_iota(jnp.int32, blk3, 1)
    lane3 = lax.broadcasted_iota(jnp.int32, blk3, 2)

    for d in range(2):
        up = d == 1
        end = 0 if up else L - 1
        gate = _dot(sb16, gw_ref[d]) + gb_ref[d]
        b = -_softplus(-gate) / GLA_NORMALIZER
        s = 1
        while s < L:
            ok = (t % L < L - s) if up else (t % L >= s)
            b = b + jnp.where(ok, pltpu.roll(b, (n_tok - s) if up else s, 0), 0.0)
            s *= 2
        b3 = b.reshape(blk3)
        b_end = b3[:, end:end + 1, :]
        dec_s[...] = jnp.exp(jnp.broadcast_to(b_end, blk3)).reshape(n_tok, GLA_DK)
        qe_s[...] = (q3 * jnp.exp(b3)).reshape(n_tok, GLA_DK).astype(MXU_DTYPE)
        ke_s[...] = (k3 * jnp.exp(b_end - b3)).reshape(n_tok, GLA_DK).astype(MXU_DTYPE)
        att = jnp.zeros(blk3, jnp.float32)
        for j in range(L):
            reach = (ri3 <= j) if up else (ri3 >= j)
            w = jnp.where(reach, jnp.exp(b3 - b3[:, j:j + 1, :]), 0.0) * q3 * k3[:, j:j + 1, :]
            att = jnp.where(lane3 == j, jnp.sum(w, axis=-1, keepdims=True), att)
        att_s[...] = att.reshape(n_tok, GLA_DK).astype(MXU_DTYPE)
        if with_state:
            st_s[...] = s0_ref[d].T
        else:
            st_s[...] = jnp.zeros_like(st_s)

        def group(i, carry):
            c = (n_grp - 1 - i) if up else i
            base = pl.multiple_of(c * (GLA_GROUP * L), GLA_GROUP * L)
            vg = v_ref[pl.ds(base, GLA_GROUP * L), :].astype(MXU_DTYPE)
            keg = ke_s[pl.ds(base, GLA_GROUP * L), :]
            attg = att_s[pl.ds(base, GLA_GROUP * L), 0:L]
            qeg = qe_s[pl.ds(base, GLA_GROUP * L), :]
            decg = dec_s[pl.ds(base, GLA_GROUP * L), :]
            order = range(GLA_GROUP - 1, -1, -1) if up else range(GLA_GROUP)
            upd = {k: _dot_tn(vg[k * L:(k + 1) * L], keg[k * L:(k + 1) * L]) for k in order}
            intra = {k: _dot(attg[k * L:(k + 1) * L], vg[k * L:(k + 1) * L]) for k in order}
            st = st_s[...]
            outs = {}
            for k in order:
                outs[k] = _dot_nt(qeg[k * L:(k + 1) * L], st.astype(MXU_DTYPE)) + intra[k]
                st = st * decg[k * L:k * L + 1, :] + upd[k]
            st_s[...] = st
            o = jnp.concatenate([outs[k] for k in range(GLA_GROUP)], axis=0)
            if up:
                o_s[pl.ds(base, GLA_GROUP * L), :] += o
            else:
                o_s[pl.ds(base, GLA_GROUP * L), :] = o
            return carry

        n_grp = n_blk // GLA_GROUP
        lax.fori_loop(0, n_grp, group, 0)
        if not with_state:
            sT_ref[d] = st_s[...].T
    o = _rms_rows(o_s[...], nw_ref[...]) * _silu(gg_ref[...])
    o_ref[...] = o.astype(o_ref.dtype)


def _gla_call(p, s, row_off, bsz, n_tok, lp, s0, out_prev):
    m = p.shape[0]
    with_state = s0 is not None
    sb = row_off // n_tok
    assert row_off % n_tok == 0 and n_tok % GLA_L == 0

    def seq_spec(col0, w):
        return pl.BlockSpec((n_tok, w), lambda b, h: (sb + b, col0 // w + h))

    gw = lp['gla_gate_w'].reshape(2, GLA_RANK, GLA_HEADS, GLA_DK).transpose(0, 2, 1, 3)
    gw_rows = jnp.zeros((2, GLA_HEADS, S_W, GLA_DK), jnp.float32)
    for d in range(2):
        r0 = 2 * SSD_HEADS + d * GLA_RANK
        gw_rows = gw_rows.at[d, :, r0:r0 + GLA_RANK, :].set(gw[d])
    in_specs = [seq_spec(P_GQ, GLA_DK), seq_spec(P_GK, GLA_DK), seq_spec(P_GV, GLA_DV), seq_spec(P_GG, GLA_DV),
                pl.BlockSpec((n_tok, S_W), lambda b, h: (sb + b, 0)),
                pl.BlockSpec((2, None, S_W, GLA_DK), lambda b, h: (0, h, 0, 0)),
                pl.BlockSpec((2, None, 1, GLA_DK), lambda b, h: (0, h, 0, 0)),
                pl.BlockSpec((1, GLA_DV), lambda b, h: (0, 0))]
    args = [p, p, p, p, s, gw_rows.astype(MXU_DTYPE), lp['gla_gate_b'].reshape(2, GLA_HEADS, 1, GLA_DK),
            lp['gla_norm_w'].reshape(1, GLA_DV)]
    state_block = pl.BlockSpec((None, 2, None, GLA_DK, GLA_DV), lambda b, h: (b, 0, h, 0, 0))
    out_specs = [pl.BlockSpec((n_tok, GLA_DV), lambda b, h: (sb + b, h))]
    out_shape = [jax.ShapeDtypeStruct((m, GROUP_W), MXU_DTYPE)]
    if with_state:
        in_specs.append(state_block)
        args.append(s0)
    else:
        out_specs.append(state_block)
        out_shape.append(jax.ShapeDtypeStruct((bsz, 2, GLA_HEADS, GLA_DK, GLA_DV), jnp.float32))
    n_in = len(args)
    aliases = {}
    if out_prev is not None:
        in_specs.append(pl.BlockSpec(memory_space=pl.ANY))
        args.append(out_prev)
        aliases = {n_in: 0}

    def body(*refs):
        if out_prev is not None:
            refs = refs[:n_in] + refs[n_in + 1:]
        _gla_kernel(*refs, n_tok=n_tok, with_state=with_state)

    return pl.pallas_call(
        body,
        grid=(bsz, GLA_HEADS),
        in_specs=in_specs,
        out_specs=out_specs,
        out_shape=out_shape,
        scratch_shapes=[pltpu.VMEM((n_tok, GLA_DK), jnp.float32), pltpu.VMEM((n_tok, GLA_DK), MXU_DTYPE),
                        pltpu.VMEM((n_tok, GLA_DK), MXU_DTYPE), pltpu.VMEM((n_tok, GLA_DK), MXU_DTYPE),
                        pltpu.VMEM((n_tok, GLA_DV), jnp.float32), pltpu.VMEM((GLA_DV, GLA_DK), jnp.float32)],
        input_output_aliases=aliases,
        compiler_params=_params(2, 48 << 20),
        name="gla_scan",
    )(*args)


def _mixers(p, s, lp, cached):
    ctx_k, ctx_v, ssd0, gla0, lru0 = cached
    kv_w = ATT_KV_HEADS * HEAD_DIM
    att, k_new = _attention_call(p, 0, BATCH, SEQ, lp['q_norm'], lp['k_norm'], None, None)
    att, = _attention_call(p, M_PROMPT, DEC_BATCH, DEC_SEQ, lp['q_norm'], lp['k_norm'],
                           (ctx_k.reshape(DEC_BATCH, PAST_LEN, kv_w), ctx_v.reshape(DEC_BATCH, PAST_LEN, kv_w)), att)
    v_new = p[:M_PROMPT, P_AV:P_AV + kv_w]

    y_p, ssd_new = _ssd_call(p, s, 0, BATCH, SEQ, lp, None)
    y_s, = _ssd_call(p, s, M_PROMPT, DEC_BATCH, DEC_SEQ, lp, ssd0)
    ssd = _ssd_finish(y_p, p, 0, lp['ssd_norm_w'], None)
    ssd = _ssd_finish(y_s, p, M_PROMPT, lp['ssd_norm_w'], ssd)

    gla, gla_new = _gla_call(p, s, 0, BATCH, SEQ, lp, None, None)
    gla, = _gla_call(p, s, M_PROMPT, DEC_BATCH, DEC_SEQ, lp, gla0, gla)

    lru, lru_new = _lru_call(p, 0, BATCH, SEQ, lp, None, None)
    lru, = _lru_call(p, M_PROMPT, DEC_BATCH, DEC_SEQ, lp, lru0, lru)

    new_ctx = (k_new.reshape(BATCH, SEQ, ATT_KV_HEADS, HEAD_DIM), v_new.reshape(BATCH, SEQ, ATT_KV_HEADS, HEAD_DIM),
               ssd_new, gla_new, lru_new.reshape(BATCH, 2, LRU_W))
    return [att, ssd, gla, lru], new_ctx


IN_W = 9280
_W_IN_MAIN = ((0, P_AQ, 1024), (1024, P_AK, 256), (1280, P_AV, 256), (1536, P_SX, 1024), (2560, P_SZ, 1024),
              (3584, P_SB, 256), (3840, P_SC, 256), (4128, P_GQ, 512), (4640, P_GK, 512), (5152, P_GV, 1024),
              (6208, P_GG, 1024), (7232, P_LX, 1024), (8256, P_LG, 1024))
_W_IN_SMALL = ((4096, 0, 2 * SSD_HEADS), (6176, 2 * SSD_HEADS, 2 * GLA_RANK))


def _w_in_kernel(w_ref, main_ref, small_ref):
    for src, dst, width in _W_IN_MAIN:
        main_ref[:, dst:dst + width] = w_ref[:, src:src + width].astype(main_ref.dtype)
    small_ref[...] = jnp.zeros_like(small_ref)
    for src, dst, width in _W_IN_SMALL:
        small_ref[:, dst:dst + width] = w_ref[:, src:src + width].astype(small_ref.dtype)


def _reorder_w_in(w, l):
    tk = 256
    return pl.pallas_call(
        _w_in_kernel,
        grid=(D_MODEL // tk,),
        in_specs=[pl.BlockSpec((None, tk, IN_W), lambda i: (l, i, 0))],
        out_specs=[pl.BlockSpec((tk, P_W), lambda i: (i, 0)), pl.BlockSpec((tk, S_W), lambda i: (i, 0))],
        out_shape=[jax.ShapeDtypeStruct((D_MODEL, P_W), MXU_DTYPE), jax.ShapeDtypeStruct((D_MODEL, S_W), MXU_DTYPE)],
        compiler_params=_params(1, 48 << 20),
        name="w_in_regroup",
    )(w)


def kernel(x_prompt, x_sample, c, cache_attn_k, cache_attn_v, state_ssd, state_gla, state_lru, c_ctx,
           mod_w, mod_b, ln_g, ln_b, ffn_w_gate, ffn_w_up, ffn_w_down, w_in, w_out, q_norm, k_norm,
           ssd_conv_w, ssd_conv_b, ssd_a_log, ssd_dt_bias, ssd_d, ssd_norm_w,
           gla_gate_w, gla_gate_b, gla_norm_w,
           lru_conv_w, lru_conv_b, lru_wa, lru_ba, lru_wx, lru_bx, lru_lam):
    cond = jnp.concatenate([c_ctx[None], c, jnp.zeros((COND_ROWS - N_COND, D_MODEL), jnp.float32)], axis=0)
    mod_all = _mod_table(cond, mod_w, mod_b)

    def ffn(xm, l, half):
        h = _dense_cast_first([xm], [ffn_w_gate, ffn_w_up], (l, half), tm_first=2048, tn_first=256, tm=2048, tn=256,
                              out_dtype=MXU_DTYPE, name="ffn_up", swiglu=True)
        return _dense_cast_first([h], [ffn_w_down], (l, half), tm_first=512, tn_first=256, tm=512, tn=512,
                                 out_dtype=jnp.float32, name="ffn_down")

    x, xm = _modulate(x_prompt, x_sample, mod_all[0], 1, 0)
    ctx_out = []
    for l in range(DEPTH):
        mod = mod_all[l]
        lp = dict(q_norm=q_norm[l], k_norm=k_norm[l], ssd_conv_w=ssd_conv_w[l], ssd_conv_b=ssd_conv_b[l],
                  ssd_a_log=ssd_a_log[l], ssd_dt_bias=ssd_dt_bias[l], ssd_d=ssd_d[l], ssd_norm_w=ssd_norm_w[l],
                  gla_gate_w=gla_gate_w[l], gla_gate_b=gla_gate_b[l], gla_norm_w=gla_norm_w[l],
                  lru_conv_w=lru_conv_w[l], lru_conv_b=lru_conv_b[l], lru_wa=lru_wa[l], lru_ba=lru_ba[l],
                  lru_wx=lru_wx[l], lru_bx=lru_bx[l], lru_lam=lru_lam[l])
        x, xm = _resid_ln(x, ffn(xm, l, 0), mod, 2, 0.5, ln_g[l, 0], ln_b[l, 0], mod, 4, 3)
        w_main, w_small = _reorder_w_in(w_in, l)
        dense = dict(tile0=0, n_tiles=M_TOK // 1024, tm=1024, out_dtype=jnp.float32)
        p = _dense([xm], [w_main], (), tn=1024, name="proj_in", **dense)
        s = _dense([xm], [w_small], (), tn=S_W, name="proj_in_small", **dense)
        cached = (cache_attn_k[:, l], cache_attn_v[:, l], state_ssd[:, l], state_gla[:, l], state_lru[:, l])
        mix, new_ctx = _mixers(p, s, lp, cached)
        ctx_out.append(new_ctx)
        m_out = _dense_cast_first(mix, [w_out], (l,), tm_first=1024, tn_first=512, tm=1024, tn=1024,
                                  out_dtype=jnp.float32, name="proj_out")
        x, xm = _resid_ln(x, m_out, mod, 5, 1.0, ln_g[l, 1], ln_b[l, 1], mod, 7, 6)
        f = ffn(xm, l, 1)
        if l + 1 < DEPTH:
            x, xm = _resid_ln(x, f, mod, 8, 0.5, ln_g[l, 2], ln_b[l, 2], mod_all[l + 1], 1, 0)
        else:
            y_prompt, y_sample = _resid_ln(x, f, mod, 8, 0.5, ln_g[l, 2], ln_b[l, 2])

    y_prompt = y_prompt.reshape(BATCH, SEQ, D_MODEL)
    y_sample = y_sample.reshape(DEC_BATCH, DEC_SEQ, D_MODEL)
    new_k, new_v, new_ssd, new_gla, new_lru = (jnp.stack([s_[i] for s_ in ctx_out], axis=1) for i in range(5))
    return (y_prompt, y_sample, new_k, new_v, new_ssd, new_gla, new_lru)
```

```python
import functools
import math

import numpy as np
import jax
import jax.numpy as jnp
from jax import lax
from jax.experimental import pallas as pl
from jax.experimental.pallas import tpu as pltpu

D_MODEL = 4096
BATCH = 16
SEQ = 256
DEPTH = 2
DEC_BATCH = 4
DEC_SEQ = 2048
PAST_LEN = 256
GRID_W = 64
GROUP_W = 1024
HEAD_DIM = 128
ATT_HEADS = 8
ATT_KV_HEADS = 2
ROPE_THETA = 10000.0
SSD_HEADDIM = 64
SSD_HEADS = 16
SSD_STATE = 128
SSD_GROUPS = 2
CONV_W = 4
GLA_HEADS = 4
GLA_DK = 128
GLA_DV = 256
GLA_RANK = 16
GLA_NORMALIZER = 16.0
LRU_W = 1024
LRU_BLOCKS = 8
LRU_BW = 128
LRU_C = 8.0
D_FF = 11008
N_MOD = 9
LN_EPS = 1e-5
RMS_EPS = 1e-6
ALPHA = (2.0 * DEPTH) ** 0.25

M_PROMPT = BATCH * SEQ
M_SAMPLE = DEC_BATCH * DEC_SEQ
M_TOK = M_PROMPT + M_SAMPLE
N_COND = 1 + DEC_BATCH
COND_ROWS = 8

VMEM_LIMIT_V7X = 56 * 1024 * 1024

MXU_DTYPE = jnp.bfloat16

P_AQ, P_SX, P_SZ, P_GV, P_GG, P_LX, P_LG = 0, 1024, 2048, 3072, 4096, 5120, 6144
P_GQ, P_GK = 7168, 7680
P_AK, P_AV, P_SB, P_SC = 8192, 8448, 8704, 8960
P_W = 9216
S_W = 128


def _params(n_axes, vmem_bytes):
    return pltpu.CompilerParams(dimension_semantics=("arbitrary",) * n_axes,
                                vmem_limit_bytes=min(int(vmem_bytes), VMEM_LIMIT_V7X))


def _cond_row(tile_idx, tile_rows):
    row0 = tile_idx * tile_rows
    return jnp.where(row0 < M_PROMPT, 0, 1 + (row0 - M_PROMPT) // DEC_SEQ)


def _mod_kernel(c_ref, w_ref, b_ref, o_ref):
    c = c_ref[...]
    a = (c * jax.nn.sigmoid(c)).astype(MXU_DTYPE)
    w = w_ref[...].astype(MXU_DTYPE)
    o_ref[...] = jnp.dot(a, w, preferred_element_type=jnp.float32) + b_ref[...]


def _mod_table(cond, mod_w, mod_b):
    tn = 512
    n = N_MOD * D_MODEL
    return pl.pallas_call(
        _mod_kernel,
        grid=(DEPTH, n // tn),
        in_specs=[pl.BlockSpec((COND_ROWS, D_MODEL), lambda l, j: (0, 0)),
                  pl.BlockSpec((None, D_MODEL, tn), lambda l, j: (l, 0, j)),
                  pl.BlockSpec((None, 1, tn), lambda l, j: (l, 0, j))],
        out_specs=pl.BlockSpec((None, COND_ROWS, tn), lambda l, j: (l, 0, j)),
        out_shape=jax.ShapeDtypeStruct((DEPTH, COND_ROWS, n), jnp.float32),
        compiler_params=_params(2, 3 * D_MODEL * tn * 4 + (4 << 20)),
        name="mod_table",
    )(cond, mod_w, mod_b.reshape(DEPTH, 1, n))


def _dense_kernel(*refs, n_a, n_w, cast, swiglu):
    a_refs, w_refs, o_ref = refs[:n_a], refs[n_a:n_a + n_w], refs[n_a + n_w]
    cast_refs = refs[n_a + n_w + 1:n_a + 2 * n_w + 1] if cast else ()
    accs = []
    for wi, w_ref in enumerate(w_refs):
        if cast:
            w_narrow = w_ref[...].astype(MXU_DTYPE)
            cast_refs[wi][...] = w_narrow
        k0, acc = 0, None
        for a_ref in a_refs:
            kp = a_ref.shape[1]
            w_rows = w_narrow[k0:k0 + kp, :] if cast else w_ref[k0:k0 + kp, :]
            part = jnp.dot(a_ref[...], w_rows, preferred_element_type=jnp.float32)
            acc = part if acc is None else acc + part
            k0 += kp
        accs.append(acc)
    out = accs[0] * jax.nn.sigmoid(accs[0]) * accs[1] if swiglu else accs[0]
    o_ref[...] = out.astype(o_ref.dtype)


def _dense(a_list, w_list, w_index, *, tile0, n_tiles, tm, tn, out_dtype, name, swiglu=False, cast=False,
           out_prev=None):
    m = a_list[0].shape[0]
    k, n = w_list[0].shape[-2:]
    n_a, n_w, lead = len(a_list), len(w_list), len(w_index)
    assert sum(a.shape[1] for a in a_list) == k and n % tn == 0 and (tile0 + n_tiles) * tm <= m
    a_mode = dict(pipeline_mode=pl.Buffered(1)) if n_tiles == 1 else {}
    in_specs = [pl.BlockSpec((tm, a.shape[1]), lambda i, j: (tile0 + i, 0), **a_mode) for a in a_list]
    in_specs += [pl.BlockSpec((None,) * lead + (k, tn), lambda i, j: tuple(w_index) + (0, j)) for _ in w_list]
    args = list(a_list) + list(w_list)
    out_specs = [pl.BlockSpec((tm, tn), lambda i, j: (tile0 + i, j))]
    out_shape = [jax.ShapeDtypeStruct((m, n), out_dtype)]
    if cast:
        out_specs += [pl.BlockSpec((k, tn), lambda i, j: (0, j)) for _ in w_list]
        out_shape += [jax.ShapeDtypeStruct((k, n), MXU_DTYPE) for _ in w_list]
    aliases = {}
    if out_prev is not None:
        in_specs.append(pl.BlockSpec(memory_space=pl.ANY))
        args.append(out_prev)
        aliases = {n_a + n_w: 0}
    esz, wsz, osz = jnp.dtype(MXU_DTYPE).itemsize, w_list[0].dtype.itemsize, jnp.dtype(out_dtype).itemsize
    vmem = ((1 if n_tiles == 1 else 2) * tm * k * esz + 2 * n_w * k * tn * wsz + 2 * tm * tn * osz
            + (n_w + 1) * tm * tn * 4 + (3 * n_w * k * tn * esz if cast else 0))

    def body(*refs):
        if out_prev is not None:
            refs = refs[:n_a + n_w] + refs[n_a + n_w + 1:]
        _dense_kernel(*refs, n_a=n_a, n_w=n_w, cast=cast, swiglu=swiglu)

    out = pl.pallas_call(
        body,
        grid=(n_tiles, n // tn),
        in_specs=in_specs,
        out_specs=out_specs,
        out_shape=out_shape,
        input_output_aliases=aliases,
        compiler_params=_params(2, vmem + (4 << 20)),
        name=name,
    )(*args)
    return out if cast else out[0]


def _dense_cast_first(a_list, w_list, w_index, *, tm_first, tn_first, tm, tn, out_dtype, name, swiglu=False):
    m = a_list[0].shape[0]
    first = _dense(a_list, w_list, w_index, tile0=0, n_tiles=1, tm=tm_first, tn=tn_first, out_dtype=out_dtype,
                   name=name + "_first", swiglu=swiglu, cast=True)
    assert tm_first % tm == 0 or tm_first == tm
    return _dense(a_list, first[1:], (), tile0=tm_first // tm, n_tiles=(m - tm_first) // tm, tm=tm, tn=tn,
                  out_dtype=out_dtype, name=name, swiglu=swiglu, out_prev=first[0])


def _modulate_kernel(xp_ref, xs_ref, sc_ref, sh_ref, x_ref, o_ref, *, tm):
    i = pl.program_id(0)
    r = _cond_row(i, tm)
    sc = sc_ref[pl.ds(r, 1), :]
    sh = sh_ref[pl.ds(r, 1), :]

    def emit(x):
        x_ref[...] = x
        o_ref[...] = (x * (1.0 + sc) + sh).astype(o_ref.dtype)

    pl.when(i < M_PROMPT // tm)(lambda: emit(xp_ref[...]))
    pl.when(i >= M_PROMPT // tm)(lambda: emit(xs_ref[...]))


def _mod_spec(k):
    return pl.BlockSpec((COND_ROWS, D_MODEL), lambda i: (0, k))


def _modulate(x_prompt, x_sample, mod, k_scale, k_shift):
    tm = 256
    n_p = M_PROMPT // tm
    row = pl.BlockSpec((tm, D_MODEL), lambda i: (i, 0))
    return pl.pallas_call(
        functools.partial(_modulate_kernel, tm=tm),
        grid=(M_TOK // tm,),
        in_specs=[pl.BlockSpec((tm, D_MODEL), lambda i: (jnp.minimum(i, n_p - 1), 0)),
                  pl.BlockSpec((tm, D_MODEL), lambda i: (jnp.maximum(i - n_p, 0), 0)),
                  _mod_spec(k_scale), _mod_spec(k_shift)],
        out_specs=[row, row],
        out_shape=[jax.ShapeDtypeStruct((M_TOK, D_MODEL), jnp.float32),
                   jax.ShapeDtypeStruct((M_TOK, D_MODEL), MXU_DTYPE)],
        compiler_params=_params(1, 32 << 20),
        name="modulate",
    )(x_prompt.reshape(M_PROMPT, D_MODEL), x_sample.reshape(M_SAMPLE, D_MODEL), mod, mod)


def _resid_ln_kernel(*refs, tm, gate_scale, with_next):
    if with_next:
        x_ref, f_ref, g_ref, lg_ref, lb_ref, sc_ref, sh_ref, xo_ref, mo_ref = refs
    else:
        x_ref, f_ref, g_ref, lg_ref, lb_ref, yp_ref, ys_ref = refs
    i = pl.program_id(0)
    r = _cond_row(i, tm)
    g = g_ref[pl.ds(r, 1), :]
    y = ALPHA * x_ref[...] + (gate_scale * g) * f_ref[...]
    mu = jnp.mean(y, axis=-1, keepdims=True)
    yc = y - mu
    var = jnp.mean(yc * yc, axis=-1, keepdims=True)
    xn = yc * lax.rsqrt(var + LN_EPS) * lg_ref[...] + lb_ref[...]
    if with_next:
        xo_ref[...] = xn
        sc = sc_ref[pl.ds(r, 1), :]
        sh = sh_ref[pl.ds(r, 1), :]
        mo_ref[...] = (xn * (1.0 + sc) + sh).astype(mo_ref.dtype)
    else:
        @pl.when(i < M_PROMPT // tm)
        def _():
            yp_ref[...] = xn

        @pl.when(i >= M_PROMPT // tm)
        def _():
            ys_ref[...] = xn


def _resid_ln(x, f, mod, k_gate, gate_scale, ln_g, ln_b, next_mod=None, k_scale=0, k_shift=0):
    tm = 256
    m = x.shape[0]
    n_p = M_PROMPT // tm
    with_next = next_mod is not None
    row = pl.BlockSpec((tm, D_MODEL), lambda i: (i, 0))
    vec = pl.BlockSpec((1, D_MODEL), lambda i: (0, 0))
    in_specs = [row, row, _mod_spec(k_gate), vec, vec]
    args = [x, f, mod, ln_g.reshape(1, D_MODEL), ln_b.reshape(1, D_MODEL)]
    if with_next:
        in_specs += [_mod_spec(k_scale), _mod_spec(k_shift)]
        args += [next_mod, next_mod]
        out_specs = [row, row]
        out_shape = [jax.ShapeDtypeStruct((m, D_MODEL), jnp.float32), jax.ShapeDtypeStruct((m, D_MODEL), MXU_DTYPE)]
    else:
        out_specs = [pl.BlockSpec((tm, D_MODEL), lambda i: (jnp.minimum(i, n_p - 1), 0)),
                     pl.BlockSpec((tm, D_MODEL), lambda i: (jnp.maximum(i - n_p, 0), 0))]
        out_shape = [jax.ShapeDtypeStruct((M_PROMPT, D_MODEL), jnp.float32),
                     jax.ShapeDtypeStruct((M_SAMPLE, D_MODEL), jnp.float32)]
    out = pl.pallas_call(
        functools.partial(_resid_ln_kernel, tm=tm, gate_scale=gate_scale, with_next=with_next),
        grid=(m // tm,),
        in_specs=in_specs,
        out_specs=out_specs,
        out_shape=out_shape,
        compiler_params=_params(1, 48 << 20),
        name="resid_ln",
    )(*args)
    return out[0], out[1]


def _row_iota(shape):
    return lax.broadcasted_iota(jnp.int32, shape, 0)


def _lane_iota(shape):
    return lax.broadcasted_iota(jnp.int32, shape, 1)


def _shift_rows(x, s, fill, up=False):
    n = x.shape[0]
    t = _row_iota(x.shape)
    if up:
        return jnp.where(t < n - s, pltpu.roll(x, n - s, 0), fill)
    return jnp.where(t >= s, pltpu.roll(x, s, 0), fill)


def _softplus(z):
    return jnp.maximum(z, 0.0) + jnp.log1p(jnp.exp(-jnp.abs(z)))


def _silu(z):
    return z * jax.nn.sigmoid(z)


def _dot(a, b):
    return jnp.dot(a, b, preferred_element_type=jnp.float32)


def _dot_nt(a, b):
    return lax.dot_general(a, b, (((1,), (1,)), ((), ())), preferred_element_type=jnp.float32)


def _dot_tn(a, b):
    return lax.dot_general(a, b, (((0,), (0,)), ((), ())), preferred_element_type=jnp.float32)


def _split3(x):
    hi = x.astype(MXU_DTYPE)
    r1 = x - hi.astype(jnp.float32)
    mid = r1.astype(MXU_DTYPE)
    lo = (r1 - mid.astype(jnp.float32)).astype(MXU_DTYPE)
    return hi, mid, lo


def _dot01_right(x, e):
    hi, mid, lo = _split3(x)
    return _dot(hi, e) + _dot(mid, e) + _dot(lo, e)


def _dot01_left(e, x):
    hi, mid, lo = _split3(x)
    return _dot(e, hi) + _dot(e, mid) + _dot(e, lo)


def _conv4(x, w_ref, b_ref):
    acc = _shift_rows(x, 2, 0.0) * w_ref[0:1, :]
    acc = acc + _shift_rows(x, 1, 0.0) * w_ref[1:2, :]
    acc = acc + x * w_ref[2:3, :]
    acc = acc + _shift_rows(x, 1, 0.0, up=True) * w_ref[3:4, :]
    return acc + b_ref[...]


def _rms_rows(x, w):
    return x * lax.rsqrt(jnp.mean(x * x, axis=-1, keepdims=True) + RMS_EPS) * w


ATT_STACK = 2


def _rope_tables(n_tok):
    rows = n_tok // GRID_W
    row = jnp.repeat(jnp.arange(rows, dtype=jnp.float32), GRID_W)
    col = jnp.tile(jnp.arange(GRID_W, dtype=jnp.float32), rows)
    n_freq = HEAD_DIM // 4
    inv = ROPE_THETA ** (-jnp.arange(n_freq, dtype=jnp.float32) / n_freq)
    ar, ac = row[:, None] * inv, col[:, None] * inv
    cos = jnp.concatenate([jnp.cos(ar), jnp.cos(ar), jnp.cos(ac), jnp.cos(ac)], axis=1)
    sin = jnp.concatenate([-jnp.sin(ar), jnp.sin(ar), -jnp.sin(ac), jnp.sin(ac)], axis=1)
    return cos, sin


def _rope(x, cos, sin):
    quarter = HEAD_DIM // 4
    lane = _lane_iota(x.shape)
    partner = jnp.where(lane % (2 * quarter) < quarter,
                        pltpu.roll(x, HEAD_DIM - quarter, 1), pltpu.roll(x, quarter, 1))
    return x * cos + partner * sin


def _attn_kernel(*refs, n_tok, tq, n_ctx, rope):
    if rope:
        (q_ref, k_ref, v_ref, qn_ref, kn_ref, cq_ref, sq_ref, ck_ref, sk_ref, xk_ref, xv_ref,
         o_ref, ks_ref, vs_ref) = refs
    else:
        q_ref, k_ref, v_ref, qn_ref, kn_ref, o_ref, ko_ref, ks_ref, vs_ref = refs
    rep = ATT_HEADS // ATT_KV_HEADS

    @pl.when(pl.program_id(2) == 0)
    def _():
        kn = _rms_rows(k_ref[...], kn_ref[...])
        if rope:
            kn = _rope(kn, ck_ref[...], sk_ref[...])
            ks_ref[n_tok:n_tok + n_ctx, :] = xk_ref[...].astype(MXU_DTYPE)
            vs_ref[n_tok:n_tok + n_ctx, :] = xv_ref[...].astype(MXU_DTYPE)
        else:
            ko_ref[...] = kn
        ks_ref[0:n_tok, :] = kn.astype(MXU_DTYPE)
        vs_ref[0:n_tok, :] = v_ref[...].astype(MXU_DTYPE)

    heads = []
    for r in range(rep):
        qh = _rms_rows(q_ref[:, r * HEAD_DIM:(r + 1) * HEAD_DIM], qn_ref[...])
        if rope:
            qh = _rope(qh, cq_ref[...], sq_ref[...])
        heads.append(qh.astype(MXU_DTYPE))
    for r0 in range(0, rep, ATT_STACK):
        qs = jnp.concatenate(heads[r0:r0 + ATT_STACK], axis=0)
        s = _dot_nt(qs, ks_ref[...]) * (HEAD_DIM ** -0.5)
        e = jnp.exp(s - jnp.max(s, axis=-1, keepdims=True))
        o = _dot(e.astype(MXU_DTYPE), vs_ref[...]) / jnp.sum(e, axis=-1, keepdims=True)
        for r in range(ATT_STACK):
            o_ref[:, (r0 + r) * HEAD_DIM:(r0 + r + 1) * HEAD_DIM] = o[r * tq:(r + 1) * tq].astype(o_ref.dtype)


def _attention_call(p, row_off, bsz, n_tok, q_norm, k_norm, ctx_kv, out_prev):
    m = p.shape[0]
    rope = ctx_kv is not None
    tq = 128 if rope else n_tok
    nq = n_tok // tq
    rep = ATT_HEADS // ATT_KV_HEADS
    qw = rep * HEAD_DIM
    n_ctx = ctx_kv[0].shape[1] if rope else 0
    assert row_off % n_tok == 0 and n_tok % tq == 0
    rb, sb = row_off // tq, row_off // n_tok

    def seq_spec(col0):
        return pl.BlockSpec((n_tok, HEAD_DIM), lambda b, g, i: (sb + b, col0 // HEAD_DIM + g))

    vec = pl.BlockSpec((1, HEAD_DIM), lambda b, g, i: (0, 0))
    in_specs = [pl.BlockSpec((tq, qw), lambda b, g, i: (rb + b * nq + i, P_AQ // qw + g)),
                seq_spec(P_AK), seq_spec(P_AV), vec, vec]
    args = [p, p, p, q_norm.reshape(1, HEAD_DIM), k_norm.reshape(1, HEAD_DIM)]
    out_block = pl.BlockSpec((tq, qw), lambda b, g, i: (rb + b * nq + i, g))
    out_specs = [out_block]
    out_shape = [jax.ShapeDtypeStruct((m, GROUP_W), MXU_DTYPE)]
    aliases = {}
    if rope:
        cos, sin = _rope_tables(n_tok)
        in_specs += [pl.BlockSpec((tq, HEAD_DIM), lambda b, g, i: (i, 0))] * 2
        in_specs += [pl.BlockSpec((n_tok, HEAD_DIM), lambda b, g, i: (0, 0))] * 2
        in_specs += [pl.BlockSpec((None, n_ctx, HEAD_DIM), lambda b, g, i: (b, 0, g))] * 2
        args += [cos, sin, cos, sin, ctx_kv[0], ctx_kv[1]]
    else:
        out_specs.append(pl.BlockSpec((n_tok, HEAD_DIM), lambda b, g, i: (b, g)))
        out_shape.append(jax.ShapeDtypeStruct((bsz * n_tok, ATT_KV_HEADS * HEAD_DIM), jnp.float32))
    if out_prev is not None:
        in_specs.append(pl.BlockSpec(memory_space=pl.ANY))
        args.append(out_prev)
        aliases = {len(args) - 1: 0}
    n_keys = n_tok + n_ctx

    def body(*refs):
        if out_prev is not None:
            refs = refs[:len(args) - 1] + refs[len(args):]
        _attn_kernel(*refs, n_tok=n_tok, tq=tq, n_ctx=n_ctx, rope=rope)

    out = pl.pallas_call(
        body,
        grid=(bsz, ATT_KV_HEADS, nq),
        in_specs=in_specs,
        out_specs=out_specs,
        out_shape=out_shape,
        scratch_shapes=[pltpu.VMEM((n_keys, HEAD_DIM), MXU_DTYPE), pltpu.VMEM((n_keys, HEAD_DIM), MXU_DTYPE)],
        input_output_aliases=aliases,
        compiler_params=_params(3, 48 << 20),
        name="attention",
    )(*args)
    return out


def _lru_kernel(*refs, n_tok, with_state):
    if with_state:
        (x_ref, g_ref, cw_ref, cb_ref, wa_ref, ba_ref, wx_ref, bx_ref, lam_ref, h0_ref, o_ref) = refs
    else:
        (x_ref, g_ref, cw_ref, cb_ref, wa_ref, ba_ref, wx_ref, bx_ref, lam_ref, o_ref, hT_ref) = refs
    xl = _conv4(x_ref[...], cw_ref, cb_ref)
    xb = xl.astype(MXU_DTYPE)
    t = _row_iota(xl.shape)
    h_sum = None
    for d in range(2):
        up = d == 1
        r = jax.nn.sigmoid(_dot(xb, wa_ref[d]) + ba_ref[d])
        i = jax.nn.sigmoid(_dot(xb, wx_ref[d]) + bx_ref[d])
        log_a = -LRU_C * r * _softplus(-lam_ref[d])
        a = jnp.exp(log_a)
        th = jnp.tanh(log_a)
        u = jnp.sqrt(-2.0 * th / (1.0 - th)) * i * xl
        if with_state:
            first = (t == n_tok - 1) if up else (t == 0)
            u = jnp.where(first, u + a * h0_ref[d], u)
        s = 1
        while s < n_tok:
            u = u + a * _shift_rows(u, s, 0.0, up=up)
            a = a * _shift_rows(a, s, 1.0, up=up)
            s *= 2
        h_sum = u if h_sum is None else h_sum + u
        if not with_state:
            hT_ref[d] = u[0:1, :] if up else u[n_tok - 1:n_tok, :]
    o_ref[...] = (h_sum * jax.nn.gelu(g_ref[...])).astype(o_ref.dtype)


def _lru_call(p, row_off, bsz, n_tok, lp, h0, out_prev):
    m = p.shape[0]
    with_state = h0 is not None
    sb = row_off // n_tok
    assert row_off % n_tok == 0

    def seq_spec(col0):
        return pl.BlockSpec((n_tok, LRU_BW), lambda b, n: (sb + b, col0 // LRU_BW + n))

    def par2(shape):
        return pl.BlockSpec((2,) + shape + (LRU_BW,), lambda b, n: (0,) + (0,) * len(shape) + (n,))

    in_specs = [seq_spec(P_LX), seq_spec(P_LG),
                pl.BlockSpec((CONV_W, LRU_BW), lambda b, n: (0, n)),
                pl.BlockSpec((1, LRU_BW), lambda b, n: (0, n)),
                pl.BlockSpec((2, None, LRU_BW, LRU_BW), lambda b, n: (0, n, 0, 0)), par2((1,)),
                pl.BlockSpec((2, None, LRU_BW, LRU_BW), lambda b, n: (0, n, 0, 0)), par2((1,)), par2((1,))]
    args = [p, p, lp['lru_conv_w'], lp['lru_conv_b'].reshape(1, LRU_W),
            lp['lru_wa'].astype(MXU_DTYPE), lp['lru_ba'].reshape(2, 1, LRU_W),
            lp['lru_wx'].astype(MXU_DTYPE), lp['lru_bx'].reshape(2, 1, LRU_W), lp['lru_lam'].reshape(2, 1, LRU_W)]
    out_specs = [pl.BlockSpec((n_tok, LRU_BW), lambda b, n: (sb + b, n))]
    out_shape = [jax.ShapeDtypeStruct((m, GROUP_W), MXU_DTYPE)]
    if with_state:
        in_specs.append(pl.BlockSpec((None, 2, 1, LRU_BW), lambda b, n: (b, 0, 0, n)))
        args.append(h0.reshape(bsz, 2, 1, LRU_W))
    else:
        out_specs.append(pl.BlockSpec((None, 2, 1, LRU_BW), lambda b, n: (b, 0, 0, n)))
        out_shape.append(jax.ShapeDtypeStruct((bsz, 2, 1, LRU_W), jnp.float32))
    n_in = len(args)
    aliases = {}
    if out_prev is not None:
        in_specs.append(pl.BlockSpec(memory_space=pl.ANY))
        args.append(out_prev)
        aliases = {n_in: 0}

    def body(*refs):
        if out_prev is not None:
            refs = refs[:n_in] + refs[n_in + 1:]
        _lru_kernel(*refs, n_tok=n_tok, with_state=with_state)

    return pl.pallas_call(
        body,
        grid=(bsz, LRU_BLOCKS),
        in_specs=in_specs,
        out_specs=out_specs,
        out_shape=out_shape,
        input_output_aliases=aliases,
        compiler_params=_params(2, 48 << 20),
        name="rg_lru",
    )(*args)


SSD_L = 128
SSD_GH = SSD_HEADS // SSD_GROUPS
SSD_GW = SSD_GH * SSD_HEADDIM


def _ssd_kernel(*refs, n_tok, with_state):
    if with_state:
        (x_ref, b_ref, c_ref, s_ref, cwx_ref, cbx_ref, cwb_ref, cbb_ref, cwc_ref, cbc_ref, dtb_ref, alog_ref,
         dskip_ref, h0_ref, y_ref, xc_s, xdt_s, bc_s, cc_s, la_s, h_s) = refs
    else:
        (x_ref, b_ref, c_ref, s_ref, cwx_ref, cbx_ref, cwb_ref, cbb_ref, cwc_ref, cbc_ref, dtb_ref, alog_ref,
         dskip_ref, y_ref, hT_ref, xc_s, xdt_s, bc_s, cc_s, la_s, h_s) = refs
    L = SSD_L
    n_chunk = n_tok // L
    g = pl.program_id(1)

    xc_s[...] = _silu(_conv4(x_ref[...], cwx_ref, cbx_ref))
    bc_s[...] = _silu(_conv4(b_ref[...], cwb_ref, cbb_ref)).astype(MXU_DTYPE)
    cc_s[...] = _silu(_conv4(c_ref[...], cwc_ref, cbc_ref)).astype(MXU_DTYPE)
    dt = _softplus(s_ref[...] + dtb_ref[...])
    y_ref[...] = xc_s[...] * dskip_ref[...]

    e8 = (_row_iota((S_W, SSD_GW)) == _lane_iota((S_W, SSD_GW)) // SSD_HEADDIM).astype(MXU_DTYPE)
    ri, ci = _row_iota((L, L)), _lane_iota((L, L))
    lane_w = _lane_iota((L, 2 * SSD_HEADDIM))

    for d in range(2):
        up = d == 1
        sel = (_row_iota((S_W, S_W)) == _lane_iota((S_W, S_W)) + (d * SSD_HEADS + g * SSD_GH)).astype(MXU_DTYPE)
        sel = jnp.where(_lane_iota((S_W, S_W)) < SSD_GH, sel, jnp.zeros_like(sel))
        dt_sel = _dot01_right(dt, sel)
        a_row = -jnp.exp(_dot01_right(alog_ref[...], sel))
        la_s[...] = dt_sel * a_row
        xdt_s[...] = xc_s[...] * _dot01_right(dt_sel, e8)
        if with_state:
            h_s[...] = h0_ref[d].reshape(SSD_GW, SSD_STATE)
        else:
            h_s[...] = jnp.zeros_like(h_s)
        tri = ((ci >= ri) if up else (ci <= ri))
        tri_f = tri.astype(MXU_DTYPE)
        end = 0 if up else L - 1

        def chunk(i, carry):
            c = (n_chunk - 1 - i) if up else i
            t0 = pl.multiple_of(c * L, L)
            rows = pl.ds(t0, L)
            cum = _dot01_left(tri_f, la_s[rows, :])
            cum_t = cum.T
            cum_i = _dot01_right(cum, e8)
            cum_end = cum_i[end:end + 1, :]
            xdt = xdt_s[rows, :]
            bcv, ccv = bc_s[rows, :], cc_s[rows, :]
            cb = _dot_nt(ccv, bcv)
            h_b16 = h_s[...].astype(MXU_DTYPE)
            y = _dot_nt(ccv, h_b16) * jnp.exp(cum_i)
            xdt_b16 = xdt.astype(MXU_DTYPE)
            pieces = []
            for pair in range(SSD_GH // 2):
                xp = xdt_b16[:, pair * 2 * SSD_HEADDIM:(pair + 1) * 2 * SSD_HEADDIM]
                acc = None
                for sub in range(2):
                    hh = 2 * pair + sub
                    seg = jnp.where(tri, jnp.exp(cum[:, hh:hh + 1] - cum_t[hh:hh + 1, :]), 0.0)
                    sc = (cb * seg).astype(MXU_DTYPE)
                    half = (lane_w // SSD_HEADDIM) == sub
                    part = _dot(sc, jnp.where(half, xp, jnp.zeros_like(xp)))
                    acc = part if acc is None else acc + part
                pieces.append(acc)
            y = y + jnp.concatenate(pieces, axis=1)
            y_ref[rows, :] += y
            upd = _dot_tn((xdt * jnp.exp(cum_end - cum_i)).astype(MXU_DTYPE), bcv)
            for hh in range(SSD_GH):
                blk = slice(hh * SSD_HEADDIM, (hh + 1) * SSD_HEADDIM)
                h_s[blk, :] = h_s[blk, :] * jnp.exp(cum_t[hh:hh + 1, end:end + 1]) + upd[blk, :]
            return carry

        lax.fori_loop(0, n_chunk, chunk, 0, unroll=2)
        if not with_state:
            hT_ref[d] = h_s[...].reshape(SSD_GH, SSD_HEADDIM, SSD_STATE)


def _ssd_call(p, s, row_off, bsz, n_tok, lp, h0):
    with_state = h0 is not None
    sb = row_off // n_tok
    assert row_off % n_tok == 0 and n_tok % SSD_L == 0

    def seq_spec(col0, w):
        return pl.BlockSpec((n_tok, w), lambda b, g: (sb + b, col0 // w + g))

    def conv_specs(col0, w):
        return [pl.BlockSpec((CONV_W, w), lambda b, g: (0, col0 // w + g)),
                pl.BlockSpec((1, w), lambda b, g: (0, col0 // w + g))]

    small = pl.BlockSpec((1, S_W), lambda b, g: (0, 0))
    pad = jnp.zeros((S_W - 2 * SSD_HEADS,), jnp.float32)
    dtb = jnp.concatenate([lp['ssd_dt_bias'].reshape(-1), pad]).reshape(1, S_W)
    alog = jnp.concatenate([lp['ssd_a_log'].reshape(-1), pad]).reshape(1, S_W)
    dskip = jnp.repeat(lp['ssd_d'][0] + lp['ssd_d'][1], SSD_HEADDIM).reshape(1, GROUP_W)
    cw, cb = lp['ssd_conv_w'], lp['ssd_conv_b'].reshape(1, -1)
    in_specs = ([seq_spec(P_SX, SSD_GW), seq_spec(P_SB, SSD_STATE), seq_spec(P_SC, SSD_STATE),
                 pl.BlockSpec((n_tok, S_W), lambda b, g: (sb + b, 0))]
                + conv_specs(0, SSD_GW) + conv_specs(GROUP_W, SSD_STATE)
                + conv_specs(GROUP_W + SSD_GROUPS * SSD_STATE, SSD_STATE)
                + [small, small, pl.BlockSpec((1, SSD_GW), lambda b, g: (0, g))])
    args = [p, p, p, s, cw, cb, cw, cb, cw, cb, dtb, alog, dskip]
    state_block = pl.BlockSpec((None, 2, SSD_GH, SSD_HEADDIM, SSD_STATE), lambda b, g: (b, 0, g, 0, 0))
    out_specs = [pl.BlockSpec((n_tok, SSD_GW), lambda b, g: (b, g))]
    out_shape = [jax.ShapeDtypeStruct((bsz * n_tok, GROUP_W), jnp.float32)]
    if with_state:
        in_specs.append(state_block)
        args.append(h0)
    else:
        out_specs.append(state_block)
        out_shape.append(jax.ShapeDtypeStruct((bsz, 2, SSD_HEADS, SSD_HEADDIM, SSD_STATE), jnp.float32))
    return pl.pallas_call(
        functools.partial(_ssd_kernel, n_tok=n_tok, with_state=with_state),
        grid=(bsz, SSD_GROUPS),
        in_specs=in_specs,
        out_specs=out_specs,
        out_shape=out_shape,
        scratch_shapes=[pltpu.VMEM((n_tok, SSD_GW), jnp.float32), pltpu.VMEM((n_tok, SSD_GW), jnp.float32),
                        pltpu.VMEM((n_tok, SSD_STATE), MXU_DTYPE), pltpu.VMEM((n_tok, SSD_STATE), MXU_DTYPE),
                        pltpu.VMEM((n_tok, S_W), jnp.float32),
                        pltpu.VMEM((SSD_GW, SSD_STATE), jnp.float32)],
        compiler_params=_params(2, VMEM_LIMIT_V7X),
        name="ssd_scan",
    )(*args)


def _gated_norm_kernel(y_ref, z_ref, w_ref, o_ref):
    o_ref[...] = _rms_rows(y_ref[...] * _silu(z_ref[...]), w_ref[...]).astype(o_ref.dtype)


def _ssd_finish(y, p, row_off, norm_w, out_prev):
    tm = 256
    m, n = p.shape[0], y.shape[0]
    rb = row_off // tm
    in_specs = [pl.BlockSpec((tm, GROUP_W), lambda i: (i, 0)),
                pl.BlockSpec((tm, GROUP_W), lambda i: (rb + i, P_SZ // GROUP_W)),
                pl.BlockSpec((1, GROUP_W), lambda i: (0, 0))]
    args = [y, p, norm_w.reshape(1, GROUP_W)]
    aliases = {}
    if out_prev is not None:
        in_specs.append(pl.BlockSpec(memory_space=pl.ANY))
        args.append(out_prev)
        aliases = {3: 0}

    def body(y_ref, z_ref, w_ref, *rest):
        _gated_norm_kernel(y_ref, z_ref, w_ref, rest[-1])

    return pl.pallas_call(
        body,
        grid=(n // tm,),
        in_specs=in_specs,
        out_specs=pl.BlockSpec((tm, GROUP_W), lambda i: (rb + i, 0)),
        out_shape=jax.ShapeDtypeStruct((m, GROUP_W), MXU_DTYPE),
        input_output_aliases=aliases,
        compiler_params=_params(1, 32 << 20),
        name="ssd_gated_norm",
    )(*args)


GLA_L = 16
GLA_GROUP = 16
LOG2_E = 1.4426950408889634


def _gla_kernel(*refs, n_tok, with_state):
    if with_state:
        (q_ref, k_ref, v_ref, gg_ref, s_ref, gw_ref, gb_ref, nw_ref, s0_ref, o_ref,
         dec_s, qe_s, ke_s, att_s, o_s, st_s) = refs
    else:
        (q_ref, k_ref, v_ref, gg_ref, s_ref, gw_ref, gb_ref, nw_ref, o_ref, sT_ref,
         dec_s, qe_s, ke_s, att_s, o_s, st_s) = refs
    L = GLA_L
    n_blk = n_tok // L
    blk3 = (n_blk, L, GLA_DK)
    q3 = (q_ref[...] * (GLA_DK ** -0.5)).reshape(blk3)
    k3 = k_ref[...].reshape(blk3)
    sb16 = s_ref[...].astype(MXU_DTYPE)
    t = _row_iota((n_tok, GLA_DK))
    ri3 = lax.broadcasted_iota(jnp.int32, blk3, 1)
    lane3 = lax.broadcasted_iota(jnp.int32, blk3, 2)

    for d in range(2):
        up = d == 1
        end = 0 if up else L - 1
        gate = _dot(sb16, gw_ref[d]) + gb_ref[d]
        b = -_softplus(-gate) / GLA_NORMALIZER
        s = 1
        while s < L:
            ok = (t % L < L - s) if up else (t % L >= s)
            b = b + jnp.where(ok, pltpu.roll(b, (n_tok - s) if up else s, 0), 0.0)
            s *= 2
        b3 = b.reshape(blk3)
        b_end = b3[:, end:end + 1, :]
        dec_s[d] = jnp.exp(jnp.broadcast_to(b_end, blk3)).reshape(n_tok, GLA_DK)
        qe_s[d] = (q3 * jnp.exp(b3)).reshape(n_tok, GLA_DK).astype(MXU_DTYPE)
        ke_s[d] = (k3 * jnp.exp(b_end - b3)).reshape(n_tok, GLA_DK).astype(MXU_DTYPE)
        code = jnp.where((ri3 <= lane3) if up else (ri3 >= lane3), lane3, -1)
        b2 = b3 * LOG2_E
        half = L // 2
        att_h = [jnp.zeros((n_blk, half, GLA_DK), jnp.float32) for _ in range(2)]
        for j in range(L):
            for h in range(2):
                if (h == 0 and j >= half) if not up else (h == 1 and j < half):
                    continue
                rows = slice(h * half, (h + 1) * half)
                w = jnp.exp2(b2[:, rows, :] - b2[:, j:j + 1, :]) * q3[:, rows, :] * k3[:, j:j + 1, :]
                att_h[h] = jnp.where(code[:, rows, :] == j, jnp.sum(w, axis=-1, keepdims=True), att_h[h])
        att = jnp.concatenate(att_h, axis=1)
        att_s[d] = att.reshape(n_tok, GLA_DK).astype(MXU_DTYPE)
        if with_state:
            st_s[d] = s0_ref[d].T
        else:
            st_s[d] = jnp.zeros((GLA_DV, GLA_DK), jnp.float32)

    o_s[...] = jnp.zeros_like(o_s)
    n_grp = n_blk // GLA_GROUP
    rows_per_trip = GLA_GROUP * L

    def group(i, carry):
        for d in range(2):
            up = d == 1
            c = (n_grp - 1 - i) if up else i
            rows = pl.ds(pl.multiple_of(c * rows_per_trip, rows_per_trip), rows_per_trip)
            vg = v_ref[rows, :].astype(MXU_DTYPE)
            keg, attg, qeg, decg = ke_s[d, rows, :], att_s[d, rows, 0:L], qe_s[d, rows, :], dec_s[d, rows, :]
            order = range(GLA_GROUP - 1, -1, -1) if up else range(GLA_GROUP)
            upd = {k: _dot_tn(vg[k * L:(k + 1) * L], keg[k * L:(k + 1) * L]) for k in order}
            intra = {k: _dot(attg[k * L:(k + 1) * L], vg[k * L:(k + 1) * L]) for k in order}
            st = st_s[d]
            outs = {}
            for k in order:
                outs[k] = _dot_nt(qeg[k * L:(k + 1) * L], st.astype(MXU_DTYPE)) + intra[k]
                st = st * decg[k * L:k * L + 1, :] + upd[k]
            st_s[d] = st
            o_s[rows, :] += jnp.concatenate([outs[k] for k in range(GLA_GROUP)], axis=0)
        return carry

    lax.fori_loop(0, n_grp, group, 0)
    if not with_state:
        for d in range(2):
            sT_ref[d] = st_s[d].T
    o = _rms_rows(o_s[...], nw_ref[...]) * _silu(gg_ref[...])
    o_ref[...] = o.astype(o_ref.dtype)


def _gla_call(p, s, row_off, bsz, n_tok, lp, s0, out_prev):
    m = p.shape[0]
    with_state = s0 is not None
    sb = row_off // n_tok
    assert row_off % n_tok == 0 and n_tok % (GLA_L * GLA_GROUP) == 0

    def seq_spec(col0, w):
        return pl.BlockSpec((n_tok, w), lambda b, h: (sb + b, col0 // w + h))

    gw = lp['gla_gate_w'].reshape(2, GLA_RANK, GLA_HEADS, GLA_DK).transpose(0, 2, 1, 3)
    gw_rows = jnp.zeros((2, GLA_HEADS, S_W, GLA_DK), jnp.float32)
    for d in range(2):
        r0 = 2 * SSD_HEADS + d * GLA_RANK
        gw_rows = gw_rows.at[d, :, r0:r0 + GLA_RANK, :].set(gw[d])
    in_specs = [seq_spec(P_GQ, GLA_DK), seq_spec(P_GK, GLA_DK), seq_spec(P_GV, GLA_DV), seq_spec(P_GG, GLA_DV),
                pl.BlockSpec((n_tok, S_W), lambda b, h: (sb + b, 0)),
                pl.BlockSpec((2, None, S_W, GLA_DK), lambda b, h: (0, h, 0, 0)),
                pl.BlockSpec((2, None, 1, GLA_DK), lambda b, h: (0, h, 0, 0)),
                pl.BlockSpec((1, GLA_DV), lambda b, h: (0, 0))]
    args = [p, p, p, p, s, gw_rows.astype(MXU_DTYPE), lp['gla_gate_b'].reshape(2, GLA_HEADS, 1, GLA_DK),
            lp['gla_norm_w'].reshape(1, GLA_DV)]
    state_block = pl.BlockSpec((None, 2, None, GLA_DK, GLA_DV), lambda b, h: (b, 0, h, 0, 0))
    out_specs = [pl.BlockSpec((n_tok, GLA_DV), lambda b, h: (sb + b, h))]
    out_shape = [jax.ShapeDtypeStruct((m, GROUP_W), MXU_DTYPE)]
    if with_state:
        in_specs.append(state_block)
        args.append(s0)
    else:
        out_specs.append(state_block)
        out_shape.append(jax.ShapeDtypeStruct((bsz, 2, GLA_HEADS, GLA_DK, GLA_DV), jnp.float32))
    n_in = len(args)
    aliases = {}
    if out_prev is not None:
        in_specs.append(pl.BlockSpec(memory_space=pl.ANY))
        args.append(out_prev)
        aliases = {n_in: 0}

    def body(*refs):
        if out_prev is not None:
            refs = refs[:n_in] + refs[n_in + 1:]
        _gla_kernel(*refs, n_tok=n_tok, with_state=with_state)

    return pl.pallas_call(
        body,
        grid=(bsz, GLA_HEADS),
        in_specs=in_specs,
        out_specs=out_specs,
        out_shape=out_shape,
        scratch_shapes=[pltpu.VMEM((2, n_tok, GLA_DK), jnp.float32), pltpu.VMEM((2, n_tok, GLA_DK), MXU_DTYPE),
                        pltpu.VMEM((2, n_tok, GLA_DK), MXU_DTYPE), pltpu.VMEM((2, n_tok, GLA_DK), MXU_DTYPE),
                        pltpu.VMEM((n_tok, GLA_DV), jnp.float32), pltpu.VMEM((2, GLA_DV, GLA_DK), jnp.float32)],
        input_output_aliases=aliases,
        compiler_params=_params(2, 48 << 20),
        name="gla_scan",
    )(*args)


def _mixers(p, s, lp, cached):
    ctx_k, ctx_v, ssd0, gla0, lru0 = cached
    kv_w = ATT_KV_HEADS * HEAD_DIM
    att, k_new = _attention_call(p, 0, BATCH, SEQ, lp['q_norm'], lp['k_norm'], None, None)
    att, = _attention_call(p, M_PROMPT, DEC_BATCH, DEC_SEQ, lp['q_norm'], lp['k_norm'],
                           (ctx_k.reshape(DEC_BATCH, PAST_LEN, kv_w), ctx_v.reshape(DEC_BATCH, PAST_LEN, kv_w)), att)
    v_new = p[:M_PROMPT, P_AV:P_AV + kv_w]

    y_p, ssd_new = _ssd_call(p, s, 0, BATCH, SEQ, lp, None)
    y_s, = _ssd_call(p, s, M_PROMPT, DEC_BATCH, DEC_SEQ, lp, ssd0)
    ssd = _ssd_finish(y_p, p, 0, lp['ssd_norm_w'], None)
    ssd = _ssd_finish(y_s, p, M_PROMPT, lp['ssd_norm_w'], ssd)

    gla, gla_new = _gla_call(p, s, 0, BATCH, SEQ, lp, None, None)
    gla, = _gla_call(p, s, M_PROMPT, DEC_BATCH, DEC_SEQ, lp, gla0, gla)

    lru, lru_new = _lru_call(p, 0, BATCH, SEQ, lp, None, None)
    lru, = _lru_call(p, M_PROMPT, DEC_BATCH, DEC_SEQ, lp, lru0, lru)

    new_ctx = (k_new.reshape(BATCH, SEQ, ATT_KV_HEADS, HEAD_DIM), v_new.reshape(BATCH, SEQ, ATT_KV_HEADS, HEAD_DIM),
               ssd_new, gla_new, lru_new.reshape(BATCH, 2, LRU_W))
    return [att, ssd, gla, lru], new_ctx


IN_W = 9280
_W_IN_MAIN = ((0, P_AQ, 1024), (1024, P_AK, 256), (1280, P_AV, 256), (1536, P_SX, 1024), (2560, P_SZ, 1024),
              (3584, P_SB, 256), (3840, P_SC, 256), (4128, P_GQ, 512), (4640, P_GK, 512), (5152, P_GV, 1024),
              (6208, P_GG, 1024), (7232, P_LX, 1024), (8256, P_LG, 1024))
_W_IN_SMALL = ((4096, 0, 2 * SSD_HEADS), (6176, 2 * SSD_HEADS, 2 * GLA_RANK))


def _w_in_kernel(w_ref, main_ref, small_ref):
    for src, dst, width in _W_IN_MAIN:
        main_ref[:, dst:dst + width] = w_ref[:, src:src + width].astype(main_ref.dtype)
    small_ref[...] = jnp.zeros_like(small_ref)
    for src, dst, width in _W_IN_SMALL:
        small_ref[:, dst:dst + width] = w_ref[:, src:src + width].astype(small_ref.dtype)


def _reorder_w_in(w, l):
    tk = 256
    return pl.pallas_call(
        _w_in_kernel,
        grid=(D_MODEL // tk,),
        in_specs=[pl.BlockSpec((None, tk, IN_W), lambda i: (l, i, 0))],
        out_specs=[pl.BlockSpec((tk, P_W), lambda i: (i, 0)), pl.BlockSpec((tk, S_W), lambda i: (i, 0))],
        out_shape=[jax.ShapeDtypeStruct((D_MODEL, P_W), MXU_DTYPE), jax.ShapeDtypeStruct((D_MODEL, S_W), MXU_DTYPE)],
        compiler_params=_params(1, 48 << 20),
        name="w_in_regroup",
    )(w)


def kernel(x_prompt, x_sample, c, cache_attn_k, cache_attn_v, state_ssd, state_gla, state_lru, c_ctx,
           mod_w, mod_b, ln_g, ln_b, ffn_w_gate, ffn_w_up, ffn_w_down, w_in, w_out, q_norm, k_norm,
           ssd_conv_w, ssd_conv_b, ssd_a_log, ssd_dt_bias, ssd_d, ssd_norm_w,
           gla_gate_w, gla_gate_b, gla_norm_w,
           lru_conv_w, lru_conv_b, lru_wa, lru_ba, lru_wx, lru_bx, lru_lam):
    cond = jnp.concatenate([c_ctx[None], c, jnp.zeros((COND_ROWS - N_COND, D_MODEL), jnp.float32)], axis=0)
    mod_all = _mod_table(cond, mod_w, mod_b)

    def ffn(xm, l, half):
        h = _dense_cast_first([xm], [ffn_w_gate, ffn_w_up], (l, half), tm_first=2048, tn_first=256, tm=2048, tn=256,
                              out_dtype=MXU_DTYPE, name="ffn_up", swiglu=True)
        return _dense_cast_first([h], [ffn_w_down], (l, half), tm_first=512, tn_first=256, tm=512, tn=512,
                                 out_dtype=jnp.float32, name="ffn_down")

    x, xm = _modulate(x_prompt, x_sample, mod_all[0], 1, 0)
    ctx_out = []
    for l in range(DEPTH):
        mod = mod_all[l]
        lp = dict(q_norm=q_norm[l], k_norm=k_norm[l], ssd_conv_w=ssd_conv_w[l], ssd_conv_b=ssd_conv_b[l],
                  ssd_a_log=ssd_a_log[l], ssd_dt_bias=ssd_dt_bias[l], ssd_d=ssd_d[l], ssd_norm_w=ssd_norm_w[l],
                  gla_gate_w=gla_gate_w[l], gla_gate_b=gla_gate_b[l], gla_norm_w=gla_norm_w[l],
                  lru_conv_w=lru_conv_w[l], lru_conv_b=lru_conv_b[l], lru_wa=lru_wa[l], lru_ba=lru_ba[l],
                  lru_wx=lru_wx[l], lru_bx=lru_bx[l], lru_lam=lru_lam[l])
        x, xm = _resid_ln(x, ffn(xm, l, 0), mod, 2, 0.5, ln_g[l, 0], ln_b[l, 0], mod, 4, 3)
        w_main, w_small = _reorder_w_in(w_in, l)
        dense = dict(tile0=0, n_tiles=M_TOK // 1024, tm=1024, out_dtype=jnp.float32)
        p = _dense([xm], [w_main], (), tn=1024, name="proj_in", **dense)
        s = _dense([xm], [w_small], (), tn=S_W, name="proj_in_small", **dense)
        cached = (cache_attn_k[:, l], cache_attn_v[:, l], state_ssd[:, l], state_gla[:, l], state_lru[:, l])
        mix, new_ctx = _mixers(p, s, lp, cached)
        ctx_out.append(new_ctx)
        m_out = _dense_cast_first(mix, [w_out], (l,), tm_first=1024, tn_first=512, tm=1024, tn=1024,
                                  out_dtype=jnp.float32, name="proj_out")
        x, xm = _resid_ln(x, m_out, mod, 5, 1.0, ln_g[l, 1], ln_b[l, 1], mod, 7, 6)
        f = ffn(xm, l, 1)
        if l + 1 < DEPTH:
            x, xm = _resid_ln(x, f, mod, 8, 0.5, ln_g[l, 2], ln_b[l, 2], mod_all[l + 1], 1, 0)
        else:
            y_prompt, y_sample = _resid_ln(x, f, mod, 8, 0.5, ln_g[l, 2], ln_b[l, 2])

    y_prompt = y_prompt.reshape(BATCH, SEQ, D_MODEL)
    y_sample = y_sample.reshape(DEC_BATCH, DEC_SEQ, D_MODEL)
    new_k, new_v, new_ssd, new_gla, new_lru = (jnp.stack([s_[i] for s_ in ctx_out], axis=1) for i in range(5))
    return (y_prompt, y_sample, new_k, new_v, new_ssd, new_gla, new_lru)
```

```python
import functools
import math

import numpy as np
import jax
import jax.numpy as jnp
from jax import lax
from jax.experimental import pallas as pl
from jax.experimental.pallas import tpu as pltpu

D_MODEL = 4096
BATCH = 16
SEQ = 256
DEPTH = 2
DEC_BATCH = 4
DEC_SEQ = 2048
PAST_LEN = 256
GRID_W = 64
GROUP_W = 1024
HEAD_DIM = 128
ATT_HEADS = 8
ATT_KV_HEADS = 2
ROPE_THETA = 10000.0
SSD_HEADDIM = 64
SSD_HEADS = 16
SSD_STATE = 128
SSD_GROUPS = 2
CONV_W = 4
GLA_HEADS = 4
GLA_DK = 128
GLA_DV = 256
GLA_RANK = 16
GLA_NORMALIZER = 16.0
LRU_W = 1024
LRU_BLOCKS = 8
LRU_BW = 128
LRU_C = 8.0
D_FF = 11008
N_MOD = 9
LN_EPS = 1e-5
RMS_EPS = 1e-6
ALPHA = (2.0 * DEPTH) ** 0.25

M_PROMPT = BATCH * SEQ
M_SAMPLE = DEC_BATCH * DEC_SEQ
M_TOK = M_PROMPT + M_SAMPLE
N_COND = 1 + DEC_BATCH
COND_ROWS = 8

VMEM_LIMIT_V7X = 56 * 1024 * 1024

MXU_DTYPE = jnp.bfloat16

P_AQ, P_SX, P_SZ, P_GV, P_GG, P_LX, P_LG = 0, 1024, 2048, 3072, 4096, 5120, 6144
P_GQ, P_GK = 7168, 7680
P_AK, P_AV, P_SB, P_SC = 8192, 8448, 8704, 8960
P_W = 9216
S_W = 128


def _params(n_axes, vmem_bytes):
    return pltpu.CompilerParams(dimension_semantics=("arbitrary",) * n_axes,
                                vmem_limit_bytes=min(int(vmem_bytes), VMEM_LIMIT_V7X))


def _cond_row(tile_idx, tile_rows):
    row0 = tile_idx * tile_rows
    return jnp.where(row0 < M_PROMPT, 0, 1 + (row0 - M_PROMPT) // DEC_SEQ)


def _mod_kernel(c_ref, w_ref, b_ref, o_ref):
    c = c_ref[...]
    a = (c * jax.nn.sigmoid(c)).astype(MXU_DTYPE)
    w = w_ref[...].astype(MXU_DTYPE)
    o_ref[...] = jnp.dot(a, w, preferred_element_type=jnp.float32) + b_ref[...]


def _mod_table(cond, mod_w, mod_b):
    tn = 512
    n = N_MOD * D_MODEL
    return pl.pallas_call(
        _mod_kernel,
        grid=(DEPTH, n // tn),
        in_specs=[pl.BlockSpec((COND_ROWS, D_MODEL), lambda l, j: (0, 0)),
                  pl.BlockSpec((None, D_MODEL, tn), lambda l, j: (l, 0, j)),
                  pl.BlockSpec((None, 1, tn), lambda l, j: (l, 0, j))],
        out_specs=pl.BlockSpec((None, COND_ROWS, tn), lambda l, j: (l, 0, j)),
        out_shape=jax.ShapeDtypeStruct((DEPTH, COND_ROWS, n), jnp.float32),
        compiler_params=_params(2, 3 * D_MODEL * tn * 4 + (4 << 20)),
        name="mod_table",
    )(cond, mod_w, mod_b.reshape(DEPTH, 1, n))


def _dense_kernel(*refs, n_a, n_w, cast, swiglu, resid):
    a_refs, w_refs = refs[:n_a], refs[n_a:n_a + n_w]
    n_in = n_a + n_w + (2 if resid else 0)
    o_ref = refs[n_in]
    cast_refs = refs[n_in + 1:n_in + n_w + 1] if cast else ()
    accs = []
    for wi, w_ref in enumerate(w_refs):
        if cast:
            w_narrow = w_ref[...].astype(MXU_DTYPE)
            cast_refs[wi][...] = w_narrow
        k0, acc = 0, None
        for a_ref in a_refs:
            kp = a_ref.shape[1]
            w_rows = w_narrow[k0:k0 + kp, :] if cast else w_ref[k0:k0 + kp, :]
            part = jnp.dot(a_ref[...], w_rows, preferred_element_type=jnp.float32)
            acc = part if acc is None else acc + part
            k0 += kp
        accs.append(acc)
    out = accs[0] * jax.nn.sigmoid(accs[0]) * accs[1] if swiglu else accs[0]
    if resid:
        x_ref, g_ref = refs[n_a + n_w], refs[n_a + n_w + 1]
        tile0, tm, gate_scale = resid
        g = g_ref[pl.ds(_cond_row(tile0 + pl.program_id(0), tm), 1), :]
        out = ALPHA * x_ref[...] + (gate_scale * g) * out
    o_ref[...] = out.astype(o_ref.dtype)


def _dense(a_list, w_list, w_index, *, tile0, n_tiles, tm, tn, out_dtype, name, swiglu=False, cast=False,
           out_prev=None, resid=None):
    m = a_list[0].shape[0]
    k, n = w_list[0].shape[-2:]
    n_a, n_w, lead = len(a_list), len(w_list), len(w_index)
    assert sum(a.shape[1] for a in a_list) == k and n % tn == 0 and (tile0 + n_tiles) * tm <= m
    a_mode = dict(pipeline_mode=pl.Buffered(1)) if n_tiles == 1 else {}
    in_specs = [pl.BlockSpec((tm, a.shape[1]), lambda i, j: (tile0 + i, 0), **a_mode) for a in a_list]
    in_specs += [pl.BlockSpec((None,) * lead + (k, tn), lambda i, j: tuple(w_index) + (0, j)) for _ in w_list]
    args = list(a_list) + list(w_list)
    if resid is not None:
        x, mod, k_gate, gate_scale = resid
        in_specs += [pl.BlockSpec((tm, tn), lambda i, j: (tile0 + i, j)),
                     pl.BlockSpec((COND_ROWS, tn), lambda i, j: (0, k_gate * (n // tn) + j))]
        args += [x, mod]
    n_in = len(args)
    out_specs = [pl.BlockSpec((tm, tn), lambda i, j: (tile0 + i, j))]
    out_shape = [jax.ShapeDtypeStruct((m, n), out_dtype)]
    if cast:
        out_specs += [pl.BlockSpec((k, tn), lambda i, j: (0, j)) for _ in w_list]
        out_shape += [jax.ShapeDtypeStruct((k, n), MXU_DTYPE) for _ in w_list]
    aliases = {}
    if out_prev is not None:
        in_specs.append(pl.BlockSpec(memory_space=pl.ANY))
        args.append(out_prev)
        aliases = {n_in: 0}
    esz, wsz, osz = jnp.dtype(MXU_DTYPE).itemsize, w_list[0].dtype.itemsize, jnp.dtype(out_dtype).itemsize
    vmem = ((1 if n_tiles == 1 else 2) * tm * k * esz + 2 * n_w * k * tn * wsz + 2 * tm * tn * osz
            + (n_w + 1) * tm * tn * 4 + (3 * n_w * k * tn * esz if cast else 0)
            + (2 * tm * tn * 4 if resid is not None else 0))

    def body(*refs):
        if out_prev is not None:
            refs = refs[:n_in] + refs[n_in + 1:]
        _dense_kernel(*refs, n_a=n_a, n_w=n_w, cast=cast, swiglu=swiglu,
                      resid=None if resid is None else (tile0, tm, resid[3]))

    out = pl.pallas_call(
        body,
        grid=(n_tiles, n // tn),
        in_specs=in_specs,
        out_specs=out_specs,
        out_shape=out_shape,
        input_output_aliases=aliases,
        compiler_params=_params(2, vmem + (4 << 20)),
        name=name,
    )(*args)
    return out if cast else out[0]


def _dense_cast_first(a_list, w_list, w_index, donor, *, tm_first, tn_first, tm, tn, out_dtype, name, swiglu=False,
                      resid=None):
    m = a_list[0].shape[0]
    first = _dense(a_list, w_list, w_index, tile0=0, n_tiles=1, tm=tm_first, tn=tn_first, out_dtype=out_dtype,
                   name=name + "_first", swiglu=swiglu, cast=True, out_prev=donor, resid=resid)
    assert tm_first % tm == 0 or tm_first == tm
    return _dense(a_list, first[1:], (), tile0=tm_first // tm, n_tiles=(m - tm_first) // tm, tm=tm, tn=tn,
                  out_dtype=out_dtype, name=name, swiglu=swiglu, out_prev=first[0], resid=resid)


def _modulate_kernel(xp_ref, xs_ref, sc_ref, sh_ref, x_ref, o_ref, *, tm):
    i = pl.program_id(0)
    r = _cond_row(i, tm)
    sc = sc_ref[pl.ds(r, 1), :]
    sh = sh_ref[pl.ds(r, 1), :]

    def emit(x):
        x_ref[...] = x
        o_ref[...] = (x * (1.0 + sc) + sh).astype(o_ref.dtype)

    pl.when(i < M_PROMPT // tm)(lambda: emit(xp_ref[...]))
    pl.when(i >= M_PROMPT // tm)(lambda: emit(xs_ref[...]))


def _mod_spec(k):
    return pl.BlockSpec((COND_ROWS, D_MODEL), lambda i: (0, k))


def _modulate(x_prompt, x_sample, mod, k_scale, k_shift):
    tm = 256
    n_p = M_PROMPT // tm
    row = pl.BlockSpec((tm, D_MODEL), lambda i: (i, 0))
    return pl.pallas_call(
        functools.partial(_modulate_kernel, tm=tm),
        grid=(M_TOK // tm,),
        in_specs=[pl.BlockSpec((tm, D_MODEL), lambda i: (jnp.minimum(i, n_p - 1), 0)),
                  pl.BlockSpec((tm, D_MODEL), lambda i: (jnp.maximum(i - n_p, 0), 0)),
                  _mod_spec(k_scale), _mod_spec(k_shift)],
        out_specs=[row, row],
        out_shape=[jax.ShapeDtypeStruct((M_TOK, D_MODEL), jnp.float32),
                   jax.ShapeDtypeStruct((M_TOK, D_MODEL), MXU_DTYPE)],
        compiler_params=_params(1, 32 << 20),
        name="modulate",
    )(x_prompt.reshape(M_PROMPT, D_MODEL), x_sample.reshape(M_SAMPLE, D_MODEL), mod, mod)


def _ln_kernel(*refs, tm, with_next):
    if with_next:
        y_ref, lg_ref, lb_ref, sc_ref, sh_ref, xo_ref, mo_ref = refs
    else:
        y_ref, lg_ref, lb_ref, yp_ref, ys_ref = refs
    i = pl.program_id(0)
    r = _cond_row(i, tm)
    y = y_ref[...]
    mu = jnp.mean(y, axis=-1, keepdims=True)
    yc = y - mu
    var = jnp.mean(yc * yc, axis=-1, keepdims=True)
    xn = yc * lax.rsqrt(var + LN_EPS) * lg_ref[...] + lb_ref[...]
    if with_next:
        xo_ref[...] = xn
        sc = sc_ref[pl.ds(r, 1), :]
        sh = sh_ref[pl.ds(r, 1), :]
        mo_ref[...] = (xn * (1.0 + sc) + sh).astype(mo_ref.dtype)
    else:
        @pl.when(i < M_PROMPT // tm)
        def _():
            yp_ref[...] = xn

        @pl.when(i >= M_PROMPT // tm)
        def _():
            ys_ref[...] = xn


def _layer_norm(y, ln_g, ln_b, next_mod=None, k_scale=0, k_shift=0):
    tm = 256
    m = y.shape[0]
    n_p = M_PROMPT // tm
    with_next = next_mod is not None
    row = pl.BlockSpec((tm, D_MODEL), lambda i: (i, 0))
    vec = pl.BlockSpec((1, D_MODEL), lambda i: (0, 0))
    in_specs = [row, vec, vec]
    args = [y, ln_g.reshape(1, D_MODEL), ln_b.reshape(1, D_MODEL)]
    if with_next:
        in_specs += [_mod_spec(k_scale), _mod_spec(k_shift)]
        args += [next_mod, next_mod]
        out_specs = [row, row]
        out_shape = [jax.ShapeDtypeStruct((m, D_MODEL), jnp.float32), jax.ShapeDtypeStruct((m, D_MODEL), MXU_DTYPE)]
    else:
        out_specs = [pl.BlockSpec((tm, D_MODEL), lambda i: (jnp.minimum(i, n_p - 1), 0)),
                     pl.BlockSpec((tm, D_MODEL), lambda i: (jnp.maximum(i - n_p, 0), 0))]
        out_shape = [jax.ShapeDtypeStruct((M_PROMPT, D_MODEL), jnp.float32),
                     jax.ShapeDtypeStruct((M_SAMPLE, D_MODEL), jnp.float32)]
    out = pl.pallas_call(
        functools.partial(_ln_kernel, tm=tm, with_next=with_next),
        grid=(m // tm,),
        in_specs=in_specs,
        out_specs=out_specs,
        out_shape=out_shape,
        compiler_params=_params(1, 48 << 20),
        name="layer_norm",
    )(*args)
    return out[0], out[1]


def _row_iota(shape):
    return lax.broadcasted_iota(jnp.int32, shape, 0)


def _lane_iota(shape):
    return lax.broadcasted_iota(jnp.int32, shape, 1)


def _shift_rows(x, s, fill, up=False):
    n = x.shape[0]
    t = _row_iota(x.shape)
    if up:
        return jnp.where(t < n - s, pltpu.roll(x, n - s, 0), fill)
    return jnp.where(t >= s, pltpu.roll(x, s, 0), fill)


def _softplus(z):
    return jnp.maximum(z, 0.0) + jnp.log1p(jnp.exp(-jnp.abs(z)))


def _silu(z):
    return z * jax.nn.sigmoid(z)


def _dot(a, b):
    return jnp.dot(a, b, preferred_element_type=jnp.float32)


def _dot_nt(a, b):
    return lax.dot_general(a, b, (((1,), (1,)), ((), ())), preferred_element_type=jnp.float32)


def _dot_tn(a, b):
    return lax.dot_general(a, b, (((0,), (0,)), ((), ())), preferred_element_type=jnp.float32)


def _split3(x):
    hi = x.astype(MXU_DTYPE)
    r1 = x - hi.astype(jnp.float32)
    mid = r1.astype(MXU_DTYPE)
    lo = (r1 - mid.astype(jnp.float32)).astype(MXU_DTYPE)
    return hi, mid, lo


def _dot01_right(x, e):
    hi, mid, lo = _split3(x)
    return _dot(hi, e) + _dot(mid, e) + _dot(lo, e)


def _dot01_left(e, x):
    hi, mid, lo = _split3(x)
    return _dot(e, hi) + _dot(e, mid) + _dot(e, lo)


def _conv4(x, w_ref, b_ref):
    acc = _shift_rows(x, 2, 0.0) * w_ref[0:1, :]
    acc = acc + _shift_rows(x, 1, 0.0) * w_ref[1:2, :]
    acc = acc + x * w_ref[2:3, :]
    acc = acc + _shift_rows(x, 1, 0.0, up=True) * w_ref[3:4, :]
    return acc + b_ref[...]


def _rms_rows(x, w):
    return x * lax.rsqrt(jnp.mean(x * x, axis=-1, keepdims=True) + RMS_EPS) * w


ATT_STACK = 2


def _rope_tables(n_tok):
    rows = n_tok // GRID_W
    row = jnp.repeat(jnp.arange(rows, dtype=jnp.float32), GRID_W)
    col = jnp.tile(jnp.arange(GRID_W, dtype=jnp.float32), rows)
    n_freq = HEAD_DIM // 4
    inv = ROPE_THETA ** (-jnp.arange(n_freq, dtype=jnp.float32) / n_freq)
    ar, ac = row[:, None] * inv, col[:, None] * inv
    cos = jnp.concatenate([jnp.cos(ar), jnp.cos(ar), jnp.cos(ac), jnp.cos(ac)], axis=1)
    sin = jnp.concatenate([-jnp.sin(ar), jnp.sin(ar), -jnp.sin(ac), jnp.sin(ac)], axis=1)
    return cos, sin


def _rope(x, cos, sin):
    quarter = HEAD_DIM // 4
    lane = _lane_iota(x.shape)
    partner = jnp.where(lane % (2 * quarter) < quarter,
                        pltpu.roll(x, HEAD_DIM - quarter, 1), pltpu.roll(x, quarter, 1))
    return x * cos + partner * sin


def _attn_kernel(*refs, n_tok, tq, n_ctx, rope):
    if rope:
        (q_ref, k_ref, v_ref, qn_ref, kn_ref, cq_ref, sq_ref, ck_ref, sk_ref, xk_ref, xv_ref,
         o_ref, ks_ref, vs_ref) = refs
    else:
        q_ref, k_ref, v_ref, qn_ref, kn_ref, o_ref, ko_ref, ks_ref, vs_ref = refs
    rep = ATT_HEADS // ATT_KV_HEADS

    @pl.when(pl.program_id(2) == 0)
    def _():
        kn = _rms_rows(k_ref[...], kn_ref[...])
        if rope:
            kn = _rope(kn, ck_ref[...], sk_ref[...])
            ks_ref[n_tok:n_tok + n_ctx, :] = xk_ref[...].astype(MXU_DTYPE)
            vs_ref[n_tok:n_tok + n_ctx, :] = xv_ref[...].astype(MXU_DTYPE)
        else:
            ko_ref[...] = kn
        ks_ref[0:n_tok, :] = kn.astype(MXU_DTYPE)
        vs_ref[0:n_tok, :] = v_ref[...].astype(MXU_DTYPE)

    heads = []
    for r in range(rep):
        qh = _rms_rows(q_ref[:, r * HEAD_DIM:(r + 1) * HEAD_DIM], qn_ref[...])
        if rope:
            qh = _rope(qh, cq_ref[...], sq_ref[...])
        heads.append(qh.astype(MXU_DTYPE))
    for r0 in range(0, rep, ATT_STACK):
        qs = jnp.concatenate(heads[r0:r0 + ATT_STACK], axis=0)
        s = _dot_nt(qs, ks_ref[...]) * (HEAD_DIM ** -0.5)
        e = jnp.exp(s - jnp.max(s, axis=-1, keepdims=True))
        o = _dot(e.astype(MXU_DTYPE), vs_ref[...]) / jnp.sum(e, axis=-1, keepdims=True)
        for r in range(ATT_STACK):
            o_ref[:, (r0 + r) * HEAD_DIM:(r0 + r + 1) * HEAD_DIM] = o[r * tq:(r + 1) * tq].astype(o_ref.dtype)


def _attention_call(p, row_off, bsz, n_tok, q_norm, k_norm, ctx_kv, out_prev):
    m = p.shape[0]
    rope = ctx_kv is not None
    tq = 128 if rope else n_tok
    nq = n_tok // tq
    rep = ATT_HEADS // ATT_KV_HEADS
    qw = rep * HEAD_DIM
    n_ctx = ctx_kv[0].shape[1] if rope else 0
    assert row_off % n_tok == 0 and n_tok % tq == 0
    rb, sb = row_off // tq, row_off // n_tok

    def seq_spec(col0):
        return pl.BlockSpec((n_tok, HEAD_DIM), lambda b, g, i: (sb + b, col0 // HEAD_DIM + g))

    vec = pl.BlockSpec((1, HEAD_DIM), lambda b, g, i: (0, 0))
    in_specs = [pl.BlockSpec((tq, qw), lambda b, g, i: (rb + b * nq + i, P_AQ // qw + g)),
                seq_spec(P_AK), seq_spec(P_AV), vec, vec]
    args = [p, p, p, q_norm.reshape(1, HEAD_DIM), k_norm.reshape(1, HEAD_DIM)]
    out_block = pl.BlockSpec((tq, qw), lambda b, g, i: (rb + b * nq + i, g))
    out_specs = [out_block]
    out_shape = [jax.ShapeDtypeStruct((m, GROUP_W), MXU_DTYPE)]
    aliases = {}
    if rope:
        cos, sin = _rope_tables(n_tok)
        in_specs += [pl.BlockSpec((tq, HEAD_DIM), lambda b, g, i: (i, 0))] * 2
        in_specs += [pl.BlockSpec((n_tok, HEAD_DIM), lambda b, g, i: (0, 0))] * 2
        in_specs += [pl.BlockSpec((None, n_ctx, HEAD_DIM), lambda b, g, i: (b, 0, g))] * 2
        args += [cos, sin, cos, sin, ctx_kv[0], ctx_kv[1]]
    else:
        out_specs.append(pl.BlockSpec((n_tok, HEAD_DIM), lambda b, g, i: (b, g)))
        out_shape.append(jax.ShapeDtypeStruct((bsz * n_tok, ATT_KV_HEADS * HEAD_DIM), jnp.float32))
    if out_prev is not None:
        in_specs.append(pl.BlockSpec(memory_space=pl.ANY))
        args.append(out_prev)
        aliases = {len(args) - 1: 0}
    n_keys = n_tok + n_ctx

    def body(*refs):
        if out_prev is not None:
            refs = refs[:len(args) - 1] + refs[len(args):]
        _attn_kernel(*refs, n_tok=n_tok, tq=tq, n_ctx=n_ctx, rope=rope)

    out = pl.pallas_call(
        body,
        grid=(bsz, ATT_KV_HEADS, nq),
        in_specs=in_specs,
        out_specs=out_specs,
        out_shape=out_shape,
        scratch_shapes=[pltpu.VMEM((n_keys, HEAD_DIM), MXU_DTYPE), pltpu.VMEM((n_keys, HEAD_DIM), MXU_DTYPE)],
        input_output_aliases=aliases,
        compiler_params=_params(3, 48 << 20),
        name="attention",
    )(*args)
    return out


def _lru_kernel(*refs, n_tok, with_state):
    if with_state:
        (x_ref, g_ref, cw_ref, cb_ref, wa_ref, ba_ref, wx_ref, bx_ref, lam_ref, h0_ref, o_ref) = refs
    else:
        (x_ref, g_ref, cw_ref, cb_ref, wa_ref, ba_ref, wx_ref, bx_ref, lam_ref, o_ref, hT_ref) = refs
    xl = _conv4(x_ref[...], cw_ref, cb_ref)
    xb = xl.astype(MXU_DTYPE)
    t = _row_iota(xl.shape)
    h_sum = None
    for d in range(2):
        up = d == 1
        r = jax.nn.sigmoid(_dot(xb, wa_ref[d]) + ba_ref[d])
        i = jax.nn.sigmoid(_dot(xb, wx_ref[d]) + bx_ref[d])
        log_a = -LRU_C * r * _softplus(-lam_ref[d])
        a = jnp.exp(log_a)
        th = jnp.tanh(log_a)
        u = jnp.sqrt(-2.0 * th / (1.0 - th)) * i * xl
        if with_state:
            first = (t == n_tok - 1) if up else (t == 0)
            u = jnp.where(first, u + a * h0_ref[d], u)
        s = 1
        while s < n_tok:
            u = u + a * _shift_rows(u, s, 0.0, up=up)
            a = a * _shift_rows(a, s, 1.0, up=up)
            s *= 2
        h_sum = u if h_sum is None else h_sum + u
        if not with_state:
            hT_ref[d] = u[0:1, :] if up else u[n_tok - 1:n_tok, :]
    o_ref[...] = (h_sum * jax.nn.gelu(g_ref[...])).astype(o_ref.dtype)


def _lru_call(p, row_off, bsz, n_tok, lp, h0, out_prev):
    m = p.shape[0]
    with_state = h0 is not None
    sb = row_off // n_tok
    assert row_off % n_tok == 0

    def seq_spec(col0):
        return pl.BlockSpec((n_tok, LRU_BW), lambda b, n: (sb + b, col0 // LRU_BW + n))

    def par2(shape):
        return pl.BlockSpec((2,) + shape + (LRU_BW,), lambda b, n: (0,) + (0,) * len(shape) + (n,))

    in_specs = [seq_spec(P_LX), seq_spec(P_LG),
                pl.BlockSpec((CONV_W, LRU_BW), lambda b, n: (0, n)),
                pl.BlockSpec((1, LRU_BW), lambda b, n: (0, n)),
                pl.BlockSpec((2, None, LRU_BW, LRU_BW), lambda b, n: (0, n, 0, 0)), par2((1,)),
                pl.BlockSpec((2, None, LRU_BW, LRU_BW), lambda b, n: (0, n, 0, 0)), par2((1,)), par2((1,))]
    args = [p, p, lp['lru_conv_w'], lp['lru_conv_b'].reshape(1, LRU_W),
            lp['lru_wa'].astype(MXU_DTYPE), lp['lru_ba'].reshape(2, 1, LRU_W),
            lp['lru_wx'].astype(MXU_DTYPE), lp['lru_bx'].reshape(2, 1, LRU_W), lp['lru_lam'].reshape(2, 1, LRU_W)]
    out_specs = [pl.BlockSpec((n_tok, LRU_BW), lambda b, n: (sb + b, n))]
    out_shape = [jax.ShapeDtypeStruct((m, GROUP_W), MXU_DTYPE)]
    if with_state:
        in_specs.append(pl.BlockSpec((None, 2, 1, LRU_BW), lambda b, n: (b, 0, 0, n)))
        args.append(h0.reshape(bsz, 2, 1, LRU_W))
    else:
        out_specs.append(pl.BlockSpec((None, 2, 1, LRU_BW), lambda b, n: (b, 0, 0, n)))
        out_shape.append(jax.ShapeDtypeStruct((bsz, 2, 1, LRU_W), jnp.float32))
    n_in = len(args)
    aliases = {}
    if out_prev is not None:
        in_specs.append(pl.BlockSpec(memory_space=pl.ANY))
        args.append(out_prev)
        aliases = {n_in: 0}

    def body(*refs):
        if out_prev is not None:
            refs = refs[:n_in] + refs[n_in + 1:]
        _lru_kernel(*refs, n_tok=n_tok, with_state=with_state)

    return pl.pallas_call(
        body,
        grid=(bsz, LRU_BLOCKS),
        in_specs=in_specs,
        out_specs=out_specs,
        out_shape=out_shape,
        input_output_aliases=aliases,
        compiler_params=_params(2, 48 << 20),
        name="rg_lru",
    )(*args)


SSD_L = 128
SSD_GH = SSD_HEADS // SSD_GROUPS
SSD_GW = SSD_GH * SSD_HEADDIM


def _ssd_kernel(*refs, n_tok, with_state):
    if with_state:
        (x_ref, b_ref, c_ref, s_ref, cwx_ref, cbx_ref, cwb_ref, cbb_ref, cwc_ref, cbc_ref, dtb_ref, alog_ref,
         dskip_ref, h0_ref, y_ref, xc_s, xdt_s, bc_s, cc_s, la_s, h_s) = refs
    else:
        (x_ref, b_ref, c_ref, s_ref, cwx_ref, cbx_ref, cwb_ref, cbb_ref, cwc_ref, cbc_ref, dtb_ref, alog_ref,
         dskip_ref, y_ref, hT_ref, xc_s, xdt_s, bc_s, cc_s, la_s, h_s) = refs
    L = SSD_L
    n_chunk = n_tok // L
    g = pl.program_id(1)

    xc_s[...] = _silu(_conv4(x_ref[...], cwx_ref, cbx_ref))
    bc_s[...] = _silu(_conv4(b_ref[...], cwb_ref, cbb_ref)).astype(MXU_DTYPE)
    cc_s[...] = _silu(_conv4(c_ref[...], cwc_ref, cbc_ref)).astype(MXU_DTYPE)
    dt = _softplus(s_ref[...] + dtb_ref[...])
    y_ref[...] = xc_s[...] * dskip_ref[...]

    e8 = (_row_iota((S_W, SSD_GW)) == _lane_iota((S_W, SSD_GW)) // SSD_HEADDIM).astype(MXU_DTYPE)
    ri, ci = _row_iota((L, L)), _lane_iota((L, L))
    lane_w = _lane_iota((L, 2 * SSD_HEADDIM))

    for d in range(2):
        up = d == 1
        sel = (_row_iota((S_W, S_W)) == _lane_iota((S_W, S_W)) + (d * SSD_HEADS + g * SSD_GH)).astype(MXU_DTYPE)
        sel = jnp.where(_lane_iota((S_W, S_W)) < SSD_GH, sel, jnp.zeros_like(sel))
        dt_sel = _dot01_right(dt, sel)
        a_row = -jnp.exp(_dot01_right(alog_ref[...], sel))
        la_s[...] = dt_sel * a_row
        xdt_s[...] = xc_s[...] * _dot01_right(dt_sel, e8)
        if with_state:
            h_s[...] = h0_ref[d].reshape(SSD_GW, SSD_STATE)
        else:
            h_s[...] = jnp.zeros_like(h_s)
        tri = ((ci >= ri) if up else (ci <= ri))
        tri_f = tri.astype(MXU_DTYPE)
        end = 0 if up else L - 1

        def chunk(i, carry):
            c = (n_chunk - 1 - i) if up else i
            t0 = pl.multiple_of(c * L, L)
            rows = pl.ds(t0, L)
            cum = _dot01_left(tri_f, la_s[rows, :])
            cum_t = cum.T
            cum_i = _dot01_right(cum, e8)
            cum_end = cum_i[end:end + 1, :]
            xdt = xdt_s[rows, :]
            bcv, ccv = bc_s[rows, :], cc_s[rows, :]
            cb = _dot_nt(ccv, bcv)
            h_b16 = h_s[...].astype(MXU_DTYPE)
            y = _dot_nt(ccv, h_b16) * jnp.exp(cum_i)
            xdt_b16 = xdt.astype(MXU_DTYPE)
            pieces = []
            for pair in range(SSD_GH // 2):
                xp = xdt_b16[:, pair * 2 * SSD_HEADDIM:(pair + 1) * 2 * SSD_HEADDIM]
                acc = None
                for sub in range(2):
                    hh = 2 * pair + sub
                    seg = jnp.where(tri, jnp.exp(cum[:, hh:hh + 1] - cum_t[hh:hh + 1, :]), 0.0)
                    sc = (cb * seg).astype(MXU_DTYPE)
                    half = (lane_w // SSD_HEADDIM) == sub
                    part = _dot(sc, jnp.where(half, xp, jnp.zeros_like(xp)))
                    acc = part if acc is None else acc + part
                pieces.append(acc)
            y = y + jnp.concatenate(pieces, axis=1)
            y_ref[rows, :] += y
            upd = _dot_tn((xdt * jnp.exp(cum_end - cum_i)).astype(MXU_DTYPE), bcv)
            for hh in range(SSD_GH):
                blk = slice(hh * SSD_HEADDIM, (hh + 1) * SSD_HEADDIM)
                h_s[blk, :] = h_s[blk, :] * jnp.exp(cum_t[hh:hh + 1, end:end + 1]) + upd[blk, :]
            return carry

        lax.fori_loop(0, n_chunk, chunk, 0, unroll=2)
        if not with_state:
            hT_ref[d] = h_s[...].reshape(SSD_GH, SSD_HEADDIM, SSD_STATE)


def _ssd_call(p, s, row_off, bsz, n_tok, lp, h0):
    with_state = h0 is not None
    sb = row_off // n_tok
    assert row_off % n_tok == 0 and n_tok % SSD_L == 0

    def seq_spec(col0, w):
        return pl.BlockSpec((n_tok, w), lambda b, g: (sb + b, col0 // w + g))

    def conv_specs(col0, w):
        return [pl.BlockSpec((CONV_W, w), lambda b, g: (0, col0 // w + g)),
                pl.BlockSpec((1, w), lambda b, g: (0, col0 // w + g))]

    small = pl.BlockSpec((1, S_W), lambda b, g: (0, 0))
    pad = jnp.zeros((S_W - 2 * SSD_HEADS,), jnp.float32)
    dtb = jnp.concatenate([lp['ssd_dt_bias'].reshape(-1), pad]).reshape(1, S_W)
    alog = jnp.concatenate([lp['ssd_a_log'].reshape(-1), pad]).reshape(1, S_W)
    dskip = jnp.repeat(lp['ssd_d'][0] + lp['ssd_d'][1], SSD_HEADDIM).reshape(1, GROUP_W)
    cw, cb = lp['ssd_conv_w'], lp['ssd_conv_b'].reshape(1, -1)
    in_specs = ([seq_spec(P_SX, SSD_GW), seq_spec(P_SB, SSD_STATE), seq_spec(P_SC, SSD_STATE),
                 pl.BlockSpec((n_tok, S_W), lambda b, g: (sb + b, 0))]
                + conv_specs(0, SSD_GW) + conv_specs(GROUP_W, SSD_STATE)
                + conv_specs(GROUP_W + SSD_GROUPS * SSD_STATE, SSD_STATE)
                + [small, small, pl.BlockSpec((1, SSD_GW), lambda b, g: (0, g))])
    args = [p, p, p, s, cw, cb, cw, cb, cw, cb, dtb, alog, dskip]
    state_block = pl.BlockSpec((None, 2, SSD_GH, SSD_HEADDIM, SSD_STATE), lambda b, g: (b, 0, g, 0, 0))
    out_specs = [pl.BlockSpec((n_tok, SSD_GW), lambda b, g: (b, g))]
    out_shape = [jax.ShapeDtypeStruct((bsz * n_tok, GROUP_W), jnp.float32)]
    if with_state:
        in_specs.append(state_block)
        args.append(h0)
    else:
        out_specs.append(state_block)
        out_shape.append(jax.ShapeDtypeStruct((bsz, 2, SSD_HEADS, SSD_HEADDIM, SSD_STATE), jnp.float32))
    return pl.pallas_call(
        functools.partial(_ssd_kernel, n_tok=n_tok, with_state=with_state),
        grid=(bsz, SSD_GROUPS),
        in_specs=in_specs,
        out_specs=out_specs,
        out_shape=out_shape,
        scratch_shapes=[pltpu.VMEM((n_tok, SSD_GW), jnp.float32), pltpu.VMEM((n_tok, SSD_GW), jnp.float32),
                        pltpu.VMEM((n_tok, SSD_STATE), MXU_DTYPE), pltpu.VMEM((n_tok, SSD_STATE), MXU_DTYPE),
                        pltpu.VMEM((n_tok, S_W), jnp.float32),
                        pltpu.VMEM((SSD_GW, SSD_STATE), jnp.float32)],
        compiler_params=_params(2, VMEM_LIMIT_V7X),
        name="ssd_scan",
    )(*args)


def _gated_norm_kernel(y_ref, z_ref, w_ref, o_ref):
    o_ref[...] = _rms_rows(y_ref[...] * _silu(z_ref[...]), w_ref[...]).astype(o_ref.dtype)


def _ssd_finish(y, p, row_off, norm_w, out_prev):
    tm = 256
    m, n = p.shape[0], y.shape[0]
    rb = row_off // tm
    in_specs = [pl.BlockSpec((tm, GROUP_W), lambda i: (i, 0)),
                pl.BlockSpec((tm, GROUP_W), lambda i: (rb + i, P_SZ // GROUP_W)),
                pl.BlockSpec((1, GROUP_W), lambda i: (0, 0))]
    args = [y, p, norm_w.reshape(1, GROUP_W)]
    aliases = {}
    if out_prev is not None:
        in_specs.append(pl.BlockSpec(memory_space=pl.ANY))
        args.append(out_prev)
        aliases = {3: 0}

    def body(y_ref, z_ref, w_ref, *rest):
        _gated_norm_kernel(y_ref, z_ref, w_ref, rest[-1])

    return pl.pallas_call(
        body,
        grid=(n // tm,),
        in_specs=in_specs,
        out_specs=pl.BlockSpec((tm, GROUP_W), lambda i: (rb + i, 0)),
        out_shape=jax.ShapeDtypeStruct((m, GROUP_W), MXU_DTYPE),
        input_output_aliases=aliases,
        compiler_params=_params(1, 32 << 20),
        name="ssd_gated_norm",
    )(*args)


GLA_L = 16
GLA_GROUP = 16
LOG2_E = 1.4426950408889634


def _gla_kernel(*refs, n_tok, with_state):
    if with_state:
        (q_ref, k_ref, v_ref, gg_ref, s_ref, gw_ref, gb_ref, nw_ref, s0_ref, o_ref,
         dec_s, qe_s, ke_s, att_s, o_s, st_s) = refs
    else:
        (q_ref, k_ref, v_ref, gg_ref, s_ref, gw_ref, gb_ref, nw_ref, o_ref, sT_ref,
         dec_s, qe_s, ke_s, att_s, o_s, st_s) = refs
    L = GLA_L
    n_blk = n_tok // L
    blk3 = (n_blk, L, GLA_DK)
    q3 = (q_ref[...] * (GLA_DK ** -0.5)).reshape(blk3)
    k3 = k_ref[...].reshape(blk3)
    sb16 = s_ref[...].astype(MXU_DTYPE)
    t = _row_iota((n_tok, GLA_DK))
    ri3 = lax.broadcasted_iota(jnp.int32, blk3, 1)
    lane3 = lax.broadcasted_iota(jnp.int32, blk3, 2)

    for d in range(2):
        up = d == 1
        end = 0 if up else L - 1
        gate = _dot(sb16, gw_ref[d]) + gb_ref[d]
        b = -_softplus(-gate) / GLA_NORMALIZER
        s = 1
        while s < L:
            ok = (t % L < L - s) if up else (t % L >= s)
            b = b + jnp.where(ok, pltpu.roll(b, (n_tok - s) if up else s, 0), 0.0)
            s *= 2
        b3 = b.reshape(blk3)
        b_end = b3[:, end:end + 1, :]
        dec_s[d] = jnp.exp(jnp.broadcast_to(b_end, blk3)).reshape(n_tok, GLA_DK)
        qe_s[d] = (q3 * jnp.exp(b3)).reshape(n_tok, GLA_DK).astype(MXU_DTYPE)
        ke_s[d] = (k3 * jnp.exp(b_end - b3)).reshape(n_tok, GLA_DK).astype(MXU_DTYPE)
        code = jnp.where((ri3 <= lane3) if up else (ri3 >= lane3), lane3, -1)
        b2 = b3 * LOG2_E
        half = L // 2
        att_h = [jnp.zeros((n_blk, half, GLA_DK), jnp.float32) for _ in range(2)]
        for j in range(L):
            for h in range(2):
                if (h == 0 and j >= half) if not up else (h == 1 and j < half):
                    continue
                rows = slice(h * half, (h + 1) * half)
                w = jnp.exp2(b2[:, rows, :] - b2[:, j:j + 1, :]) * q3[:, rows, :] * k3[:, j:j + 1, :]
                att_h[h] = jnp.where(code[:, rows, :] == j, jnp.sum(w, axis=-1, keepdims=True), att_h[h])
        att = jnp.concatenate(att_h, axis=1)
        att_s[d] = att.reshape(n_tok, GLA_DK).astype(MXU_DTYPE)
        if with_state:
            st_s[d] = s0_ref[d].T
        else:
            st_s[d] = jnp.zeros((GLA_DV, GLA_DK), jnp.float32)

    o_s[...] = jnp.zeros_like(o_s)
    n_grp = n_blk // GLA_GROUP
    rows_per_trip = GLA_GROUP * L

    def group(i, carry):
        for d in range(2):
            up = d == 1
            c = (n_grp - 1 - i) if up else i
            rows = pl.ds(pl.multiple_of(c * rows_per_trip, rows_per_trip), rows_per_trip)
            vg = v_ref[rows, :].astype(MXU_DTYPE)
            keg, attg, qeg, decg = ke_s[d, rows, :], att_s[d, rows, 0:L], qe_s[d, rows, :], dec_s[d, rows, :]
            order = range(GLA_GROUP - 1, -1, -1) if up else range(GLA_GROUP)
            upd = {k: _dot_tn(vg[k * L:(k + 1) * L], keg[k * L:(k + 1) * L]) for k in order}
            intra = {k: _dot(attg[k * L:(k + 1) * L], vg[k * L:(k + 1) * L]) for k in order}
            st = st_s[d]
            outs = {}
            for k in order:
                outs[k] = _dot_nt(qeg[k * L:(k + 1) * L], st.astype(MXU_DTYPE)) + intra[k]
                st = st * decg[k * L:k * L + 1, :] + upd[k]
            st_s[d] = st
            o_s[rows, :] += jnp.concatenate([outs[k] for k in range(GLA_GROUP)], axis=0)
        return carry

    lax.fori_loop(0, n_grp, group, 0)
    if not with_state:
        for d in range(2):
            sT_ref[d] = st_s[d].T
    o = _rms_rows(o_s[...], nw_ref[...]) * _silu(gg_ref[...])
    o_ref[...] = o.astype(o_ref.dtype)


def _gla_call(p, s, row_off, bsz, n_tok, lp, s0, out_prev):
    m = p.shape[0]
    with_state = s0 is not None
    sb = row_off // n_tok
    assert row_off % n_tok == 0 and n_tok % (GLA_L * GLA_GROUP) == 0

    def seq_spec(col0, w):
        return pl.BlockSpec((n_tok, w), lambda b, h: (sb + b, col0 // w + h))

    gw = lp['gla_gate_w'].reshape(2, GLA_RANK, GLA_HEADS, GLA_DK).transpose(0, 2, 1, 3)
    gw_rows = jnp.zeros((2, GLA_HEADS, S_W, GLA_DK), jnp.float32)
    for d in range(2):
        r0 = 2 * SSD_HEADS + d * GLA_RANK
        gw_rows = gw_rows.at[d, :, r0:r0 + GLA_RANK, :].set(gw[d])
    in_specs = [seq_spec(P_GQ, GLA_DK), seq_spec(P_GK, GLA_DK), seq_spec(P_GV, GLA_DV), seq_spec(P_GG, GLA_DV),
                pl.BlockSpec((n_tok, S_W), lambda b, h: (sb + b, 0)),
                pl.BlockSpec((2, None, S_W, GLA_DK), lambda b, h: (0, h, 0, 0)),
                pl.BlockSpec((2, None, 1, GLA_DK), lambda b, h: (0, h, 0, 0)),
                pl.BlockSpec((1, GLA_DV), lambda b, h: (0, 0))]
    args = [p, p, p, p, s, gw_rows.astype(MXU_DTYPE), lp['gla_gate_b'].reshape(2, GLA_HEADS, 1, GLA_DK),
            lp['gla_norm_w'].reshape(1, GLA_DV)]
    state_block = pl.BlockSpec((None, 2, None, GLA_DK, GLA_DV), lambda b, h: (b, 0, h, 0, 0))
    out_specs = [pl.BlockSpec((n_tok, GLA_DV), lambda b, h: (sb + b, h))]
    out_shape = [jax.ShapeDtypeStruct((m, GROUP_W), MXU_DTYPE)]
    if with_state:
        in_specs.append(state_block)
        args.append(s0)
    else:
        out_specs.append(state_block)
        out_shape.append(jax.ShapeDtypeStruct((bsz, 2, GLA_HEADS, GLA_DK, GLA_DV), jnp.float32))
    n_in = len(args)
    aliases = {}
    if out_prev is not None:
        in_specs.append(pl.BlockSpec(memory_space=pl.ANY))
        args.append(out_prev)
        aliases = {n_in: 0}

    def body(*refs):
        if out_prev is not None:
            refs = refs[:n_in] + refs[n_in + 1:]
        _gla_kernel(*refs, n_tok=n_tok, with_state=with_state)

    return pl.pallas_call(
        body,
        grid=(bsz, GLA_HEADS),
        in_specs=in_specs,
        out_specs=out_specs,
        out_shape=out_shape,
        scratch_shapes=[pltpu.VMEM((2, n_tok, GLA_DK), jnp.float32), pltpu.VMEM((2, n_tok, GLA_DK), MXU_DTYPE),
                        pltpu.VMEM((2, n_tok, GLA_DK), MXU_DTYPE), pltpu.VMEM((2, n_tok, GLA_DK), MXU_DTYPE),
                        pltpu.VMEM((n_tok, GLA_DV), jnp.float32), pltpu.VMEM((2, GLA_DV, GLA_DK), jnp.float32)],
        input_output_aliases=aliases,
        compiler_params=_params(2, 48 << 20),
        name="gla_scan",
    )(*args)


def _mixers(p, s, lp, cached, donors):
    ctx_k, ctx_v, ssd0, gla0, lru0 = cached
    kv_w = ATT_KV_HEADS * HEAD_DIM
    att, k_new = _attention_call(p, 0, BATCH, SEQ, lp['q_norm'], lp['k_norm'], None, donors[0])
    att, = _attention_call(p, M_PROMPT, DEC_BATCH, DEC_SEQ, lp['q_norm'], lp['k_norm'],
                           (ctx_k.reshape(DEC_BATCH, PAST_LEN, kv_w), ctx_v.reshape(DEC_BATCH, PAST_LEN, kv_w)), att)
    v_new = p[:M_PROMPT, P_AV:P_AV + kv_w]

    y_p, ssd_new = _ssd_call(p, s, 0, BATCH, SEQ, lp, None)
    y_s, = _ssd_call(p, s, M_PROMPT, DEC_BATCH, DEC_SEQ, lp, ssd0)
    ssd = _ssd_finish(y_p, p, 0, lp['ssd_norm_w'], donors[1])
    ssd = _ssd_finish(y_s, p, M_PROMPT, lp['ssd_norm_w'], ssd)

    gla, gla_new = _gla_call(p, s, 0, BATCH, SEQ, lp, None, donors[2])
    gla, = _gla_call(p, s, M_PROMPT, DEC_BATCH, DEC_SEQ, lp, gla0, gla)

    lru, lru_new = _lru_call(p, 0, BATCH, SEQ, lp, None, donors[3])
    lru, = _lru_call(p, M_PROMPT, DEC_BATCH, DEC_SEQ, lp, lru0, lru)

    new_ctx = (k_new.reshape(BATCH, SEQ, ATT_KV_HEADS, HEAD_DIM), v_new.reshape(BATCH, SEQ, ATT_KV_HEADS, HEAD_DIM),
               ssd_new, gla_new, lru_new.reshape(BATCH, 2, LRU_W))
    return [att, ssd, gla, lru], new_ctx


IN_W = 9280
_W_IN_MAIN = ((0, P_AQ, 1024), (1024, P_AK, 256), (1280, P_AV, 256), (1536, P_SX, 1024), (2560, P_SZ, 1024),
              (3584, P_SB, 256), (3840, P_SC, 256), (4128, P_GQ, 512), (4640, P_GK, 512), (5152, P_GV, 1024),
              (6208, P_GG, 1024), (7232, P_LX, 1024), (8256, P_LG, 1024))
_W_IN_SMALL = ((4096, 0, 2 * SSD_HEADS), (6176, 2 * SSD_HEADS, 2 * GLA_RANK))


def _w_in_kernel(w_ref, main_ref, small_ref):
    for src, dst, width in _W_IN_MAIN:
        main_ref[:, dst:dst + width] = w_ref[:, src:src + width].astype(main_ref.dtype)
    small_ref[...] = jnp.zeros_like(small_ref)
    for src, dst, width in _W_IN_SMALL:
        small_ref[:, dst:dst + width] = w_ref[:, src:src + width].astype(small_ref.dtype)


def _reorder_w_in(w, l):
    tk = 256
    return pl.pallas_call(
        _w_in_kernel,
        grid=(D_MODEL // tk,),
        in_specs=[pl.BlockSpec((None, tk, IN_W), lambda i: (l, i, 0))],
        out_specs=[pl.BlockSpec((tk, P_W), lambda i: (i, 0)), pl.BlockSpec((tk, S_W), lambda i: (i, 0))],
        out_shape=[jax.ShapeDtypeStruct((D_MODEL, P_W), MXU_DTYPE), jax.ShapeDtypeStruct((D_MODEL, S_W), MXU_DTYPE)],
        compiler_params=_params(1, 48 << 20),
        name="w_in_regroup",
    )(w)


def kernel(x_prompt, x_sample, c, cache_attn_k, cache_attn_v, state_ssd, state_gla, state_lru, c_ctx,
           mod_w, mod_b, ln_g, ln_b, ffn_w_gate, ffn_w_up, ffn_w_down, w_in, w_out, q_norm, k_norm,
           ssd_conv_w, ssd_conv_b, ssd_a_log, ssd_dt_bias, ssd_d, ssd_norm_w,
           gla_gate_w, gla_gate_b, gla_norm_w,
           lru_conv_w, lru_conv_b, lru_wa, lru_ba, lru_wx, lru_bx, lru_lam):
    cond = jnp.concatenate([c_ctx[None], c, jnp.zeros((COND_ROWS - N_COND, D_MODEL), jnp.float32)], axis=0)
    mod_all = _mod_table(cond, mod_w, mod_b)

    def ffn(x, xm, mod, k_gate, l, half):
        h = _dense_cast_first([xm], [ffn_w_gate, ffn_w_up], (l, half), spare['h'], tm_first=2048, tn_first=256,
                              tm=2048, tn=256, out_dtype=MXU_DTYPE, name="ffn_up", swiglu=True)
        y = _dense_cast_first([h], [ffn_w_down], (l, half), spare['y'], tm_first=512, tn_first=256, tm=512, tn=512,
                              out_dtype=jnp.float32, name="ffn_down", resid=(x, mod, k_gate, 0.5))
        spare['h'] = h
        return y

    spare = dict(h=jnp.zeros((M_TOK, D_FF), MXU_DTYPE), y=jnp.zeros((M_TOK, D_MODEL), jnp.float32),
                 mix=[jnp.zeros((M_TOK, GROUP_W), MXU_DTYPE)] * 4)
    x, xm = _modulate(x_prompt, x_sample, mod_all[0], 1, 0)
    ctx_out = []
    for l in range(DEPTH):
        mod = mod_all[l]
        lp = dict(q_norm=q_norm[l], k_norm=k_norm[l], ssd_conv_w=ssd_conv_w[l], ssd_conv_b=ssd_conv_b[l],
                  ssd_a_log=ssd_a_log[l], ssd_dt_bias=ssd_dt_bias[l], ssd_d=ssd_d[l], ssd_norm_w=ssd_norm_w[l],
                  gla_gate_w=gla_gate_w[l], gla_gate_b=gla_gate_b[l], gla_norm_w=gla_norm_w[l],
                  lru_conv_w=lru_conv_w[l], lru_conv_b=lru_conv_b[l], lru_wa=lru_wa[l], lru_ba=lru_ba[l],
                  lru_wx=lru_wx[l], lru_bx=lru_bx[l], lru_lam=lru_lam[l])
        y = ffn(x, xm, mod, 2, l, 0)
        x, xm = _layer_norm(y, ln_g[l, 0], ln_b[l, 0], mod, 4, 3)
        spare['y'] = y
        w_main, w_small = _reorder_w_in(w_in, l)
        dense = dict(tile0=0, n_tiles=M_TOK // 1024, tm=1024, out_dtype=jnp.float32)
        p = _dense([xm], [w_main], (), tn=1024, name="proj_in", **dense)
        s = _dense([xm], [w_small], (), tn=S_W, name="proj_in_small", **dense)
        cached = (cache_attn_k[:, l], cache_attn_v[:, l], state_ssd[:, l], state_gla[:, l], state_lru[:, l])
        mix, new_ctx = _mixers(p, s, lp, cached, spare['mix'])
        ctx_out.append(new_ctx)
        y = _dense_cast_first(mix, [w_out], (l,), spare['y'], tm_first=1024, tn_first=512, tm=1024, tn=1024,
                              out_dtype=jnp.float32, name="proj_out", resid=(x, mod, 5, 1.0))
        spare['mix'] = mix
        x, xm = _layer_norm(y, ln_g[l, 1], ln_b[l, 1], mod, 7, 6)
        spare['y'] = y
        y = ffn(x, xm, mod, 8, l, 1)
        if l + 1 < DEPTH:
            x, xm = _layer_norm(y, ln_g[l, 2], ln_b[l, 2], mod_all[l + 1], 1, 0)
            spare['y'] = y
        else:
            y_prompt, y_sample = _layer_norm(y, ln_g[l, 2], ln_b[l, 2])

    y_prompt = y_prompt.reshape(BATCH, SEQ, D_MODEL)
    y_sample = y_sample.reshape(DEC_BATCH, DEC_SEQ, D_MODEL)
    new_k, new_v, new_ssd, new_gla, new_lru = (jnp.stack([s_[i] for s_ in ctx_out], axis=1) for i in range(5))
    return (y_prompt, y_sample, new_k, new_v, new_ssd, new_gla, new_lru)
```

```python
import functools
import math

import numpy as np
import jax
import jax.numpy as jnp
from jax import lax
from jax.experimental import pallas as pl
from jax.experimental.pallas import tpu as pltpu

D_MODEL = 4096
BATCH = 16
SEQ = 256
DEPTH = 2
DEC_BATCH = 4
DEC_SEQ = 2048
PAST_LEN = 256
GRID_W = 64
GROUP_W = 1024
HEAD_DIM = 128
ATT_HEADS = 8
ATT_KV_HEADS = 2
ROPE_THETA = 10000.0
SSD_HEADDIM = 64
SSD_HEADS = 16
SSD_STATE = 128
SSD_GROUPS = 2
CONV_W = 4
GLA_HEADS = 4
GLA_DK = 128
GLA_DV = 256
GLA_RANK = 16
GLA_NORMALIZER = 16.0
LRU_W = 1024
LRU_BLOCKS = 8
LRU_BW = 128
LRU_C = 8.0
D_FF = 11008
N_MOD = 9
LN_EPS = 1e-5
RMS_EPS = 1e-6
ALPHA = (2.0 * DEPTH) ** 0.25

M_PROMPT = BATCH * SEQ
M_SAMPLE = DEC_BATCH * DEC_SEQ
M_TOK = M_PROMPT + M_SAMPLE
N_COND = 1 + DEC_BATCH
COND_ROWS = 8

VMEM_LIMIT_V7X = 56 * 1024 * 1024

MXU_DTYPE = jnp.bfloat16

P_AQ, P_SX, P_SZ, P_GV, P_GG, P_LX, P_LG = 0, 1024, 2048, 3072, 4096, 5120, 6144
P_GQ, P_GK = 7168, 7680
P_AK, P_AV, P_SB, P_SC = 8192, 8448, 8704, 8960
P_W = 9216
S_W = 128


def _params(n_axes, vmem_bytes):
    return pltpu.CompilerParams(dimension_semantics=("arbitrary",) * n_axes,
                                vmem_limit_bytes=min(int(vmem_bytes), VMEM_LIMIT_V7X))


def _cond_row(tile_idx, tile_rows):
    row0 = tile_idx * tile_rows
    return jnp.where(row0 < M_PROMPT, 0, 1 + (row0 - M_PROMPT) // DEC_SEQ)


def _mod_kernel(c_ref, w_ref, b_ref, o_ref):
    c = c_ref[...]
    a = (c * jax.nn.sigmoid(c)).astype(MXU_DTYPE)
    w = w_ref[...].astype(MXU_DTYPE)
    o_ref[...] = jnp.dot(a, w, preferred_element_type=jnp.float32) + b_ref[...]


def _mod_table(cond, mod_w, mod_b):
    tn = 512
    n = N_MOD * D_MODEL
    return pl.pallas_call(
        _mod_kernel,
        grid=(DEPTH, n // tn),
        in_specs=[pl.BlockSpec((COND_ROWS, D_MODEL), lambda l, j: (0, 0)),
                  pl.BlockSpec((None, D_MODEL, tn), lambda l, j: (l, 0, j)),
                  pl.BlockSpec((None, 1, tn), lambda l, j: (l, 0, j))],
        out_specs=pl.BlockSpec((None, COND_ROWS, tn), lambda l, j: (l, 0, j)),
        out_shape=jax.ShapeDtypeStruct((DEPTH, COND_ROWS, n), jnp.float32),
        compiler_params=_params(2, 3 * D_MODEL * tn * 4 + (4 << 20)),
        name="mod_table",
    )(cond, mod_w, mod_b.reshape(DEPTH, 1, n))


def _dense_kernel(*refs, n_a, n_w, cast, swiglu, resid, w_t):
    a_refs, w_refs = refs[:n_a], refs[n_a:n_a + n_w]
    n_in = n_a + n_w + (2 if resid else 0)
    o_ref = refs[n_in]
    cast_refs = refs[n_in + 1:n_in + n_w + 1] if cast else ()
    accs = []
    for wi, w_ref in enumerate(w_refs):
        if cast:
            w_narrow = w_ref[...].astype(MXU_DTYPE)
            cast_refs[wi][...] = w_narrow
        k0, acc = 0, None
        for a_ref in a_refs:
            kp = a_ref.shape[1]
            if w_t:
                part = _dot_nt(a_ref[...], w_ref[:, k0:k0 + kp])
            else:
                w_rows = w_narrow[k0:k0 + kp, :] if cast else w_ref[k0:k0 + kp, :]
                part = jnp.dot(a_ref[...], w_rows, preferred_element_type=jnp.float32)
            acc = part if acc is None else acc + part
            k0 += kp
        accs.append(acc)
    out = accs[0] * jax.nn.sigmoid(accs[0]) * accs[1] if swiglu else accs[0]
    if resid:
        x_ref, g_ref = refs[n_a + n_w], refs[n_a + n_w + 1]
        tile0, tm, gate_scale = resid
        g = g_ref[pl.ds(_cond_row(tile0 + pl.program_id(0), tm), 1), :]
        out = ALPHA * x_ref[...] + (gate_scale * g) * out
    o_ref[...] = out.astype(o_ref.dtype)


def _dense(a_list, w_list, w_index, *, tile0, n_tiles, tm, tn, out_dtype, name, swiglu=False, cast=False,
           out_prev=None, resid=None, w_t=False):
    m = a_list[0].shape[0]
    k, n = w_list[0].shape[-2:][::-1] if w_t else w_list[0].shape[-2:]
    n_a, n_w, lead = len(a_list), len(w_list), len(w_index)
    assert sum(a.shape[1] for a in a_list) == k and n % tn == 0 and (tile0 + n_tiles) * tm <= m
    assert not (w_t and cast)
    a_mode = dict(pipeline_mode=pl.Buffered(1)) if n_tiles == 1 else {}
    in_specs = [pl.BlockSpec((tm, a.shape[1]), lambda i, j: (tile0 + i, 0), **a_mode) for a in a_list]
    if w_t:
        in_specs += [pl.BlockSpec((None,) * lead + (tn, k), lambda i, j: tuple(w_index) + (j, 0)) for _ in w_list]
    else:
        in_specs += [pl.BlockSpec((None,) * lead + (k, tn), lambda i, j: tuple(w_index) + (0, j)) for _ in w_list]
    args = list(a_list) + list(w_list)
    if resid is not None:
        x, mod, k_gate, gate_scale = resid
        in_specs += [pl.BlockSpec((tm, tn), lambda i, j: (tile0 + i, j)),
                     pl.BlockSpec((COND_ROWS, tn), lambda i, j: (0, k_gate * (n // tn) + j))]
        args += [x, mod]
    n_in = len(args)
    out_specs = [pl.BlockSpec((tm, tn), lambda i, j: (tile0 + i, j))]
    out_shape = [jax.ShapeDtypeStruct((m, n), out_dtype)]
    if cast:
        out_specs += [pl.BlockSpec((k, tn), lambda i, j: (0, j)) for _ in w_list]
        out_shape += [jax.ShapeDtypeStruct((k, n), MXU_DTYPE) for _ in w_list]
    aliases = {}
    if out_prev is not None:
        in_specs.append(pl.BlockSpec(memory_space=pl.ANY))
        args.append(out_prev)
        aliases = {n_in: 0}
    esz, wsz, osz = jnp.dtype(MXU_DTYPE).itemsize, w_list[0].dtype.itemsize, jnp.dtype(out_dtype).itemsize
    vmem = ((1 if n_tiles == 1 else 2) * tm * k * esz + 2 * n_w * k * tn * wsz + 2 * tm * tn * osz
            + (n_w + 1) * tm * tn * 4 + (3 * n_w * k * tn * esz if cast else 0)
            + (2 * tm * tn * 4 if resid is not None else 0))

    def body(*refs):
        if out_prev is not None:
            refs = refs[:n_in] + refs[n_in + 1:]
        _dense_kernel(*refs, n_a=n_a, n_w=n_w, cast=cast, swiglu=swiglu,
                      resid=None if resid is None else (tile0, tm, resid[3]), w_t=w_t)

    out = pl.pallas_call(
        body,
        grid=(n_tiles, n // tn),
        in_specs=in_specs,
        out_specs=out_specs,
        out_shape=out_shape,
        input_output_aliases=aliases,
        compiler_params=_params(2, vmem + (4 << 20)),
        name=name,
    )(*args)
    return out if cast else out[0]


def _dense_cast_first(a_list, w_list, w_index, donor, *, tm_first, tn_first, tm, tn, out_dtype, name, swiglu=False,
                      resid=None):
    m = a_list[0].shape[0]
    first = _dense(a_list, w_list, w_index, tile0=0, n_tiles=1, tm=tm_first, tn=tn_first, out_dtype=out_dtype,
                   name=name + "_first", swiglu=swiglu, cast=True, out_prev=donor, resid=resid)
    assert tm_first % tm == 0 or tm_first == tm
    return _dense(a_list, first[1:], (), tile0=tm_first // tm, n_tiles=(m - tm_first) // tm, tm=tm, tn=tn,
                  out_dtype=out_dtype, name=name, swiglu=swiglu, out_prev=first[0], resid=resid)


def _modulate_kernel(xp_ref, xs_ref, sc_ref, sh_ref, x_ref, o_ref, *, tm):
    i = pl.program_id(0)
    r = _cond_row(i, tm)
    sc = sc_ref[pl.ds(r, 1), :]
    sh = sh_ref[pl.ds(r, 1), :]

    def emit(x):
        x_ref[...] = x
        o_ref[...] = (x * (1.0 + sc) + sh).astype(o_ref.dtype)

    pl.when(i < M_PROMPT // tm)(lambda: emit(xp_ref[...]))
    pl.when(i >= M_PROMPT // tm)(lambda: emit(xs_ref[...]))


def _mod_spec(k):
    return pl.BlockSpec((COND_ROWS, D_MODEL), lambda i: (0, k))


def _modulate(x_prompt, x_sample, mod, k_scale, k_shift):
    tm = 256
    n_p = M_PROMPT // tm
    row = pl.BlockSpec((tm, D_MODEL), lambda i: (i, 0))
    return pl.pallas_call(
        functools.partial(_modulate_kernel, tm=tm),
        grid=(M_TOK // tm,),
        in_specs=[pl.BlockSpec((tm, D_MODEL), lambda i: (jnp.minimum(i, n_p - 1), 0)),
                  pl.BlockSpec((tm, D_MODEL), lambda i: (jnp.maximum(i - n_p, 0), 0)),
                  _mod_spec(k_scale), _mod_spec(k_shift)],
        out_specs=[row, row],
        out_shape=[jax.ShapeDtypeStruct((M_TOK, D_MODEL), jnp.float32),
                   jax.ShapeDtypeStruct((M_TOK, D_MODEL), MXU_DTYPE)],
        compiler_params=_params(1, 32 << 20),
        name="modulate",
    )(x_prompt.reshape(M_PROMPT, D_MODEL), x_sample.reshape(M_SAMPLE, D_MODEL), mod, mod)


def _ln_kernel(*refs, tm, with_next):
    if with_next:
        y_ref, lg_ref, lb_ref, sc_ref, sh_ref, xo_ref, mo_ref = refs
    else:
        y_ref, lg_ref, lb_ref, yp_ref, ys_ref = refs
    i = pl.program_id(0)
    r = _cond_row(i, tm)
    y = y_ref[...]
    mu = jnp.mean(y, axis=-1, keepdims=True)
    yc = y - mu
    var = jnp.mean(yc * yc, axis=-1, keepdims=True)
    xn = yc * lax.rsqrt(var + LN_EPS) * lg_ref[...] + lb_ref[...]
    if with_next:
        xo_ref[...] = xn
        sc = sc_ref[pl.ds(r, 1), :]
        sh = sh_ref[pl.ds(r, 1), :]
        mo_ref[...] = (xn * (1.0 + sc) + sh).astype(mo_ref.dtype)
    else:
        @pl.when(i < M_PROMPT // tm)
        def _():
            yp_ref[...] = xn

        @pl.when(i >= M_PROMPT // tm)
        def _():
            ys_ref[...] = xn


def _layer_norm(y, ln_g, ln_b, next_mod=None, k_scale=0, k_shift=0):
    tm = 256
    m = y.shape[0]
    n_p = M_PROMPT // tm
    with_next = next_mod is not None
    row = pl.BlockSpec((tm, D_MODEL), lambda i: (i, 0))
    vec = pl.BlockSpec((1, D_MODEL), lambda i: (0, 0))
    in_specs = [row, vec, vec]
    args = [y, ln_g.reshape(1, D_MODEL), ln_b.reshape(1, D_MODEL)]
    if with_next:
        in_specs += [_mod_spec(k_scale), _mod_spec(k_shift)]
        args += [next_mod, next_mod]
        out_specs = [row, row]
        out_shape = [jax.ShapeDtypeStruct((m, D_MODEL), jnp.float32), jax.ShapeDtypeStruct((m, D_MODEL), MXU_DTYPE)]
    else:
        out_specs = [pl.BlockSpec((tm, D_MODEL), lambda i: (jnp.minimum(i, n_p - 1), 0)),
                     pl.BlockSpec((tm, D_MODEL), lambda i: (jnp.maximum(i - n_p, 0), 0))]
        out_shape = [jax.ShapeDtypeStruct((M_PROMPT, D_MODEL), jnp.float32),
                     jax.ShapeDtypeStruct((M_SAMPLE, D_MODEL), jnp.float32)]
    out = pl.pallas_call(
        functools.partial(_ln_kernel, tm=tm, with_next=with_next),
        grid=(m // tm,),
        in_specs=in_specs,
        out_specs=out_specs,
        out_shape=out_shape,
        compiler_params=_params(1, 48 << 20),
        name="layer_norm",
    )(*args)
    return out[0], out[1]


def _row_iota(shape):
    return lax.broadcasted_iota(jnp.int32, shape, 0)


def _lane_iota(shape):
    return lax.broadcasted_iota(jnp.int32, shape, 1)


def _shift_rows(x, s, fill, up=False):
    n = x.shape[0]
    t = _row_iota(x.shape)
    if up:
        return jnp.where(t < n - s, pltpu.roll(x, n - s, 0), fill)
    return jnp.where(t >= s, pltpu.roll(x, s, 0), fill)


def _softplus(z):
    return jnp.maximum(z, 0.0) + jnp.log1p(jnp.exp(-jnp.abs(z)))


def _silu(z):
    return z * jax.nn.sigmoid(z)


def _dot(a, b):
    return jnp.dot(a, b, preferred_element_type=jnp.float32)


def _dot_nt(a, b):
    return lax.dot_general(a, b, (((1,), (1,)), ((), ())), preferred_element_type=jnp.float32)


def _dot_tn(a, b):
    return lax.dot_general(a, b, (((0,), (0,)), ((), ())), preferred_element_type=jnp.float32)


def _split3(x):
    hi = x.astype(MXU_DTYPE)
    r1 = x - hi.astype(jnp.float32)
    mid = r1.astype(MXU_DTYPE)
    lo = (r1 - mid.astype(jnp.float32)).astype(MXU_DTYPE)
    return hi, mid, lo


def _dot01_right(x, e):
    hi, mid, lo = _split3(x)
    return _dot(hi, e) + _dot(mid, e) + _dot(lo, e)


def _dot01_left(e, x):
    hi, mid, lo = _split3(x)
    return _dot(e, hi) + _dot(e, mid) + _dot(e, lo)


def _conv4(x, w_ref, b_ref):
    acc = _shift_rows(x, 2, 0.0) * w_ref[0:1, :]
    acc = acc + _shift_rows(x, 1, 0.0) * w_ref[1:2, :]
    acc = acc + x * w_ref[2:3, :]
    acc = acc + _shift_rows(x, 1, 0.0, up=True) * w_ref[3:4, :]
    return acc + b_ref[...]


def _rms_rows(x, w):
    return x * lax.rsqrt(jnp.mean(x * x, axis=-1, keepdims=True) + RMS_EPS) * w


ATT_STACK = 2


def _rope_tables(n_tok):
    rows = n_tok // GRID_W
    row = jnp.repeat(jnp.arange(rows, dtype=jnp.float32), GRID_W)
    col = jnp.tile(jnp.arange(GRID_W, dtype=jnp.float32), rows)
    n_freq = HEAD_DIM // 4
    inv = ROPE_THETA ** (-jnp.arange(n_freq, dtype=jnp.float32) / n_freq)
    ar, ac = row[:, None] * inv, col[:, None] * inv
    cos = jnp.concatenate([jnp.cos(ar), jnp.cos(ar), jnp.cos(ac), jnp.cos(ac)], axis=1)
    sin = jnp.concatenate([-jnp.sin(ar), jnp.sin(ar), -jnp.sin(ac), jnp.sin(ac)], axis=1)
    return cos, sin


def _rope(x, cos, sin):
    quarter = HEAD_DIM // 4
    lane = _lane_iota(x.shape)
    partner = jnp.where(lane % (2 * quarter) < quarter,
                        pltpu.roll(x, HEAD_DIM - quarter, 1), pltpu.roll(x, quarter, 1))
    return x * cos + partner * sin


def _attn_kernel(*refs, n_tok, tq, n_ctx, rope):
    if rope:
        (q_ref, k_ref, v_ref, qn_ref, kn_ref, cq_ref, sq_ref, ck_ref, sk_ref, xk_ref, xv_ref,
         o_ref, ks_ref, vs_ref) = refs
    else:
        q_ref, k_ref, v_ref, qn_ref, kn_ref, o_ref, ko_ref, ks_ref, vs_ref = refs
    rep = ATT_HEADS // ATT_KV_HEADS

    @pl.when(pl.program_id(2) == 0)
    def _():
        kn = _rms_rows(k_ref[...], kn_ref[...])
        if rope:
            kn = _rope(kn, ck_ref[...], sk_ref[...])
            ks_ref[n_tok:n_tok + n_ctx, :] = xk_ref[...].astype(MXU_DTYPE)
            vs_ref[n_tok:n_tok + n_ctx, :] = xv_ref[...].astype(MXU_DTYPE)
        else:
            ko_ref[...] = kn
        ks_ref[0:n_tok, :] = kn.astype(MXU_DTYPE)
        vs_ref[0:n_tok, :] = v_ref[...].astype(MXU_DTYPE)

    heads = []
    for r in range(rep):
        qh = _rms_rows(q_ref[:, r * HEAD_DIM:(r + 1) * HEAD_DIM], qn_ref[...])
        if rope:
            qh = _rope(qh, cq_ref[...], sq_ref[...])
        heads.append(qh.astype(MXU_DTYPE))
    for r0 in range(0, rep, ATT_STACK):
        qs = jnp.concatenate(heads[r0:r0 + ATT_STACK], axis=0)
        s = _dot_nt(qs, ks_ref[...]) * (HEAD_DIM ** -0.5)
        e = jnp.exp(s - jnp.max(s, axis=-1, keepdims=True))
        o = _dot(e.astype(MXU_DTYPE), vs_ref[...]) / jnp.sum(e, axis=-1, keepdims=True)
        for r in range(ATT_STACK):
            o_ref[:, (r0 + r) * HEAD_DIM:(r0 + r + 1) * HEAD_DIM] = o[r * tq:(r + 1) * tq].astype(o_ref.dtype)


def _attention_call(p, row_off, bsz, n_tok, q_norm, k_norm, ctx_kv, out_prev):
    m = p.shape[0]
    rope = ctx_kv is not None
    tq = 128 if rope else n_tok
    nq = n_tok // tq
    rep = ATT_HEADS // ATT_KV_HEADS
    qw = rep * HEAD_DIM
    n_ctx = ctx_kv[0].shape[1] if rope else 0
    assert row_off % n_tok == 0 and n_tok % tq == 0
    rb, sb = row_off // tq, row_off // n_tok

    def seq_spec(col0):
        return pl.BlockSpec((n_tok, HEAD_DIM), lambda b, g, i: (sb + b, col0 // HEAD_DIM + g))

    vec = pl.BlockSpec((1, HEAD_DIM), lambda b, g, i: (0, 0))
    in_specs = [pl.BlockSpec((tq, qw), lambda b, g, i: (rb + b * nq + i, P_AQ // qw + g)),
                seq_spec(P_AK), seq_spec(P_AV), vec, vec]
    args = [p, p, p, q_norm.reshape(1, HEAD_DIM), k_norm.reshape(1, HEAD_DIM)]
    out_block = pl.BlockSpec((tq, qw), lambda b, g, i: (rb + b * nq + i, g))
    out_specs = [out_block]
    out_shape = [jax.ShapeDtypeStruct((m, GROUP_W), MXU_DTYPE)]
    aliases = {}
    if rope:
        cos, sin = _rope_tables(n_tok)
        in_specs += [pl.BlockSpec((tq, HEAD_DIM), lambda b, g, i: (i, 0))] * 2
        in_specs += [pl.BlockSpec((n_tok, HEAD_DIM), lambda b, g, i: (0, 0))] * 2
        in_specs += [pl.BlockSpec((None, n_ctx, HEAD_DIM), lambda b, g, i: (b, 0, g))] * 2
        args += [cos, sin, cos, sin, ctx_kv[0], ctx_kv[1]]
    else:
        out_specs.append(pl.BlockSpec((n_tok, HEAD_DIM), lambda b, g, i: (b, g)))
        out_shape.append(jax.ShapeDtypeStruct((bsz * n_tok, ATT_KV_HEADS * HEAD_DIM), jnp.float32))
    if out_prev is not None:
        in_specs.append(pl.BlockSpec(memory_space=pl.ANY))
        args.append(out_prev)
        aliases = {len(args) - 1: 0}
    n_keys = n_tok + n_ctx

    def body(*refs):
        if out_prev is not None:
            refs = refs[:len(args) - 1] + refs[len(args):]
        _attn_kernel(*refs, n_tok=n_tok, tq=tq, n_ctx=n_ctx, rope=rope)

    out = pl.pallas_call(
        body,
        grid=(bsz, ATT_KV_HEADS, nq),
        in_specs=in_specs,
        out_specs=out_specs,
        out_shape=out_shape,
        scratch_shapes=[pltpu.VMEM((n_keys, HEAD_DIM), MXU_DTYPE), pltpu.VMEM((n_keys, HEAD_DIM), MXU_DTYPE)],
        input_output_aliases=aliases,
        compiler_params=_params(3, 48 << 20),
        name="attention",
    )(*args)
    return out


def _lru_kernel(*refs, n_tok, with_state):
    if with_state:
        (x_ref, g_ref, cw_ref, cb_ref, wa_ref, ba_ref, wx_ref, bx_ref, lam_ref, h0_ref, o_ref) = refs
    else:
        (x_ref, g_ref, cw_ref, cb_ref, wa_ref, ba_ref, wx_ref, bx_ref, lam_ref, o_ref, hT_ref) = refs
    xl = _conv4(x_ref[...], cw_ref, cb_ref)
    xb = xl.astype(MXU_DTYPE)
    t = _row_iota(xl.shape)
    h_sum = None
    for d in range(2):
        up = d == 1
        r = jax.nn.sigmoid(_dot(xb, wa_ref[d]) + ba_ref[d])
        i = jax.nn.sigmoid(_dot(xb, wx_ref[d]) + bx_ref[d])
        log_a = -LRU_C * r * _softplus(-lam_ref[d])
        a = jnp.exp(log_a)
        th = jnp.tanh(log_a)
        u = jnp.sqrt(-2.0 * th / (1.0 - th)) * i * xl
        if with_state:
            first = (t == n_tok - 1) if up else (t == 0)
            u = jnp.where(first, u + a * h0_ref[d], u)
        s = 1
        while s < n_tok:
            u = u + a * _shift_rows(u, s, 0.0, up=up)
            a = a * _shift_rows(a, s, 1.0, up=up)
            s *= 2
        h_sum = u if h_sum is None else h_sum + u
        if not with_state:
            hT_ref[d] = u[0:1, :] if up else u[n_tok - 1:n_tok, :]
    o_ref[...] = (h_sum * jax.nn.gelu(g_ref[...])).astype(o_ref.dtype)


def _lru_call(p, row_off, bsz, n_tok, lp, h0, out_prev):
    m = p.shape[0]
    with_state = h0 is not None
    sb = row_off // n_tok
    assert row_off % n_tok == 0

    def seq_spec(col0):
        return pl.BlockSpec((n_tok, LRU_BW), lambda b, n: (sb + b, col0 // LRU_BW + n))

    def par2(shape):
        return pl.BlockSpec((2,) + shape + (LRU_BW,), lambda b, n: (0,) + (0,) * len(shape) + (n,))

    in_specs = [seq_spec(P_LX), seq_spec(P_LG),
                pl.BlockSpec((CONV_W, LRU_BW), lambda b, n: (0, n)),
                pl.BlockSpec((1, LRU_BW), lambda b, n: (0, n)),
                pl.BlockSpec((2, None, LRU_BW, LRU_BW), lambda b, n: (0, n, 0, 0)), par2((1,)),
                pl.BlockSpec((2, None, LRU_BW, LRU_BW), lambda b, n: (0, n, 0, 0)), par2((1,)), par2((1,))]
    args = [p, p, lp['lru_conv_w'], lp['lru_conv_b'].reshape(1, LRU_W),
            lp['lru_wa'].astype(MXU_DTYPE), lp['lru_ba'].reshape(2, 1, LRU_W),
            lp['lru_wx'].astype(MXU_DTYPE), lp['lru_bx'].reshape(2, 1, LRU_W), lp['lru_lam'].reshape(2, 1, LRU_W)]
    out_specs = [pl.BlockSpec((n_tok, LRU_BW), lambda b, n: (sb + b, n))]
    out_shape = [jax.ShapeDtypeStruct((m, GROUP_W), MXU_DTYPE)]
    if with_state:
        in_specs.append(pl.BlockSpec((None, 2, 1, LRU_BW), lambda b, n: (b, 0, 0, n)))
        args.append(h0.reshape(bsz, 2, 1, LRU_W))
    else:
        out_specs.append(pl.BlockSpec((None, 2, 1, LRU_BW), lambda b, n: (b, 0, 0, n)))
        out_shape.append(jax.ShapeDtypeStruct((bsz, 2, 1, LRU_W), jnp.float32))
    n_in = len(args)
    aliases = {}
    if out_prev is not None:
        in_specs.append(pl.BlockSpec(memory_space=pl.ANY))
        args.append(out_prev)
        aliases = {n_in: 0}

    def body(*refs):
        if out_prev is not None:
            refs = refs[:n_in] + refs[n_in + 1:]
        _lru_kernel(*refs, n_tok=n_tok, with_state=with_state)

    return pl.pallas_call(
        body,
        grid=(bsz, LRU_BLOCKS),
        in_specs=in_specs,
        out_specs=out_specs,
        out_shape=out_shape,
        input_output_aliases=aliases,
        compiler_params=_params(2, 48 << 20),
        name="rg_lru",
    )(*args)


SSD_L = 128
SSD_GH = SSD_HEADS // SSD_GROUPS
SSD_GW = SSD_GH * SSD_HEADDIM


def _ssd_kernel(*refs, n_tok, with_state):
    if with_state:
        (x_ref, b_ref, c_ref, s_ref, cwx_ref, cbx_ref, cwb_ref, cbb_ref, cwc_ref, cbc_ref, dtb_ref, alog_ref,
         dskip_ref, h0_ref, y_ref, xc_s, xdt_s, xte_s, dec_s, bc_s, cc_s, cum_s, cumt_s, h_s) = refs
    else:
        (x_ref, b_ref, c_ref, s_ref, cwx_ref, cbx_ref, cwb_ref, cbb_ref, cwc_ref, cbc_ref, dtb_ref, alog_ref,
         dskip_ref, y_ref, hT_ref, xc_s, xdt_s, xte_s, dec_s, bc_s, cc_s, cum_s, cumt_s, h_s) = refs
    L = SSD_L
    n_chunk = n_tok // L
    g = pl.program_id(1)

    xc_s[...] = _silu(_conv4(x_ref[...], cwx_ref, cbx_ref))
    bc_s[...] = _silu(_conv4(b_ref[...], cwb_ref, cbb_ref)).astype(MXU_DTYPE)
    cc_s[...] = _silu(_conv4(c_ref[...], cwc_ref, cbc_ref)).astype(MXU_DTYPE)
    dt = _softplus(s_ref[...] + dtb_ref[...])
    y_ref[...] = xc_s[...] * dskip_ref[...]

    e8 = (_row_iota((S_W, SSD_GW)) == _lane_iota((S_W, SSD_GW)) // SSD_HEADDIM).astype(MXU_DTYPE)
    ri, ci = _row_iota((L, L)), _lane_iota((L, L))
    lane_w = _lane_iota((L, 2 * SSD_HEADDIM))

    t = _row_iota((n_tok, S_W))
    for d in range(2):
        up = d == 1
        end = 0 if up else L - 1
        sel = (_row_iota((S_W, S_W)) == _lane_iota((S_W, S_W)) + (d * SSD_HEADS + g * SSD_GH)).astype(MXU_DTYPE)
        sel = jnp.where(_lane_iota((S_W, S_W)) < SSD_GH, sel, jnp.zeros_like(sel))
        dt_sel = _dot01_right(dt, sel)
        a_row = -jnp.exp(_dot01_right(alog_ref[...], sel))
        cum = dt_sel * a_row
        s = 1
        while s < L:
            ok = (t % L < L - s) if up else (t % L >= s)
            cum = cum + jnp.where(ok, pltpu.roll(cum, (n_tok - s) if up else s, 0), 0.0)
            s *= 2
        cum_s[...] = cum
        cumt_s[...] = jnp.swapaxes(cum.reshape(n_chunk, L, S_W), 1, 2).reshape(n_chunk * S_W, L)
        cum_i = _dot01_right(cum, e8).reshape(n_chunk, L, SSD_GW)
        xdt = xc_s[...] * _dot01_right(dt_sel, e8)
        xdt_s[...] = xdt.astype(MXU_DTYPE)
        to_end = jnp.exp(cum_i[:, end:end + 1, :] - cum_i)
        xte_s[...] = (xdt.reshape(n_chunk, L, SSD_GW) * to_end).reshape(n_tok, SSD_GW).astype(MXU_DTYPE)
        dec_s[...] = jnp.exp(cum_i).reshape(n_tok, SSD_GW)
        if with_state:
            h_s[...] = h0_ref[d].reshape(SSD_GW, SSD_STATE)
        else:
            h_s[...] = jnp.zeros_like(h_s)
        tri = ((ci >= ri) if up else (ci <= ri))

        def chunk(i, carry):
            c = (n_chunk - 1 - i) if up else i
            t0 = pl.multiple_of(c * L, L)
            rows = pl.ds(t0, L)
            cum = cum_s[rows, :]
            cum_t = cumt_s[rows, :]
            bcv, ccv = bc_s[rows, :], cc_s[rows, :]
            cb = _dot_nt(ccv, bcv)
            h_b16 = h_s[...].astype(MXU_DTYPE)
            y = _dot_nt(ccv, h_b16) * dec_s[rows, :]
            xdt_b16 = xdt_s[rows, :]
            pieces = []
            for pair in range(SSD_GH // 2):
                xp = xdt_b16[:, pair * 2 * SSD_HEADDIM:(pair + 1) * 2 * SSD_HEADDIM]
                acc = None
                for sub in range(2):
                    hh = 2 * pair + sub
                    seg = jnp.where(tri, jnp.exp(cum[:, hh:hh + 1] - cum_t[hh:hh + 1, :]), 0.0)
                    sc = (cb * seg).astype(MXU_DTYPE)
                    half = (lane_w // SSD_HEADDIM) == sub
                    part = _dot(sc, jnp.where(half, xp, jnp.zeros_like(xp)))
                    acc = part if acc is None else acc + part
                pieces.append(acc)
            y = y + jnp.concatenate(pieces, axis=1)
            y_ref[rows, :] += y
            upd = _dot_tn(xte_s[rows, :], bcv)
            for hh in range(SSD_GH):
                blk = slice(hh * SSD_HEADDIM, (hh + 1) * SSD_HEADDIM)
                h_s[blk, :] = h_s[blk, :] * jnp.exp(cum_t[hh:hh + 1, end:end + 1]) + upd[blk, :]
            return carry

        lax.fori_loop(0, n_chunk, chunk, 0, unroll=2)
        if not with_state:
            hT_ref[d] = h_s[...].reshape(SSD_GH, SSD_HEADDIM, SSD_STATE)


def _ssd_call(p, s, row_off, bsz, n_tok, lp, h0):
    with_state = h0 is not None
    sb = row_off // n_tok
    assert row_off % n_tok == 0 and n_tok % SSD_L == 0
    assert SSD_L == S_W

    def seq_spec(col0, w):
        return pl.BlockSpec((n_tok, w), lambda b, g: (sb + b, col0 // w + g))

    def conv_specs(col0, w):
        return [pl.BlockSpec((CONV_W, w), lambda b, g: (0, col0 // w + g)),
                pl.BlockSpec((1, w), lambda b, g: (0, col0 // w + g))]

    small = pl.BlockSpec((1, S_W), lambda b, g: (0, 0))
    pad = jnp.zeros((S_W - 2 * SSD_HEADS,), jnp.float32)
    dtb = jnp.concatenate([lp['ssd_dt_bias'].reshape(-1), pad]).reshape(1, S_W)
    alog = jnp.concatenate([lp['ssd_a_log'].reshape(-1), pad]).reshape(1, S_W)
    dskip = jnp.repeat(lp['ssd_d'][0] + lp['ssd_d'][1], SSD_HEADDIM).reshape(1, GROUP_W)
    cw, cb = lp['ssd_conv_w'], lp['ssd_conv_b'].reshape(1, -1)
    in_specs = ([seq_spec(P_SX, SSD_GW), seq_spec(P_SB, SSD_STATE), seq_spec(P_SC, SSD_STATE),
                 pl.BlockSpec((n_tok, S_W), lambda b, g: (sb + b, 0))]
                + conv_specs(0, SSD_GW) + conv_specs(GROUP_W, SSD_STATE)
                + conv_specs(GROUP_W + SSD_GROUPS * SSD_STATE, SSD_STATE)
                + [small, small, pl.BlockSpec((1, SSD_GW), lambda b, g: (0, g))])
    args = [p, p, p, s, cw, cb, cw, cb, cw, cb, dtb, alog, dskip]
    state_block = pl.BlockSpec((None, 2, SSD_GH, SSD_HEADDIM, SSD_STATE), lambda b, g: (b, 0, g, 0, 0))
    out_specs = [pl.BlockSpec((n_tok, SSD_GW), lambda b, g: (b, g))]
    out_shape = [jax.ShapeDtypeStruct((bsz * n_tok, GROUP_W), jnp.float32)]
    if with_state:
        in_specs.append(state_block)
        args.append(h0)
    else:
        out_specs.append(state_block)
        out_shape.append(jax.ShapeDtypeStruct((bsz, 2, SSD_HEADS, SSD_HEADDIM, SSD_STATE), jnp.float32))
    return pl.pallas_call(
        functools.partial(_ssd_kernel, n_tok=n_tok, with_state=with_state),
        grid=(bsz, SSD_GROUPS),
        in_specs=in_specs,
        out_specs=out_specs,
        out_shape=out_shape,
        scratch_shapes=[pltpu.VMEM((n_tok, SSD_GW), jnp.float32), pltpu.VMEM((n_tok, SSD_GW), MXU_DTYPE),
                        pltpu.VMEM((n_tok, SSD_GW), MXU_DTYPE), pltpu.VMEM((n_tok, SSD_GW), jnp.float32),
                        pltpu.VMEM((n_tok, SSD_STATE), MXU_DTYPE), pltpu.VMEM((n_tok, SSD_STATE), MXU_DTYPE),
                        pltpu.VMEM((n_tok, S_W), jnp.float32), pltpu.VMEM((n_tok, SSD_L), jnp.float32),
                        pltpu.VMEM((SSD_GW, SSD_STATE), jnp.float32)],
        compiler_params=_params(2, VMEM_LIMIT_V7X),
        name="ssd_scan",
    )(*args)


def _gated_norm_kernel(y_ref, z_ref, w_ref, o_ref):
    o_ref[...] = _rms_rows(y_ref[...] * _silu(z_ref[...]), w_ref[...]).astype(o_ref.dtype)


def _ssd_finish(y, p, row_off, norm_w, out_prev):
    tm = 256
    m, n = p.shape[0], y.shape[0]
    rb = row_off // tm
    in_specs = [pl.BlockSpec((tm, GROUP_W), lambda i: (i, 0)),
                pl.BlockSpec((tm, GROUP_W), lambda i: (rb + i, P_SZ // GROUP_W)),
                pl.BlockSpec((1, GROUP_W), lambda i: (0, 0))]
    args = [y, p, norm_w.reshape(1, GROUP_W)]
    aliases = {}
    if out_prev is not None:
        in_specs.append(pl.BlockSpec(memory_space=pl.ANY))
        args.append(out_prev)
        aliases = {3: 0}

    def body(y_ref, z_ref, w_ref, *rest):
        _gated_norm_kernel(y_ref, z_ref, w_ref, rest[-1])

    return pl.pallas_call(
        body,
        grid=(n // tm,),
        in_specs=in_specs,
        out_specs=pl.BlockSpec((tm, GROUP_W), lambda i: (rb + i, 0)),
        out_shape=jax.ShapeDtypeStruct((m, GROUP_W), MXU_DTYPE),
        input_output_aliases=aliases,
        compiler_params=_params(1, 32 << 20),
        name="ssd_gated_norm",
    )(*args)


GLA_L = 16
GLA_GROUP = 16
LOG2_E = 1.4426950408889634


def _gla_kernel(*refs, n_tok, with_state):
    if with_state:
        (q_ref, k_ref, v_ref, gg_ref, s_ref, gw_ref, gb_ref, nw_ref, s0_ref, o_ref,
         dec_s, qe_s, ke_s, att_s, o_s, st_s) = refs
    else:
        (q_ref, k_ref, v_ref, gg_ref, s_ref, gw_ref, gb_ref, nw_ref, o_ref, sT_ref,
         dec_s, qe_s, ke_s, att_s, o_s, st_s) = refs
    L = GLA_L
    n_blk = n_tok // L
    blk3 = (n_blk, L, GLA_DK)
    q3 = (q_ref[...] * (GLA_DK ** -0.5)).reshape(blk3)
    k3 = k_ref[...].reshape(blk3)
    sb16 = s_ref[...].astype(MXU_DTYPE)
    t = _row_iota((n_tok, GLA_DK))
    ri3 = lax.broadcasted_iota(jnp.int32, blk3, 1)
    lane3 = lax.broadcasted_iota(jnp.int32, blk3, 2)

    for d in range(2):
        up = d == 1
        end = 0 if up else L - 1
        gate = _dot(sb16, gw_ref[d]) + gb_ref[d]
        b = -_softplus(-gate) / GLA_NORMALIZER
        s = 1
        while s < L:
            ok = (t % L < L - s) if up else (t % L >= s)
            b = b + jnp.where(ok, pltpu.roll(b, (n_tok - s) if up else s, 0), 0.0)
            s *= 2
        b3 = b.reshape(blk3)
        b_end = b3[:, end:end + 1, :]
        dec_s[d] = jnp.exp(jnp.broadcast_to(b_end, blk3)).reshape(n_tok, GLA_DK)
        qe_s[d] = (q3 * jnp.exp(b3)).reshape(n_tok, GLA_DK).astype(MXU_DTYPE)
        ke_s[d] = (k3 * jnp.exp(b_end - b3)).reshape(n_tok, GLA_DK).astype(MXU_DTYPE)
        code = jnp.where((ri3 <= lane3) if up else (ri3 >= lane3), lane3, -1)
        b2 = b3 * LOG2_E
        half = L // 2
        att_h = [jnp.zeros((n_blk, half, GLA_DK), jnp.float32) for _ in range(2)]
        for j in range(L):
            for h in range(2):
                if (h == 0 and j >= half) if not up else (h == 1 and j < half):
                    continue
                rows = slice(h * half, (h + 1) * half)
                w = jnp.exp2(b2[:, rows, :] - b2[:, j:j + 1, :]) * q3[:, rows, :] * k3[:, j:j + 1, :]
                att_h[h] = jnp.where(code[:, rows, :] == j, jnp.sum(w, axis=-1, keepdims=True), att_h[h])
        att = jnp.concatenate(att_h, axis=1)
        att_s[d] = att.reshape(n_tok, GLA_DK).astype(MXU_DTYPE)
        if with_state:
            st_s[d] = s0_ref[d].T
        else:
            st_s[d] = jnp.zeros((GLA_DV, GLA_DK), jnp.float32)

    o_s[...] = jnp.zeros_like(o_s)
    n_grp = n_blk // GLA_GROUP
    rows_per_trip = GLA_GROUP * L

    def group(i, carry):
        for d in range(2):
            up = d == 1
            c = (n_grp - 1 - i) if up else i
            rows = pl.ds(pl.multiple_of(c * rows_per_trip, rows_per_trip), rows_per_trip)
            vg = v_ref[rows, :].astype(MXU_DTYPE)
            keg, attg, qeg, decg = ke_s[d, rows, :], att_s[d, rows, 0:L], qe_s[d, rows, :], dec_s[d, rows, :]
            order = range(GLA_GROUP - 1, -1, -1) if up else range(GLA_GROUP)
            upd = {k: _dot_tn(vg[k * L:(k + 1) * L], keg[k * L:(k + 1) * L]) for k in order}
            intra = {k: _dot(attg[k * L:(k + 1) * L], vg[k * L:(k + 1) * L]) for k in order}
            st = st_s[d]
            outs = {}
            for k in order:
                outs[k] = _dot_nt(qeg[k * L:(k + 1) * L], st.astype(MXU_DTYPE)) + intra[k]
                st = st * decg[k * L:k * L + 1, :] + upd[k]
            st_s[d] = st
            o_s[rows, :] += jnp.concatenate([outs[k] for k in range(GLA_GROUP)], axis=0)
        return carry

    lax.fori_loop(0, n_grp, group, 0)
    if not with_state:
        for d in range(2):
            sT_ref[d] = st_s[d].T
    o = _rms_rows(o_s[...], nw_ref[...]) * _silu(gg_ref[...])
    o_ref[...] = o.astype(o_ref.dtype)


def _gla_call(p, s, row_off, bsz, n_tok, lp, s0, out_prev):
    m = p.shape[0]
    with_state = s0 is not None
    sb = row_off // n_tok
    assert row_off % n_tok == 0 and n_tok % (GLA_L * GLA_GROUP) == 0

    def seq_spec(col0, w):
        return pl.BlockSpec((n_tok, w), lambda b, h: (sb + b, col0 // w + h))

    gw = lp['gla_gate_w'].reshape(2, GLA_RANK, GLA_HEADS, GLA_DK).transpose(0, 2, 1, 3)
    gw_rows = jnp.zeros((2, GLA_HEADS, S_W, GLA_DK), jnp.float32)
    for d in range(2):
        r0 = 2 * SSD_HEADS + d * GLA_RANK
        gw_rows = gw_rows.at[d, :, r0:r0 + GLA_RANK, :].set(gw[d])
    in_specs = [seq_spec(P_GQ, GLA_DK), seq_spec(P_GK, GLA_DK), seq_spec(P_GV, GLA_DV), seq_spec(P_GG, GLA_DV),
                pl.BlockSpec((n_tok, S_W), lambda b, h: (sb + b, 0)),
                pl.BlockSpec((2, None, S_W, GLA_DK), lambda b, h: (0, h, 0, 0)),
                pl.BlockSpec((2, None, 1, GLA_DK), lambda b, h: (0, h, 0, 0)),
                pl.BlockSpec((1, GLA_DV), lambda b, h: (0, 0))]
    args = [p, p, p, p, s, gw_rows.astype(MXU_DTYPE), lp['gla_gate_b'].reshape(2, GLA_HEADS, 1, GLA_DK),
            lp['gla_norm_w'].reshape(1, GLA_DV)]
    state_block = pl.BlockSpec((None, 2, None, GLA_DK, GLA_DV), lambda b, h: (b, 0, h, 0, 0))
    out_specs = [pl.BlockSpec((n_tok, GLA_DV), lambda b, h: (sb + b, h))]
    out_shape = [jax.ShapeDtypeStruct((m, GROUP_W), MXU_DTYPE)]
    if with_state:
        in_specs.append(state_block)
        args.append(s0)
    else:
        out_specs.append(state_block)
        out_shape.append(jax.ShapeDtypeStruct((bsz, 2, GLA_HEADS, GLA_DK, GLA_DV), jnp.float32))
    n_in = len(args)
    aliases = {}
    if out_prev is not None:
        in_specs.append(pl.BlockSpec(memory_space=pl.ANY))
        args.append(out_prev)
        aliases = {n_in: 0}

    def body(*refs):
        if out_prev is not None:
            refs = refs[:n_in] + refs[n_in + 1:]
        _gla_kernel(*refs, n_tok=n_tok, with_state=with_state)

    return pl.pallas_call(
        body,
        grid=(bsz, GLA_HEADS),
        in_specs=in_specs,
        out_specs=out_specs,
        out_shape=out_shape,
        scratch_shapes=[pltpu.VMEM((2, n_tok, GLA_DK), jnp.float32), pltpu.VMEM((2, n_tok, GLA_DK), MXU_DTYPE),
                        pltpu.VMEM((2, n_tok, GLA_DK), MXU_DTYPE), pltpu.VMEM((2, n_tok, GLA_DK), MXU_DTYPE),
                        pltpu.VMEM((n_tok, GLA_DV), jnp.float32), pltpu.VMEM((2, GLA_DV, GLA_DK), jnp.float32)],
        input_output_aliases=aliases,
        compiler_params=_params(2, 48 << 20),
        name="gla_scan",
    )(*args)


def _mixers(p, s, lp, cached, donors):
    ctx_k, ctx_v, ssd0, gla0, lru0 = cached
    kv_w = ATT_KV_HEADS * HEAD_DIM
    att, k_new = _attention_call(p, 0, BATCH, SEQ, lp['q_norm'], lp['k_norm'], None, donors[0])
    att, = _attention_call(p, M_PROMPT, DEC_BATCH, DEC_SEQ, lp['q_norm'], lp['k_norm'],
                           (ctx_k.reshape(DEC_BATCH, PAST_LEN, kv_w), ctx_v.reshape(DEC_BATCH, PAST_LEN, kv_w)), att)
    v_new = p[:M_PROMPT, P_AV:P_AV + kv_w]

    y_p, ssd_new = _ssd_call(p, s, 0, BATCH, SEQ, lp, None)
    y_s, = _ssd_call(p, s, M_PROMPT, DEC_BATCH, DEC_SEQ, lp, ssd0)
    ssd = _ssd_finish(y_p, p, 0, lp['ssd_norm_w'], donors[1])
    ssd = _ssd_finish(y_s, p, M_PROMPT, lp['ssd_norm_w'], ssd)

    gla, gla_new = _gla_call(p, s, 0, BATCH, SEQ, lp, None, donors[2])
    gla, = _gla_call(p, s, M_PROMPT, DEC_BATCH, DEC_SEQ, lp, gla0, gla)

    lru, lru_new = _lru_call(p, 0, BATCH, SEQ, lp, None, donors[3])
    lru, = _lru_call(p, M_PROMPT, DEC_BATCH, DEC_SEQ, lp, lru0, lru)

    new_ctx = (k_new.reshape(BATCH, SEQ, ATT_KV_HEADS, HEAD_DIM), v_new.reshape(BATCH, SEQ, ATT_KV_HEADS, HEAD_DIM),
               ssd_new, gla_new, lru_new.reshape(BATCH, 2, LRU_W))
    return [att, ssd, gla, lru], new_ctx


IN_W = 9280
_W_IN_MAIN = ((0, P_AQ, 1024), (1024, P_AK, 256), (1280, P_AV, 256), (1536, P_SX, 1024), (2560, P_SZ, 1024),
              (3584, P_SB, 256), (3840, P_SC, 256), (4128, P_GQ, 512), (4640, P_GK, 512), (5152, P_GV, 1024),
              (6208, P_GG, 1024), (7232, P_LX, 1024), (8256, P_LG, 1024))
_W_IN_SMALL = ((4096, 0, 2 * SSD_HEADS), (6176, 2 * SSD_HEADS, 2 * GLA_RANK))


def _w_in_kernel(w_ref, main_ref, small_ref):
    for src, dst, width in _W_IN_MAIN:
        main_ref[dst:dst + width, :] = w_ref[src:src + width, :].astype(main_ref.dtype)
    small_ref[...] = jnp.zeros_like(small_ref)
    for src, dst, width in _W_IN_SMALL:
        small_ref[dst:dst + width, :] = w_ref[src:src + width, :].astype(small_ref.dtype)


def _reorder_w_in(w_t, l):
    tk = 256
    return pl.pallas_call(
        _w_in_kernel,
        grid=(D_MODEL // tk,),
        in_specs=[pl.BlockSpec((None, IN_W, tk), lambda i: (l, 0, i))],
        out_specs=[pl.BlockSpec((P_W, tk), lambda i: (0, i)), pl.BlockSpec((S_W, tk), lambda i: (0, i))],
        out_shape=[jax.ShapeDtypeStruct((P_W, D_MODEL), MXU_DTYPE), jax.ShapeDtypeStruct((S_W, D_MODEL), MXU_DTYPE)],
        compiler_params=_params(1, 48 << 20),
        name="w_in_regroup",
    )(w_t)


def kernel(x_prompt, x_sample, c, cache_attn_k, cache_attn_v, state_ssd, state_gla, state_lru, c_ctx,
           mod_w, mod_b, ln_g, ln_b, ffn_w_gate, ffn_w_up, ffn_w_down, w_in, w_out, q_norm, k_norm,
           ssd_conv_w, ssd_conv_b, ssd_a_log, ssd_dt_bias, ssd_d, ssd_norm_w,
           gla_gate_w, gla_gate_b, gla_norm_w,
           lru_conv_w, lru_conv_b, lru_wa, lru_ba, lru_wx, lru_bx, lru_lam):
    cond = jnp.concatenate([c_ctx[None], c, jnp.zeros((COND_ROWS - N_COND, D_MODEL), jnp.float32)], axis=0)
    mod_all = _mod_table(cond, mod_w, mod_b)

    def ffn(x, xm, mod, k_gate, l, half):
        h = _dense_cast_first([xm], [ffn_w_gate, ffn_w_up], (l, half), spare['h'], tm_first=2048, tn_first=256,
                              tm=2048, tn=256, out_dtype=MXU_DTYPE, name="ffn_up", swiglu=True)
        y = _dense_cast_first([h], [ffn_w_down], (l, half), spare['y'], tm_first=512, tn_first=256, tm=512, tn=512,
                              out_dtype=jnp.float32, name="ffn_down", resid=(x, mod, k_gate, 0.5))
        spare['h'] = h
        return y

    spare = dict(h=jnp.zeros((M_TOK, D_FF), MXU_DTYPE), y=jnp.zeros((M_TOK, D_MODEL), jnp.float32),
                 mix=[jnp.zeros((M_TOK, GROUP_W), MXU_DTYPE)] * 4)
    x, xm = _modulate(x_prompt, x_sample, mod_all[0], 1, 0)
    w_in_t = jnp.swapaxes(w_in, 1, 2)
    ctx_out = []
    for l in range(DEPTH):
        mod = mod_all[l]
        lp = dict(q_norm=q_norm[l], k_norm=k_norm[l], ssd_conv_w=ssd_conv_w[l], ssd_conv_b=ssd_conv_b[l],
                  ssd_a_log=ssd_a_log[l], ssd_dt_bias=ssd_dt_bias[l], ssd_d=ssd_d[l], ssd_norm_w=ssd_norm_w[l],
                  gla_gate_w=gla_gate_w[l], gla_gate_b=gla_gate_b[l], gla_norm_w=gla_norm_w[l],
                  lru_conv_w=lru_conv_w[l], lru_conv_b=lru_conv_b[l], lru_wa=lru_wa[l], lru_ba=lru_ba[l],
                  lru_wx=lru_wx[l], lru_bx=lru_bx[l], lru_lam=lru_lam[l])
        y = ffn(x, xm, mod, 2, l, 0)
        x, xm = _layer_norm(y, ln_g[l, 0], ln_b[l, 0], mod, 4, 3)
        spare['y'] = y
        w_main, w_small = _reorder_w_in(w_in_t, l)
        dense = dict(tile0=0, n_tiles=M_TOK // 1024, tm=1024, out_dtype=jnp.float32, w_t=True)
        p = _dense([xm], [w_main], (), tn=1024, name="proj_in", **dense)
        s = _dense([xm], [w_small], (), tn=S_W, name="proj_in_small", **dense)
        cached = (cache_attn_k[:, l], cache_attn_v[:, l], state_ssd[:, l], state_gla[:, l], state_lru[:, l])
        mix, new_ctx = _mixers(p, s, lp, cached, spare['mix'])
        ctx_out.append(new_ctx)
        y = _dense_cast_first(mix, [w_out], (l,), spare['y'], tm_first=1024, tn_first=512, tm=1024, tn=1024,
                              out_dtype=jnp.float32, name="proj_out", resid=(x, mod, 5, 1.0))
        spare['mix'] = mix
        x, xm = _layer_norm(y, ln_g[l, 1], ln_b[l, 1], mod, 7, 6)
        spare['y'] = y
        y = ffn(x, xm, mod, 8, l, 1)
        if l + 1 < DEPTH:
            x, xm = _layer_norm(y, ln_g[l, 2], ln_b[l, 2], mod_all[l + 1], 1, 0)
            spare['y'] = y
        else:
            y_prompt, y_sample = _layer_norm(y, ln_g[l, 2], ln_b[l, 2])

    y_prompt = y_prompt.reshape(BATCH, SEQ, D_MODEL)
    y_sample = y_sample.reshape(DEC_BATCH, DEC_SEQ, D_MODEL)
    new_k, new_v, new_ssd, new_gla, new_lru = (jnp.stack([s_[i] for s_ in ctx_out], axis=1) for i in range(5))
    return (y_prompt, y_sample, new_k, new_v, new_ssd, new_gla, new_lru)
```

```python
import functools
import math

import numpy as np
import jax
import jax.numpy as jnp
from jax import lax
from jax.experimental import pallas as pl
from jax.experimental.pallas import tpu as pltpu

D_MODEL = 4096
BATCH = 16
SEQ = 256
DEPTH = 2
DEC_BATCH = 4
DEC_SEQ = 2048
PAST_LEN = 256
GRID_W = 64
GROUP_W = 1024
HEAD_DIM = 128
ATT_HEADS = 8
ATT_KV_HEADS = 2
ROPE_THETA = 10000.0
SSD_HEADDIM = 64
SSD_HEADS = 16
SSD_STATE = 128
SSD_GROUPS = 2
CONV_W = 4
GLA_HEADS = 4
GLA_DK = 128
GLA_DV = 256
GLA_RANK = 16
GLA_NORMALIZER = 16.0
LRU_W = 1024
LRU_BLOCKS = 8
LRU_BW = 128
LRU_C = 8.0
D_FF = 11008
N_MOD = 9
LN_EPS = 1e-5
RMS_EPS = 1e-6
ALPHA = (2.0 * DEPTH) ** 0.25

M_PROMPT = BATCH * SEQ
M_SAMPLE = DEC_BATCH * DEC_SEQ
M_TOK = M_PROMPT + M_SAMPLE
N_COND = 1 + DEC_BATCH
COND_ROWS = 8

VMEM_LIMIT_V7X = 56 * 1024 * 1024

MXU_DTYPE = jnp.bfloat16

P_AQ, P_SX, P_SZ, P_GV, P_GG, P_LX, P_LG = 0, 1024, 2048, 3072, 4096, 5120, 6144
P_GQ, P_GK = 7168, 7680
P_AK, P_AV, P_SB, P_SC = 8192, 8448, 8704, 8960
P_W = 9216
S_W = 128


def _params(n_axes, vmem_bytes):
    return pltpu.CompilerParams(dimension_semantics=("arbitrary",) * n_axes,
                                vmem_limit_bytes=min(int(vmem_bytes), VMEM_LIMIT_V7X))


def _cond_row(tile_idx, tile_rows):
    row0 = tile_idx * tile_rows
    return jnp.where(row0 < M_PROMPT, 0, 1 + (row0 - M_PROMPT) // DEC_SEQ)


def _mod_kernel(c_ref, w_ref, b_ref, o_ref):
    c = c_ref[...]
    a = (c * jax.nn.sigmoid(c)).astype(MXU_DTYPE)
    w = w_ref[...].astype(MXU_DTYPE)
    o_ref[...] = jnp.dot(a, w, preferred_element_type=jnp.float32) + b_ref[...]


def _mod_table(cond, mod_w, mod_b):
    tn = 512
    n = N_MOD * D_MODEL
    return pl.pallas_call(
        _mod_kernel,
        grid=(DEPTH, n // tn),
        in_specs=[pl.BlockSpec((COND_ROWS, D_MODEL), lambda l, j: (0, 0)),
                  pl.BlockSpec((None, D_MODEL, tn), lambda l, j: (l, 0, j)),
                  pl.BlockSpec((None, 1, tn), lambda l, j: (l, 0, j))],
        out_specs=pl.BlockSpec((None, COND_ROWS, tn), lambda l, j: (l, 0, j)),
        out_shape=jax.ShapeDtypeStruct((DEPTH, COND_ROWS, n), jnp.float32),
        compiler_params=_params(2, 3 * D_MODEL * tn * 4 + (4 << 20)),
        name="mod_table",
    )(cond, mod_w, mod_b.reshape(DEPTH, 1, n))


def _dense_kernel(*refs, n_a, n_w, cast, swiglu, resid, w_t):
    a_refs, w_refs = refs[:n_a], refs[n_a:n_a + n_w]
    n_in = n_a + n_w + (2 if resid else 0)
    o_ref = refs[n_in]
    cast_refs = refs[n_in + 1:n_in + n_w + 1] if cast else ()
    accs = []
    for wi, w_ref in enumerate(w_refs):
        if cast:
            w_narrow = w_ref[...].astype(MXU_DTYPE)
            cast_refs[wi][...] = w_narrow
        k0, acc = 0, None
        for a_ref in a_refs:
            kp = a_ref.shape[1]
            if w_t:
                part = _dot_nt(a_ref[...], w_ref[:, k0:k0 + kp])
            else:
                w_rows = w_narrow[k0:k0 + kp, :] if cast else w_ref[k0:k0 + kp, :]
                part = jnp.dot(a_ref[...], w_rows, preferred_element_type=jnp.float32)
            acc = part if acc is None else acc + part
            k0 += kp
        accs.append(acc)
    out = accs[0] * jax.nn.sigmoid(accs[0]) * accs[1] if swiglu else accs[0]
    if resid:
        x_ref, g_ref = refs[n_a + n_w], refs[n_a + n_w + 1]
        tile0, tm, gate_scale = resid
        g = g_ref[pl.ds(_cond_row(tile0 + pl.program_id(0), tm), 1), :]
        out = ALPHA * x_ref[...] + (gate_scale * g) * out
    o_ref[...] = out.astype(o_ref.dtype)


def _dense(a_list, w_list, w_index, *, tile0, n_tiles, tm, tn, out_dtype, name, swiglu=False, cast=False,
           resid=None, w_t=False, a_tile0=None, out_tile0=None, out_rows=None):
    a_tile0 = tile0 if a_tile0 is None else a_tile0
    out_tile0 = tile0 if out_tile0 is None else out_tile0
    out_rows = a_list[0].shape[0] if out_rows is None else out_rows
    k, n = w_list[0].shape[-2:][::-1] if w_t else w_list[0].shape[-2:]
    n_a, n_w, lead = len(a_list), len(w_list), len(w_index)
    assert sum(a.shape[1] for a in a_list) == k and n % tn == 0
    assert (a_tile0 + n_tiles) * tm <= a_list[0].shape[0] and (out_tile0 + n_tiles) * tm <= out_rows
    assert not (w_t and cast)
    a_mode = dict(pipeline_mode=pl.Buffered(1)) if n_tiles == 1 else {}
    in_specs = [pl.BlockSpec((tm, a.shape[1]), lambda i, j: (a_tile0 + i, 0), **a_mode) for a in a_list]
    if w_t:
        in_specs += [pl.BlockSpec((None,) * lead + (tn, k), lambda i, j: tuple(w_index) + (j, 0)) for _ in w_list]
    else:
        in_specs += [pl.BlockSpec((None,) * lead + (k, tn), lambda i, j: tuple(w_index) + (0, j)) for _ in w_list]
    args = list(a_list) + list(w_list)
    aliases = {}
    if resid is not None:
        x, mod, k_gate, gate_scale = resid
        assert x.shape == (out_rows, n) and x.dtype == out_dtype and out_tile0 == tile0
        in_specs += [pl.BlockSpec((tm, tn), lambda i, j: (tile0 + i, j)),
                     pl.BlockSpec((COND_ROWS, tn), lambda i, j: (0, k_gate * (n // tn) + j))]
        args += [x, mod]
        aliases = {n_a + n_w: 0}
    out_specs = [pl.BlockSpec((tm, tn), lambda i, j: (out_tile0 + i, j))]
    out_shape = [jax.ShapeDtypeStruct((out_rows, n), out_dtype)]
    if cast:
        out_specs += [pl.BlockSpec((k, tn), lambda i, j: (0, j)) for _ in w_list]
        out_shape += [jax.ShapeDtypeStruct((k, n), MXU_DTYPE) for _ in w_list]
    esz, wsz, osz = jnp.dtype(MXU_DTYPE).itemsize, w_list[0].dtype.itemsize, jnp.dtype(out_dtype).itemsize
    vmem = ((1 if n_tiles == 1 else 2) * tm * k * esz + 2 * n_w * k * tn * wsz + 2 * tm * tn * osz
            + (n_w + 1) * tm * tn * 4 + (3 * n_w * k * tn * esz if cast else 0)
            + (2 * tm * tn * 4 if resid is not None else 0))
    out = pl.pallas_call(
        functools.partial(_dense_kernel, n_a=n_a, n_w=n_w, cast=cast, swiglu=swiglu,
                          resid=None if resid is None else (tile0, tm, resid[3]), w_t=w_t),
        grid=(n_tiles, n // tn),
        in_specs=in_specs,
        out_specs=out_specs,
        out_shape=out_shape,
        input_output_aliases=aliases,
        compiler_params=_params(2, vmem + (4 << 20)),
        name=name,
    )(*args)
    return out if cast else out[0]


def _modulate_kernel(xp_ref, xs_ref, sc_ref, sh_ref, x_ref, o_ref, *, tm):
    i = pl.program_id(0)
    r = _cond_row(i, tm)
    sc = sc_ref[pl.ds(r, 1), :]
    sh = sh_ref[pl.ds(r, 1), :]

    def emit(x):
        x_ref[...] = x
        o_ref[...] = (x * (1.0 + sc) + sh).astype(o_ref.dtype)

    pl.when(i < M_PROMPT // tm)(lambda: emit(xp_ref[...]))
    pl.when(i >= M_PROMPT // tm)(lambda: emit(xs_ref[...]))


def _mod_spec(k):
    return pl.BlockSpec((COND_ROWS, D_MODEL), lambda i: (0, k))


def _modulate(x_prompt, x_sample, mod, k_scale, k_shift):
    tm = 256
    n_p = M_PROMPT // tm
    row = pl.BlockSpec((tm, D_MODEL), lambda i: (i, 0))
    return pl.pallas_call(
        functools.partial(_modulate_kernel, tm=tm),
        grid=(M_TOK // tm,),
        in_specs=[pl.BlockSpec((tm, D_MODEL), lambda i: (jnp.minimum(i, n_p - 1), 0)),
                  pl.BlockSpec((tm, D_MODEL), lambda i: (jnp.maximum(i - n_p, 0), 0)),
                  _mod_spec(k_scale), _mod_spec(k_shift)],
        out_specs=[row, row],
        out_shape=[jax.ShapeDtypeStruct((M_TOK, D_MODEL), jnp.float32),
                   jax.ShapeDtypeStruct((M_TOK, D_MODEL), MXU_DTYPE)],
        compiler_params=_params(1, 32 << 20),
        name="modulate",
    )(x_prompt.reshape(M_PROMPT, D_MODEL), x_sample.reshape(M_SAMPLE, D_MODEL), mod, mod)


def _ln_kernel(*refs, tm, with_next):
    if with_next:
        y_ref, lg_ref, lb_ref, sc_ref, sh_ref, xo_ref, mo_ref = refs
    else:
        y_ref, lg_ref, lb_ref, yp_ref, ys_ref = refs
    i = pl.program_id(0)
    r = _cond_row(i, tm)
    y = y_ref[...]
    mu = jnp.mean(y, axis=-1, keepdims=True)
    yc = y - mu
    var = jnp.mean(yc * yc, axis=-1, keepdims=True)
    xn = yc * lax.rsqrt(var + LN_EPS) * lg_ref[...] + lb_ref[...]
    if with_next:
        xo_ref[...] = xn
        sc = sc_ref[pl.ds(r, 1), :]
        sh = sh_ref[pl.ds(r, 1), :]
        mo_ref[...] = (xn * (1.0 + sc) + sh).astype(mo_ref.dtype)
    else:
        @pl.when(i < M_PROMPT // tm)
        def _():
            yp_ref[...] = xn

        @pl.when(i >= M_PROMPT // tm)
        def _():
            ys_ref[...] = xn


def _layer_norm(y, ln_g, ln_b, next_mod=None, k_scale=0, k_shift=0):
    tm = 256
    m = y.shape[0]
    n_p = M_PROMPT // tm
    with_next = next_mod is not None
    row = pl.BlockSpec((tm, D_MODEL), lambda i: (i, 0))
    vec = pl.BlockSpec((1, D_MODEL), lambda i: (0, 0))
    in_specs = [row, vec, vec]
    args = [y, ln_g.reshape(1, D_MODEL), ln_b.reshape(1, D_MODEL)]
    if with_next:
        in_specs += [_mod_spec(k_scale), _mod_spec(k_shift)]
        args += [next_mod, next_mod]
        out_specs = [row, row]
        out_shape = [jax.ShapeDtypeStruct((m, D_MODEL), jnp.float32), jax.ShapeDtypeStruct((m, D_MODEL), MXU_DTYPE)]
    else:
        out_specs = [pl.BlockSpec((tm, D_MODEL), lambda i: (jnp.minimum(i, n_p - 1), 0)),
                     pl.BlockSpec((tm, D_MODEL), lambda i: (jnp.maximum(i - n_p, 0), 0))]
        out_shape = [jax.ShapeDtypeStruct((M_PROMPT, D_MODEL), jnp.float32),
                     jax.ShapeDtypeStruct((M_SAMPLE, D_MODEL), jnp.float32)]
    out = pl.pallas_call(
        functools.partial(_ln_kernel, tm=tm, with_next=with_next),
        grid=(m // tm,),
        in_specs=in_specs,
        out_specs=out_specs,
        out_shape=out_shape,
        compiler_params=_params(1, 48 << 20),
        name="layer_norm",
    )(*args)
    return out[0], out[1]


def _row_iota(shape):
    return lax.broadcasted_iota(jnp.int32, shape, 0)


def _lane_iota(shape):
    return lax.broadcasted_iota(jnp.int32, shape, 1)


def _shift_rows(x, s, fill, up=False):
    n = x.shape[0]
    t = _row_iota(x.shape)
    if up:
        return jnp.where(t < n - s, pltpu.roll(x, n - s, 0), fill)
    return jnp.where(t >= s, pltpu.roll(x, s, 0), fill)


def _softplus(z):
    return jnp.maximum(z, 0.0) + jnp.log1p(jnp.exp(-jnp.abs(z)))


def _silu(z):
    return z * jax.nn.sigmoid(z)


def _dot(a, b):
    return jnp.dot(a, b, preferred_element_type=jnp.float32)


def _dot_nt(a, b):
    return lax.dot_general(a, b, (((1,), (1,)), ((), ())), preferred_element_type=jnp.float32)


def _dot_tn(a, b):
    return lax.dot_general(a, b, (((0,), (0,)), ((), ())), preferred_element_type=jnp.float32)


def _split3(x):
    hi = x.astype(MXU_DTYPE)
    r1 = x - hi.astype(jnp.float32)
    mid = r1.astype(MXU_DTYPE)
    lo = (r1 - mid.astype(jnp.float32)).astype(MXU_DTYPE)
    return hi, mid, lo


def _dot01_right(x, e):
    hi, mid, lo = _split3(x)
    return _dot(hi, e) + _dot(mid, e) + _dot(lo, e)


def _dot01_left(e, x):
    hi, mid, lo = _split3(x)
    return _dot(e, hi) + _dot(e, mid) + _dot(e, lo)


def _conv4(x, w_ref, b_ref):
    acc = _shift_rows(x, 2, 0.0) * w_ref[0:1, :]
    acc = acc + _shift_rows(x, 1, 0.0) * w_ref[1:2, :]
    acc = acc + x * w_ref[2:3, :]
    acc = acc + _shift_rows(x, 1, 0.0, up=True) * w_ref[3:4, :]
    return acc + b_ref[...]


def _rms_rows(x, w):
    return x * lax.rsqrt(jnp.mean(x * x, axis=-1, keepdims=True) + RMS_EPS) * w


ATT_STACK = 2


def _rope_tables(n_tok):
    rows = n_tok // GRID_W
    row = jnp.repeat(jnp.arange(rows, dtype=jnp.float32), GRID_W)
    col = jnp.tile(jnp.arange(GRID_W, dtype=jnp.float32), rows)
    n_freq = HEAD_DIM // 4
    inv = ROPE_THETA ** (-jnp.arange(n_freq, dtype=jnp.float32) / n_freq)
    ar, ac = row[:, None] * inv, col[:, None] * inv
    cos = jnp.concatenate([jnp.cos(ar), jnp.cos(ar), jnp.cos(ac), jnp.cos(ac)], axis=1)
    sin = jnp.concatenate([-jnp.sin(ar), jnp.sin(ar), -jnp.sin(ac), jnp.sin(ac)], axis=1)
    return cos, sin


def _rope(x, cos, sin):
    quarter = HEAD_DIM // 4
    lane = _lane_iota(x.shape)
    partner = jnp.where(lane % (2 * quarter) < quarter,
                        pltpu.roll(x, HEAD_DIM - quarter, 1), pltpu.roll(x, quarter, 1))
    return x * cos + partner * sin


def _attn_kernel(*refs, n_tok, tq, n_ctx, rope):
    if rope:
        (q_ref, k_ref, v_ref, qn_ref, kn_ref, cq_ref, sq_ref, ck_ref, sk_ref, xk_ref, xv_ref,
         o_ref, ks_ref, vs_ref) = refs
    else:
        q_ref, k_ref, v_ref, qn_ref, kn_ref, o_ref, ko_ref, ks_ref, vs_ref = refs
    rep = ATT_HEADS // ATT_KV_HEADS

    @pl.when(pl.program_id(2) == 0)
    def _():
        kn = _rms_rows(k_ref[...], kn_ref[...])
        if rope:
            kn = _rope(kn, ck_ref[...], sk_ref[...])
            ks_ref[n_tok:n_tok + n_ctx, :] = xk_ref[...].astype(MXU_DTYPE)
            vs_ref[n_tok:n_tok + n_ctx, :] = xv_ref[...].astype(MXU_DTYPE)
        else:
            ko_ref[...] = kn
        ks_ref[0:n_tok, :] = kn.astype(MXU_DTYPE)
        vs_ref[0:n_tok, :] = v_ref[...].astype(MXU_DTYPE)

    heads = []
    for r in range(rep):
        qh = _rms_rows(q_ref[:, r * HEAD_DIM:(r + 1) * HEAD_DIM], qn_ref[...])
        if rope:
            qh = _rope(qh, cq_ref[...], sq_ref[...])
        heads.append(qh.astype(MXU_DTYPE))
    for r0 in range(0, rep, ATT_STACK):
        qs = jnp.concatenate(heads[r0:r0 + ATT_STACK], axis=0)
        s = _dot_nt(qs, ks_ref[...]) * (HEAD_DIM ** -0.5)
        e = jnp.exp(s - jnp.max(s, axis=-1, keepdims=True))
        o = _dot(e.astype(MXU_DTYPE), vs_ref[...]) / jnp.sum(e, axis=-1, keepdims=True)
        for r in range(ATT_STACK):
            o_ref[:, (r0 + r) * HEAD_DIM:(r0 + r + 1) * HEAD_DIM] = o[r * tq:(r + 1) * tq].astype(o_ref.dtype)


def _attention_call(p, row_off, bsz, n_tok, q_norm, k_norm, ctx_kv, out_prev):
    m = p.shape[0]
    rope = ctx_kv is not None
    tq = 128 if rope else n_tok
    nq = n_tok // tq
    rep = ATT_HEADS // ATT_KV_HEADS
    qw = rep * HEAD_DIM
    n_ctx = ctx_kv[0].shape[1] if rope else 0
    assert row_off % n_tok == 0 and n_tok % tq == 0
    rb, sb = row_off // tq, row_off // n_tok

    def seq_spec(col0):
        return pl.BlockSpec((n_tok, HEAD_DIM), lambda b, g, i: (sb + b, col0 // HEAD_DIM + g))

    vec = pl.BlockSpec((1, HEAD_DIM), lambda b, g, i: (0, 0))
    in_specs = [pl.BlockSpec((tq, qw), lambda b, g, i: (rb + b * nq + i, P_AQ // qw + g)),
                seq_spec(P_AK), seq_spec(P_AV), vec, vec]
    args = [p, p, p, q_norm.reshape(1, HEAD_DIM), k_norm.reshape(1, HEAD_DIM)]
    out_block = pl.BlockSpec((tq, qw), lambda b, g, i: (rb + b * nq + i, g))
    out_specs = [out_block]
    out_shape = [jax.ShapeDtypeStruct((m, GROUP_W), MXU_DTYPE)]
    aliases = {}
    if rope:
        cos, sin = _rope_tables(n_tok)
        in_specs += [pl.BlockSpec((tq, HEAD_DIM), lambda b, g, i: (i, 0))] * 2
        in_specs += [pl.BlockSpec((n_tok, HEAD_DIM), lambda b, g, i: (0, 0))] * 2
        in_specs += [pl.BlockSpec((None, n_ctx, HEAD_DIM), lambda b, g, i: (b, 0, g))] * 2
        args += [cos, sin, cos, sin, ctx_kv[0], ctx_kv[1]]
    else:
        out_specs.append(pl.BlockSpec((n_tok, HEAD_DIM), lambda b, g, i: (b, g)))
        out_shape.append(jax.ShapeDtypeStruct((bsz * n_tok, ATT_KV_HEADS * HEAD_DIM), jnp.float32))
    if out_prev is not None:
        in_specs.append(pl.BlockSpec(memory_space=pl.ANY))
        args.append(out_prev)
        aliases = {len(args) - 1: 0}
    n_keys = n_tok + n_ctx

    def body(*refs):
        if out_prev is not None:
            refs = refs[:len(args) - 1] + refs[len(args):]
        _attn_kernel(*refs, n_tok=n_tok, tq=tq, n_ctx=n_ctx, rope=rope)

    out = pl.pallas_call(
        body,
        grid=(bsz, ATT_KV_HEADS, nq),
        in_specs=in_specs,
        out_specs=out_specs,
        out_shape=out_shape,
        scratch_shapes=[pltpu.VMEM((n_keys, HEAD_DIM), MXU_DTYPE), pltpu.VMEM((n_keys, HEAD_DIM), MXU_DTYPE)],
        input_output_aliases=aliases,
        compiler_params=_params(3, 48 << 20),
        name="attention",
    )(*args)
    return out


def _lru_kernel(*refs, n_tok, with_state):
    if with_state:
        (x_ref, g_ref, cw_ref, cb_ref, wa_ref, ba_ref, wx_ref, bx_ref, lam_ref, h0_ref, o_ref) = refs
    else:
        (x_ref, g_ref, cw_ref, cb_ref, wa_ref, ba_ref, wx_ref, bx_ref, lam_ref, o_ref, hT_ref) = refs
    xl = _conv4(x_ref[...], cw_ref, cb_ref)
    xb = xl.astype(MXU_DTYPE)
    t = _row_iota(xl.shape)
    h_sum = None
    for d in range(2):
        up = d == 1
        r = jax.nn.sigmoid(_dot(xb, wa_ref[d]) + ba_ref[d])
        i = jax.nn.sigmoid(_dot(xb, wx_ref[d]) + bx_ref[d])
        log_a = -LRU_C * r * _softplus(-lam_ref[d])
        a = jnp.exp(log_a)
        th = jnp.tanh(log_a)
        u = jnp.sqrt(-2.0 * th / (1.0 - th)) * i * xl
        if with_state:
            first = (t == n_tok - 1) if up else (t == 0)
            u = jnp.where(first, u + a * h0_ref[d], u)
        s = 1
        while s < n_tok:
            u = u + a * _shift_rows(u, s, 0.0, up=up)
            a = a * _shift_rows(a, s, 1.0, up=up)
            s *= 2
        h_sum = u if h_sum is None else h_sum + u
        if not with_state:
            hT_ref[d] = u[0:1, :] if up else u[n_tok - 1:n_tok, :]
    o_ref[...] = (h_sum * jax.nn.gelu(g_ref[...])).astype(o_ref.dtype)


def _lru_call(p, row_off, bsz, n_tok, lp, h0, out_prev):
    m = p.shape[0]
    with_state = h0 is not None
    sb = row_off // n_tok
    assert row_off % n_tok == 0

    def seq_spec(col0):
        return pl.BlockSpec((n_tok, LRU_BW), lambda b, n: (sb + b, col0 // LRU_BW + n))

    def par2(shape):
        return pl.BlockSpec((2,) + shape + (LRU_BW,), lambda b, n: (0,) + (0,) * len(shape) + (n,))

    in_specs = [seq_spec(P_LX), seq_spec(P_LG),
                pl.BlockSpec((CONV_W, LRU_BW), lambda b, n: (0, n)),
                pl.BlockSpec((1, LRU_BW), lambda b, n: (0, n)),
                pl.BlockSpec((2, None, LRU_BW, LRU_BW), lambda b, n: (0, n, 0, 0)), par2((1,)),
                pl.BlockSpec((2, None, LRU_BW, LRU_BW), lambda b, n: (0, n, 0, 0)), par2((1,)), par2((1,))]
    args = [p, p, lp['lru_conv_w'], lp['lru_conv_b'].reshape(1, LRU_W),
            lp['lru_wa'].astype(MXU_DTYPE), lp['lru_ba'].reshape(2, 1, LRU_W),
            lp['lru_wx'].astype(MXU_DTYPE), lp['lru_bx'].reshape(2, 1, LRU_W), lp['lru_lam'].reshape(2, 1, LRU_W)]
    out_specs = [pl.BlockSpec((n_tok, LRU_BW), lambda b, n: (sb + b, n))]
    out_shape = [jax.ShapeDtypeStruct((m, GROUP_W), MXU_DTYPE)]
    if with_state:
        in_specs.append(pl.BlockSpec((None, 2, 1, LRU_BW), lambda b, n: (b, 0, 0, n)))
        args.append(h0.reshape(bsz, 2, 1, LRU_W))
    else:
        out_specs.append(pl.BlockSpec((None, 2, 1, LRU_BW), lambda b, n: (b, 0, 0, n)))
        out_shape.append(jax.ShapeDtypeStruct((bsz, 2, 1, LRU_W), jnp.float32))
    n_in = len(args)
    aliases = {}
    if out_prev is not None:
        in_specs.append(pl.BlockSpec(memory_space=pl.ANY))
        args.append(out_prev)
        aliases = {n_in: 0}

    def body(*refs):
        if out_prev is not None:
            refs = refs[:n_in] + refs[n_in + 1:]
        _lru_kernel(*refs, n_tok=n_tok, with_state=with_state)

    return pl.pallas_call(
        body,
        grid=(bsz, LRU_BLOCKS),
        in_specs=in_specs,
        out_specs=out_specs,
        out_shape=out_shape,
        input_output_aliases=aliases,
        compiler_params=_params(2, 48 << 20),
        name="rg_lru",
    )(*args)


SSD_L = 128
SSD_GH = SSD_HEADS // SSD_GROUPS
SSD_GW = SSD_GH * SSD_HEADDIM


def _ssd_kernel(*refs, n_tok, with_state):
    if with_state:
        (x_ref, b_ref, c_ref, s_ref, cwx_ref, cbx_ref, cwb_ref, cbb_ref, cwc_ref, cbc_ref, dtb_ref, alog_ref,
         dskip_ref, h0_ref, y_ref, xc_s, xdt_s, xte_s, dec_s, bc_s, cc_s, cum_s, cumt_s, h_s) = refs
    else:
        (x_ref, b_ref, c_ref, s_ref, cwx_ref, cbx_ref, cwb_ref, cbb_ref, cwc_ref, cbc_ref, dtb_ref, alog_ref,
         dskip_ref, y_ref, hT_ref, xc_s, xdt_s, xte_s, dec_s, bc_s, cc_s, cum_s, cumt_s, h_s) = refs
    L = SSD_L
    n_chunk = n_tok // L
    g = pl.program_id(1)

    xc_s[...] = _silu(_conv4(x_ref[...], cwx_ref, cbx_ref))
    bc_s[...] = _silu(_conv4(b_ref[...], cwb_ref, cbb_ref)).astype(MXU_DTYPE)
    cc_s[...] = _silu(_conv4(c_ref[...], cwc_ref, cbc_ref)).astype(MXU_DTYPE)
    dt = _softplus(s_ref[...] + dtb_ref[...])
    y_ref[...] = xc_s[...] * dskip_ref[...]

    e8 = (_row_iota((S_W, SSD_GW)) == _lane_iota((S_W, SSD_GW)) // SSD_HEADDIM).astype(MXU_DTYPE)
    ri, ci = _row_iota((L, L)), _lane_iota((L, L))
    lane_w = _lane_iota((L, 2 * SSD_HEADDIM))

    t = _row_iota((n_tok, S_W))
    for d in range(2):
        up = d == 1
        end = 0 if up else L - 1
        sel = (_row_iota((S_W, S_W)) == _lane_iota((S_W, S_W)) + (d * SSD_HEADS + g * SSD_GH)).astype(MXU_DTYPE)
        sel = jnp.where(_lane_iota((S_W, S_W)) < SSD_GH, sel, jnp.zeros_like(sel))
        dt_sel = _dot01_right(dt, sel)
        a_row = -jnp.exp(_dot01_right(alog_ref[...], sel))
        cum = dt_sel * a_row
        s = 1
        while s < L:
            ok = (t % L < L - s) if up else (t % L >= s)
            cum = cum + jnp.where(ok, pltpu.roll(cum, (n_tok - s) if up else s, 0), 0.0)
            s *= 2
        cum_s[...] = cum
        cumt_s[...] = jnp.swapaxes(cum.reshape(n_chunk, L, S_W), 1, 2).reshape(n_chunk * S_W, L)
        cum_i = _dot01_right(cum, e8).reshape(n_chunk, L, SSD_GW)
        xdt = xc_s[...] * _dot01_right(dt_sel, e8)
        xdt_s[...] = xdt.astype(MXU_DTYPE)
        to_end = jnp.exp(cum_i[:, end:end + 1, :] - cum_i)
        xte_s[...] = (xdt.reshape(n_chunk, L, SSD_GW) * to_end).reshape(n_tok, SSD_GW).astype(MXU_DTYPE)
        dec_s[...] = jnp.exp(cum_i).reshape(n_tok, SSD_GW)
        if with_state:
            h_s[...] = h0_ref[d].reshape(SSD_GW, SSD_STATE)
        else:
            h_s[...] = jnp.zeros_like(h_s)
        tri = ((ci >= ri) if up else (ci <= ri))

        def chunk(i, carry):
            c = (n_chunk - 1 - i) if up else i
            t0 = pl.multiple_of(c * L, L)
            rows = pl.ds(t0, L)
            cum = cum_s[rows, :]
            cum_t = cumt_s[rows, :]
            bcv, ccv = bc_s[rows, :], cc_s[rows, :]
            cb = _dot_nt(ccv, bcv)
            h_b16 = h_s[...].astype(MXU_DTYPE)
            y = _dot_nt(ccv, h_b16) * dec_s[rows, :]
            xdt_b16 = xdt_s[rows, :]
            pieces = []
            for pair in range(SSD_GH // 2):
                xp = xdt_b16[:, pair * 2 * SSD_HEADDIM:(pair + 1) * 2 * SSD_HEADDIM]
                acc = None
                for sub in range(2):
                    hh = 2 * pair + sub
                    seg = jnp.where(tri, jnp.exp(cum[:, hh:hh + 1] - cum_t[hh:hh + 1, :]), 0.0)
                    sc = (cb * seg).astype(MXU_DTYPE)
                    half = (lane_w // SSD_HEADDIM) == sub
                    part = _dot(sc, jnp.where(half, xp, jnp.zeros_like(xp)))
                    acc = part if acc is None else acc + part
                pieces.append(acc)
            y = y + jnp.concatenate(pieces, axis=1)
            y_ref[rows, :] += y
            upd = _dot_tn(xte_s[rows, :], bcv)
            for hh in range(SSD_GH):
                blk = slice(hh * SSD_HEADDIM, (hh + 1) * SSD_HEADDIM)
                h_s[blk, :] = h_s[blk, :] * jnp.exp(cum_t[hh:hh + 1, end:end + 1]) + upd[blk, :]
            return carry

        lax.fori_loop(0, n_chunk, chunk, 0, unroll=2)
        if not with_state:
            hT_ref[d] = h_s[...].reshape(SSD_GH, SSD_HEADDIM, SSD_STATE)


def _ssd_call(p, s, row_off, bsz, n_tok, lp, h0):
    with_state = h0 is not None
    sb = row_off // n_tok
    assert row_off % n_tok == 0 and n_tok % SSD_L == 0
    assert SSD_L == S_W

    def seq_spec(col0, w):
        return pl.BlockSpec((n_tok, w), lambda b, g: (sb + b, col0 // w + g))

    def conv_specs(col0, w):
        return [pl.BlockSpec((CONV_W, w), lambda b, g: (0, col0 // w + g)),
                pl.BlockSpec((1, w), lambda b, g: (0, col0 // w + g))]

    small = pl.BlockSpec((1, S_W), lambda b, g: (0, 0))
    pad = jnp.zeros((S_W - 2 * SSD_HEADS,), jnp.float32)
    dtb = jnp.concatenate([lp['ssd_dt_bias'].reshape(-1), pad]).reshape(1, S_W)
    alog = jnp.concatenate([lp['ssd_a_log'].reshape(-1), pad]).reshape(1, S_W)
    dskip = jnp.repeat(lp['ssd_d'][0] + lp['ssd_d'][1], SSD_HEADDIM).reshape(1, GROUP_W)
    cw, cb = lp['ssd_conv_w'], lp['ssd_conv_b'].reshape(1, -1)
    in_specs = ([seq_spec(P_SX, SSD_GW), seq_spec(P_SB, SSD_STATE), seq_spec(P_SC, SSD_STATE),
                 pl.BlockSpec((n_tok, S_W), lambda b, g: (sb + b, 0))]
                + conv_specs(0, SSD_GW) + conv_specs(GROUP_W, SSD_STATE)
                + conv_specs(GROUP_W + SSD_GROUPS * SSD_STATE, SSD_STATE)
                + [small, small, pl.BlockSpec((1, SSD_GW), lambda b, g: (0, g))])
    args = [p, p, p, s, cw, cb, cw, cb, cw, cb, dtb, alog, dskip]
    state_block = pl.BlockSpec((None, 2, SSD_GH, SSD_HEADDIM, SSD_STATE), lambda b, g: (b, 0, g, 0, 0))
    out_specs = [pl.BlockSpec((n_tok, SSD_GW), lambda b, g: (b, g))]
    out_shape = [jax.ShapeDtypeStruct((bsz * n_tok, GROUP_W), jnp.float32)]
    if with_state:
        in_specs.append(state_block)
        args.append(h0)
    else:
        out_specs.append(state_block)
        out_shape.append(jax.ShapeDtypeStruct((bsz, 2, SSD_HEADS, SSD_HEADDIM, SSD_STATE), jnp.float32))
    return pl.pallas_call(
        functools.partial(_ssd_kernel, n_tok=n_tok, with_state=with_state),
        grid=(bsz, SSD_GROUPS),
        in_specs=in_specs,
        out_specs=out_specs,
        out_shape=out_shape,
        scratch_shapes=[pltpu.VMEM((n_tok, SSD_GW), jnp.float32), pltpu.VMEM((n_tok, SSD_GW), MXU_DTYPE),
                        pltpu.VMEM((n_tok, SSD_GW), MXU_DTYPE), pltpu.VMEM((n_tok, SSD_GW), jnp.float32),
                        pltpu.VMEM((n_tok, SSD_STATE), MXU_DTYPE), pltpu.VMEM((n_tok, SSD_STATE), MXU_DTYPE),
                        pltpu.VMEM((n_tok, S_W), jnp.float32), pltpu.VMEM((n_tok, SSD_L), jnp.float32),
                        pltpu.VMEM((SSD_GW, SSD_STATE), jnp.float32)],
        compiler_params=_params(2, VMEM_LIMIT_V7X),
        name="ssd_scan",
    )(*args)


def _gated_norm_kernel(y_ref, z_ref, w_ref, o_ref):
    o_ref[...] = _rms_rows(y_ref[...] * _silu(z_ref[...]), w_ref[...]).astype(o_ref.dtype)


def _ssd_finish(y, p, row_off, norm_w, out_prev):
    tm = 256
    m, n = p.shape[0], y.shape[0]
    rb = row_off // tm
    in_specs = [pl.BlockSpec((tm, GROUP_W), lambda i: (i, 0)),
                pl.BlockSpec((tm, GROUP_W), lambda i: (rb + i, P_SZ // GROUP_W)),
                pl.BlockSpec((1, GROUP_W), lambda i: (0, 0))]
    args = [y, p, norm_w.reshape(1, GROUP_W)]
    aliases = {}
    if out_prev is not None:
        in_specs.append(pl.BlockSpec(memory_space=pl.ANY))
        args.append(out_prev)
        aliases = {3: 0}

    def body(y_ref, z_ref, w_ref, *rest):
        _gated_norm_kernel(y_ref, z_ref, w_ref, rest[-1])

    return pl.pallas_call(
        body,
        grid=(n // tm,),
        in_specs=in_specs,
        out_specs=pl.BlockSpec((tm, GROUP_W), lambda i: (rb + i, 0)),
        out_shape=jax.ShapeDtypeStruct((m, GROUP_W), MXU_DTYPE),
        input_output_aliases=aliases,
        compiler_params=_params(1, 32 << 20),
        name="ssd_gated_norm",
    )(*args)


GLA_L = 16
GLA_GROUP = 16
LOG2_E = 1.4426950408889634


def _gla_kernel(*refs, n_tok, with_state):
    if with_state:
        (q_ref, k_ref, v_ref, gg_ref, s_ref, gw_ref, gb_ref, nw_ref, s0_ref, o_ref,
         dec_s, qe_s, ke_s, att_s, o_s, st_s) = refs
    else:
        (q_ref, k_ref, v_ref, gg_ref, s_ref, gw_ref, gb_ref, nw_ref, o_ref, sT_ref,
         dec_s, qe_s, ke_s, att_s, o_s, st_s) = refs
    L = GLA_L
    n_blk = n_tok // L
    blk3 = (n_blk, L, GLA_DK)
    q3 = (q_ref[...] * (GLA_DK ** -0.5)).reshape(blk3)
    k3 = k_ref[...].reshape(blk3)
    sb16 = s_ref[...].astype(MXU_DTYPE)
    t = _row_iota((n_tok, GLA_DK))
    ri3 = lax.broadcasted_iota(jnp.int32, blk3, 1)
    lane3 = lax.broadcasted_iota(jnp.int32, blk3, 2)

    for d in range(2):
        up = d == 1
        end = 0 if up else L - 1
        gate = _dot(sb16, gw_ref[d]) + gb_ref[d]
        b = -_softplus(-gate) / GLA_NORMALIZER
        s = 1
        while s < L:
            ok = (t % L < L - s) if up else (t % L >= s)
            b = b + jnp.where(ok, pltpu.roll(b, (n_tok - s) if up else s, 0), 0.0)
            s *= 2
        b3 = b.reshape(blk3)
        b_end = b3[:, end:end + 1, :]
        dec_s[d] = jnp.exp(jnp.broadcast_to(b_end, blk3)).reshape(n_tok, GLA_DK)
        qe_s[d] = (q3 * jnp.exp(b3)).reshape(n_tok, GLA_DK).astype(MXU_DTYPE)
        ke_s[d] = (k3 * jnp.exp(b_end - b3)).reshape(n_tok, GLA_DK).astype(MXU_DTYPE)
        code = jnp.where((ri3 <= lane3) if up else (ri3 >= lane3), lane3, -1)
        b2 = b3 * LOG2_E
        half = L // 2
        att_h = [jnp.zeros((n_blk, half, GLA_DK), jnp.float32) for _ in range(2)]
        for j in range(L):
            for h in range(2):
                if (h == 0 and j >= half) if not up else (h == 1 and j < half):
                    continue
                rows = slice(h * half, (h + 1) * half)
                w = jnp.exp2(b2[:, rows, :] - b2[:, j:j + 1, :]) * q3[:, rows, :] * k3[:, j:j + 1, :]
                att_h[h] = jnp.where(code[:, rows, :] == j, jnp.sum(w, axis=-1, keepdims=True), att_h[h])
        att = jnp.concatenate(att_h, axis=1)
        att_s[d] = att.reshape(n_tok, GLA_DK).astype(MXU_DTYPE)
        if with_state:
            st_s[d] = s0_ref[d].T
        else:
            st_s[d] = jnp.zeros((GLA_DV, GLA_DK), jnp.float32)

    o_s[...] = jnp.zeros_like(o_s)
    n_grp = n_blk // GLA_GROUP
    rows_per_trip = GLA_GROUP * L

    def group(i, carry):
        for d in range(2):
            up = d == 1
            c = (n_grp - 1 - i) if up else i
            rows = pl.ds(pl.multiple_of(c * rows_per_trip, rows_per_trip), rows_per_trip)
            vg = v_ref[rows, :].astype(MXU_DTYPE)
            keg, attg, qeg, decg = ke_s[d, rows, :], att_s[d, rows, 0:L], qe_s[d, rows, :], dec_s[d, rows, :]
            order = range(GLA_GROUP - 1, -1, -1) if up else range(GLA_GROUP)
            upd = {k: _dot_tn(vg[k * L:(k + 1) * L], keg[k * L:(k + 1) * L]) for k in order}
            intra = {k: _dot(attg[k * L:(k + 1) * L], vg[k * L:(k + 1) * L]) for k in order}
            st = st_s[d]
            outs = {}
            for k in order:
                outs[k] = _dot_nt(qeg[k * L:(k + 1) * L], st.astype(MXU_DTYPE)) + intra[k]
                st = st * decg[k * L:k * L + 1, :] + upd[k]
            st_s[d] = st
            o_s[rows, :] += jnp.concatenate([outs[k] for k in range(GLA_GROUP)], axis=0)
        return carry

    lax.fori_loop(0, n_grp, group, 0)
    if not with_state:
        for d in range(2):
            sT_ref[d] = st_s[d].T
    o = _rms_rows(o_s[...], nw_ref[...]) * _silu(gg_ref[...])
    o_ref[...] = o.astype(o_ref.dtype)


def _gla_call(p, s, row_off, bsz, n_tok, lp, s0, out_prev):
    m = p.shape[0]
    with_state = s0 is not None
    sb = row_off // n_tok
    assert row_off % n_tok == 0 and n_tok % (GLA_L * GLA_GROUP) == 0

    def seq_spec(col0, w):
        return pl.BlockSpec((n_tok, w), lambda b, h: (sb + b, col0 // w + h))

    gw = lp['gla_gate_w'].reshape(2, GLA_RANK, GLA_HEADS, GLA_DK).transpose(0, 2, 1, 3)
    gw_rows = jnp.zeros((2, GLA_HEADS, S_W, GLA_DK), jnp.float32)
    for d in range(2):
        r0 = 2 * SSD_HEADS + d * GLA_RANK
        gw_rows = gw_rows.at[d, :, r0:r0 + GLA_RANK, :].set(gw[d])
    in_specs = [seq_spec(P_GQ, GLA_DK), seq_spec(P_GK, GLA_DK), seq_spec(P_GV, GLA_DV), seq_spec(P_GG, GLA_DV),
                pl.BlockSpec((n_tok, S_W), lambda b, h: (sb + b, 0)),
                pl.BlockSpec((2, None, S_W, GLA_DK), lambda b, h: (0, h, 0, 0)),
                pl.BlockSpec((2, None, 1, GLA_DK), lambda b, h: (0, h, 0, 0)),
                pl.BlockSpec((1, GLA_DV), lambda b, h: (0, 0))]
    args = [p, p, p, p, s, gw_rows.astype(MXU_DTYPE), lp['gla_gate_b'].reshape(2, GLA_HEADS, 1, GLA_DK),
            lp['gla_norm_w'].reshape(1, GLA_DV)]
    state_block = pl.BlockSpec((None, 2, None, GLA_DK, GLA_DV), lambda b, h: (b, 0, h, 0, 0))
    out_specs = [pl.BlockSpec((n_tok, GLA_DV), lambda b, h: (sb + b, h))]
    out_shape = [jax.ShapeDtypeStruct((m, GROUP_W), MXU_DTYPE)]
    if with_state:
        in_specs.append(state_block)
        args.append(s0)
    else:
        out_specs.append(state_block)
        out_shape.append(jax.ShapeDtypeStruct((bsz, 2, GLA_HEADS, GLA_DK, GLA_DV), jnp.float32))
    n_in = len(args)
    aliases = {}
    if out_prev is not None:
        in_specs.append(pl.BlockSpec(memory_space=pl.ANY))
        args.append(out_prev)
        aliases = {n_in: 0}

    def body(*refs):
        if out_prev is not None:
            refs = refs[:n_in] + refs[n_in + 1:]
        _gla_kernel(*refs, n_tok=n_tok, with_state=with_state)

    return pl.pallas_call(
        body,
        grid=(bsz, GLA_HEADS),
        in_specs=in_specs,
        out_specs=out_specs,
        out_shape=out_shape,
        scratch_shapes=[pltpu.VMEM((2, n_tok, GLA_DK), jnp.float32), pltpu.VMEM((2, n_tok, GLA_DK), MXU_DTYPE),
                        pltpu.VMEM((2, n_tok, GLA_DK), MXU_DTYPE), pltpu.VMEM((2, n_tok, GLA_DK), MXU_DTYPE),
                        pltpu.VMEM((n_tok, GLA_DV), jnp.float32), pltpu.VMEM((2, GLA_DV, GLA_DK), jnp.float32)],
        input_output_aliases=aliases,
        compiler_params=_params(2, 48 << 20),
        name="gla_scan",
    )(*args)


def _mixers(p, s, lp, cached, donors):
    ctx_k, ctx_v, ssd0, gla0, lru0 = cached
    kv_w = ATT_KV_HEADS * HEAD_DIM
    att, k_new = _attention_call(p, 0, BATCH, SEQ, lp['q_norm'], lp['k_norm'], None, donors[0])
    att, = _attention_call(p, M_PROMPT, DEC_BATCH, DEC_SEQ, lp['q_norm'], lp['k_norm'],
                           (ctx_k.reshape(DEC_BATCH, PAST_LEN, kv_w), ctx_v.reshape(DEC_BATCH, PAST_LEN, kv_w)), att)
    v_new = p[:M_PROMPT, P_AV:P_AV + kv_w]

    y_p, ssd_new = _ssd_call(p, s, 0, BATCH, SEQ, lp, None)
    y_s, = _ssd_call(p, s, M_PROMPT, DEC_BATCH, DEC_SEQ, lp, ssd0)
    ssd = _ssd_finish(y_p, p, 0, lp['ssd_norm_w'], donors[1])
    ssd = _ssd_finish(y_s, p, M_PROMPT, lp['ssd_norm_w'], ssd)

    gla, gla_new = _gla_call(p, s, 0, BATCH, SEQ, lp, None, donors[2])
    gla, = _gla_call(p, s, M_PROMPT, DEC_BATCH, DEC_SEQ, lp, gla0, gla)

    lru, lru_new = _lru_call(p, 0, BATCH, SEQ, lp, None, donors[3])
    lru, = _lru_call(p, M_PROMPT, DEC_BATCH, DEC_SEQ, lp, lru0, lru)

    new_ctx = (k_new.reshape(BATCH, SEQ, ATT_KV_HEADS, HEAD_DIM), v_new.reshape(BATCH, SEQ, ATT_KV_HEADS, HEAD_DIM),
               ssd_new, gla_new, lru_new.reshape(BATCH, 2, LRU_W))
    return [att, ssd, gla, lru], new_ctx


IN_W = 9280
_W_IN_MAIN = ((0, P_AQ, 1024), (1024, P_AK, 256), (1280, P_AV, 256), (1536, P_SX, 1024), (2560, P_SZ, 1024),
              (3584, P_SB, 256), (3840, P_SC, 256), (4128, P_GQ, 512), (4640, P_GK, 512), (5152, P_GV, 1024),
              (6208, P_GG, 1024), (7232, P_LX, 1024), (8256, P_LG, 1024))
_W_IN_SMALL = ((4096, 0, 2 * SSD_HEADS), (6176, 2 * SSD_HEADS, 2 * GLA_RANK))


def _w_in_kernel(w_ref, main_ref, small_ref):
    for src, dst, width in _W_IN_MAIN:
        main_ref[dst:dst + width, :] = w_ref[src:src + width, :].astype(main_ref.dtype)
    small_ref[...] = jnp.zeros_like(small_ref)
    for src, dst, width in _W_IN_SMALL:
        small_ref[dst:dst + width, :] = w_ref[src:src + width, :].astype(small_ref.dtype)


def _reorder_w_in(w_t, l):
    tk = 256
    return pl.pallas_call(
        _w_in_kernel,
        grid=(D_MODEL // tk,),
        in_specs=[pl.BlockSpec((None, IN_W, tk), lambda i: (l, 0, i))],
        out_specs=[pl.BlockSpec((P_W, tk), lambda i: (0, i)), pl.BlockSpec((S_W, tk), lambda i: (0, i))],
        out_shape=[jax.ShapeDtypeStruct((P_W, D_MODEL), MXU_DTYPE), jax.ShapeDtypeStruct((S_W, D_MODEL), MXU_DTYPE)],
        compiler_params=_params(1, 48 << 20),
        name="w_in_regroup",
    )(w_t)


def kernel(x_prompt, x_sample, c, cache_attn_k, cache_attn_v, state_ssd, state_gla, state_lru, c_ctx,
           mod_w, mod_b, ln_g, ln_b, ffn_w_gate, ffn_w_up, ffn_w_down, w_in, w_out, q_norm, k_norm,
           ssd_conv_w, ssd_conv_b, ssd_a_log, ssd_dt_bias, ssd_d, ssd_norm_w,
           gla_gate_w, gla_gate_b, gla_norm_w,
           lru_conv_w, lru_conv_b, lru_wa, lru_ba, lru_wx, lru_bx, lru_lam):
    cond = jnp.concatenate([c_ctx[None], c, jnp.zeros((COND_ROWS - N_COND, D_MODEL), jnp.float32)], axis=0)
    mod_all = _mod_table(cond, mod_w, mod_b)

    def ffn(x, xm, mod, k_gate, l, half):
        up, rows0 = dict(tm=2048, tn=256, out_dtype=MXU_DTYPE, swiglu=True), 2048
        h0, wg, wu = _dense([xm], [ffn_w_gate, ffn_w_up], (l, half), tile0=0, n_tiles=1, cast=True, out_rows=rows0,
                            name="ffn_up_first", **up)
        h1 = _dense([xm], [wg, wu], (), tile0=1, n_tiles=(M_TOK - rows0) // up['tm'], out_tile0=0,
                    out_rows=M_TOK - rows0, name="ffn_up", **up)
        down = dict(tm=512, out_dtype=jnp.float32)
        split = rows0 // down['tm']
        y, wd = _dense([h0], [ffn_w_down], (l, half), tile0=0, n_tiles=1, tn=256, cast=True, out_rows=M_TOK,
                       resid=(x, mod, k_gate, 0.5), name="ffn_down_first", **down)
        y = _dense([h0], [wd], (), tile0=1, n_tiles=split - 1, tn=512, out_rows=M_TOK,
                   resid=(y, mod, k_gate, 0.5), name="ffn_down_head", **down)
        return _dense([h1], [wd], (), tile0=split, a_tile0=0, n_tiles=M_TOK // down['tm'] - split, tn=512,
                      out_rows=M_TOK, resid=(y, mod, k_gate, 0.5), name="ffn_down", **down)

    spare_mix = [jnp.zeros((M_TOK, GROUP_W), MXU_DTYPE)] * 4
    x, xm = _modulate(x_prompt, x_sample, mod_all[0], 1, 0)
    w_in_t = jnp.swapaxes(w_in, 1, 2)
    ctx_out = []
    for l in range(DEPTH):
        mod = mod_all[l]
        lp = dict(q_norm=q_norm[l], k_norm=k_norm[l], ssd_conv_w=ssd_conv_w[l], ssd_conv_b=ssd_conv_b[l],
                  ssd_a_log=ssd_a_log[l], ssd_dt_bias=ssd_dt_bias[l], ssd_d=ssd_d[l], ssd_norm_w=ssd_norm_w[l],
                  gla_gate_w=gla_gate_w[l], gla_gate_b=gla_gate_b[l], gla_norm_w=gla_norm_w[l],
                  lru_conv_w=lru_conv_w[l], lru_conv_b=lru_conv_b[l], lru_wa=lru_wa[l], lru_ba=lru_ba[l],
                  lru_wx=lru_wx[l], lru_bx=lru_bx[l], lru_lam=lru_lam[l])
        y = ffn(x, xm, mod, 2, l, 0)
        x, xm = _layer_norm(y, ln_g[l, 0], ln_b[l, 0], mod, 4, 3)
        w_main, w_small = _reorder_w_in(w_in_t, l)
        dense = dict(tile0=0, n_tiles=M_TOK // 1024, tm=1024, out_dtype=jnp.float32, w_t=True)
        p = _dense([xm], [w_main], (), tn=1024, name="proj_in", **dense)
        s = _dense([xm], [w_small], (), tn=S_W, name="proj_in_small", **dense)
        cached = (cache_attn_k[:, l], cache_attn_v[:, l], state_ssd[:, l], state_gla[:, l], state_lru[:, l])
        mix, new_ctx = _mixers(p, s, lp, cached, spare_mix)
        ctx_out.append(new_ctx)
        out = dict(tm=1024, out_dtype=jnp.float32)
        y, wo = _dense(mix, [w_out], (l,), tile0=0, n_tiles=1, tn=512, cast=True, resid=(x, mod, 5, 1.0),
                       name="proj_out_first", **out)
        y = _dense(mix, [wo], (), tile0=1, n_tiles=M_TOK // out['tm'] - 1, tn=1024, resid=(y, mod, 5, 1.0),
                   name="proj_out", **out)
        spare_mix = mix
        x, xm = _layer_norm(y, ln_g[l, 1], ln_b[l, 1], mod, 7, 6)
        y = ffn(x, xm, mod, 8, l, 1)
        if l + 1 < DEPTH:
            x, xm = _layer_norm(y, ln_g[l, 2], ln_b[l, 2], mod_all[l + 1], 1, 0)
        else:
            y_prompt, y_sample = _layer_norm(y, ln_g[l, 2], ln_b[l, 2])

    y_prompt = y_prompt.reshape(BATCH, SEQ, D_MODEL)
    y_sample = y_sample.reshape(DEC_BATCH, DEC_SEQ, D_MODEL)
    new_k, new_v, new_ssd, new_gla, new_lru = (jnp.stack([s_[i] for s_ in ctx_out], axis=1) for i in range(5))
    return (y_prompt, y_sample, new_k, new_v, new_ssd, new_gla, new_lru)
```

```python
import functools

import jax
import jax.numpy as jnp
from jax import lax
from jax.experimental import pallas as pl
from jax.experimental.pallas import tpu as pltpu

D_MODEL = 4096
BATCH = 16
SEQ = 256
DEPTH = 2
DEC_BATCH = 4
DEC_SEQ = 2048
PAST_LEN = 256
GRID_W = 64
GROUP_W = 1024
HEAD_DIM = 128
ATT_HEADS = 8
ATT_KV_HEADS = 2
ROPE_THETA = 10000.0
SSD_HEADDIM = 64
SSD_HEADS = 16
SSD_STATE = 128
SSD_GROUPS = 2
CONV_W = 4
GLA_HEADS = 4
GLA_DK = 128
GLA_DV = 256
GLA_RANK = 16
GLA_NORMALIZER = 16.0
LRU_W = 1024
LRU_BLOCKS = 8
LRU_BW = 128
LRU_C = 8.0
N_MOD = 9
LN_EPS = 1e-5
RMS_EPS = 1e-6
ALPHA = (2.0 * DEPTH) ** 0.25

M_PROMPT = BATCH * SEQ
M_SAMPLE = DEC_BATCH * DEC_SEQ
M_TOK = M_PROMPT + M_SAMPLE
N_COND = 1 + DEC_BATCH
COND_ROWS = 8

VMEM_LIMIT_V7X = 56 * 1024 * 1024

MXU_DTYPE = jnp.bfloat16

P_AQ, P_SX, P_SZ, P_GV, P_GG, P_LX, P_LG = 0, 1024, 2048, 3072, 4096, 5120, 6144
P_GQ, P_GK = 7168, 7680
P_AK, P_AV, P_SB, P_SC = 8192, 8448, 8704, 8960
P_W = 9216
S_W = 128


def _params(n_axes, vmem_bytes):
    return pltpu.CompilerParams(dimension_semantics=("arbitrary",) * n_axes,
                                vmem_limit_bytes=min(int(vmem_bytes), VMEM_LIMIT_V7X))


def _cond_row(tile_idx, tile_rows):
    row0 = tile_idx * tile_rows
    return jnp.where(row0 < M_PROMPT, 0, 1 + (row0 - M_PROMPT) // DEC_SEQ)


def _mod_kernel(c_ref, w_ref, b_ref, o_ref):
    c = c_ref[...]
    a = (c * jax.nn.sigmoid(c)).astype(MXU_DTYPE)
    w = w_ref[...].astype(MXU_DTYPE)
    o_ref[...] = jnp.dot(a, w, preferred_element_type=jnp.float32) + b_ref[...]


def _mod_table(cond, mod_w, mod_b):
    tn = 1024
    n = N_MOD * D_MODEL
    return pl.pallas_call(
        _mod_kernel,
        grid=(DEPTH, n // tn),
        in_specs=[pl.BlockSpec((COND_ROWS, D_MODEL), lambda l, j: (0, 0)),
                  pl.BlockSpec((None, D_MODEL, tn), lambda l, j: (l, 0, j)),
                  pl.BlockSpec((None, 1, tn), lambda l, j: (l, 0, j))],
        out_specs=pl.BlockSpec((None, COND_ROWS, tn), lambda l, j: (l, 0, j)),
        out_shape=jax.ShapeDtypeStruct((DEPTH, COND_ROWS, n), jnp.float32),
        compiler_params=_params(2, 3 * D_MODEL * tn * 4 + (4 << 20)),
        name="mod_table",
    )(cond, mod_w, mod_b.reshape(DEPTH, 1, n))


def _dense_kernel(*refs, n_a, n_w, cast, swiglu, resid, w_t):
    a_refs, w_refs = refs[:n_a], refs[n_a:n_a + n_w]
    n_in = n_a + n_w + (2 if resid else 0)
    o_ref = refs[n_in]
    cast_refs = refs[n_in + 1:n_in + n_w + 1] if cast else ()
    accs = []
    for wi, w_ref in enumerate(w_refs):
        if cast:
            w_narrow = w_ref[...].astype(MXU_DTYPE)
            cast_refs[wi][...] = w_narrow
        k0, acc = 0, None
        for a_ref in a_refs:
            kp = a_ref.shape[1]
            if w_t:
                part = _dot_nt(a_ref[...], w_ref[:, k0:k0 + kp])
            else:
                w_rows = w_narrow[k0:k0 + kp, :] if cast else w_ref[k0:k0 + kp, :]
                part = jnp.dot(a_ref[...], w_rows, preferred_element_type=jnp.float32)
            acc = part if acc is None else acc + part
            k0 += kp
        accs.append(acc)
    out = accs[0] * jax.nn.sigmoid(accs[0]) * accs[1] if swiglu else accs[0]
    if resid:
        x_ref, g_ref = refs[n_a + n_w], refs[n_a + n_w + 1]
        tile0, tm, gate_scale = resid
        g = g_ref[pl.ds(_cond_row(tile0 + pl.program_id(0), tm), 1), :]
        out = ALPHA * x_ref[...] + (gate_scale * g) * out
    o_ref[...] = out.astype(o_ref.dtype)


def _dense(a_list, w_list, w_index, *, tile0, n_tiles, tm, tn, out_dtype, name, swiglu=False, cast=False,
           resid=None, w_t=False, a_tile0=None, out_tile0=None, out_rows=None):
    a_tile0 = tile0 if a_tile0 is None else a_tile0
    out_tile0 = tile0 if out_tile0 is None else out_tile0
    out_rows = a_list[0].shape[0] if out_rows is None else out_rows
    k, n = w_list[0].shape[-2:][::-1] if w_t else w_list[0].shape[-2:]
    n_a, n_w, lead = len(a_list), len(w_list), len(w_index)
    assert sum(a.shape[1] for a in a_list) == k and n % tn == 0
    assert (a_tile0 + n_tiles) * tm <= a_list[0].shape[0] and (out_tile0 + n_tiles) * tm <= out_rows
    assert not (w_t and cast)
    a_mode = dict(pipeline_mode=pl.Buffered(1)) if n_tiles == 1 else {}
    in_specs = [pl.BlockSpec((tm, a.shape[1]), lambda i, j: (a_tile0 + i, 0), **a_mode) for a in a_list]
    if w_t:
        in_specs += [pl.BlockSpec((None,) * lead + (tn, k), lambda i, j: tuple(w_index) + (j, 0)) for _ in w_list]
    else:
        in_specs += [pl.BlockSpec((None,) * lead + (k, tn), lambda i, j: tuple(w_index) + (0, j)) for _ in w_list]
    args = list(a_list) + list(w_list)
    aliases = {}
    if resid is not None:
        x, mod, k_gate, gate_scale = resid
        assert x.shape == (out_rows, n) and x.dtype == out_dtype and out_tile0 == tile0
        in_specs += [pl.BlockSpec((tm, tn), lambda i, j: (tile0 + i, j)),
                     pl.BlockSpec((COND_ROWS, tn), lambda i, j: (0, k_gate * (n // tn) + j))]
        args += [x, mod]
        aliases = {n_a + n_w: 0}
    out_specs = [pl.BlockSpec((tm, tn), lambda i, j: (out_tile0 + i, j))]
    out_shape = [jax.ShapeDtypeStruct((out_rows, n), out_dtype)]
    if cast:
        out_specs += [pl.BlockSpec((k, tn), lambda i, j: (0, j)) for _ in w_list]
        out_shape += [jax.ShapeDtypeStruct((k, n), MXU_DTYPE) for _ in w_list]
    esz, wsz, osz = jnp.dtype(MXU_DTYPE).itemsize, w_list[0].dtype.itemsize, jnp.dtype(out_dtype).itemsize
    vmem = ((1 if n_tiles == 1 else 2) * tm * k * esz + 2 * n_w * k * tn * wsz + 2 * tm * tn * osz
            + (n_w + 1) * tm * tn * 4 + (3 * n_w * k * tn * esz if cast else 0)
            + (2 * tm * tn * 4 if resid is not None else 0))
    out = pl.pallas_call(
        functools.partial(_dense_kernel, n_a=n_a, n_w=n_w, cast=cast, swiglu=swiglu,
                          resid=None if resid is None else (tile0, tm, resid[3]), w_t=w_t),
        grid=(n_tiles, n // tn),
        in_specs=in_specs,
        out_specs=out_specs,
        out_shape=out_shape,
        input_output_aliases=aliases,
        compiler_params=_params(2, vmem + (4 << 20)),
        name=name,
    )(*args)
    return out if cast else out[0]


def _modulate_kernel(xp_ref, xs_ref, sc_ref, sh_ref, x_ref, o_ref, *, tm):
    i = pl.program_id(0)
    r = _cond_row(i, tm)
    sc = sc_ref[pl.ds(r, 1), :]
    sh = sh_ref[pl.ds(r, 1), :]

    def emit(x):
        x_ref[...] = x
        o_ref[...] = (x * (1.0 + sc) + sh).astype(o_ref.dtype)

    pl.when(i < M_PROMPT // tm)(lambda: emit(xp_ref[...]))
    pl.when(i >= M_PROMPT // tm)(lambda: emit(xs_ref[...]))


def _mod_spec(k):
    return pl.BlockSpec((COND_ROWS, D_MODEL), lambda i: (0, k))


def _modulate(x_prompt, x_sample, mod, k_scale, k_shift):
    tm = 256
    n_p = M_PROMPT // tm
    row = pl.BlockSpec((tm, D_MODEL), lambda i: (i, 0))
    return pl.pallas_call(
        functools.partial(_modulate_kernel, tm=tm),
        grid=(M_TOK // tm,),
        in_specs=[pl.BlockSpec((tm, D_MODEL), lambda i: (jnp.minimum(i, n_p - 1), 0)),
                  pl.BlockSpec((tm, D_MODEL), lambda i: (jnp.maximum(i - n_p, 0), 0)),
                  _mod_spec(k_scale), _mod_spec(k_shift)],
        out_specs=[row, row],
        out_shape=[jax.ShapeDtypeStruct((M_TOK, D_MODEL), jnp.float32),
                   jax.ShapeDtypeStruct((M_TOK, D_MODEL), MXU_DTYPE)],
        compiler_params=_params(1, 32 << 20),
        name="modulate",
    )(x_prompt.reshape(M_PROMPT, D_MODEL), x_sample.reshape(M_SAMPLE, D_MODEL), mod, mod)


def _ln_kernel(*refs, tm, with_next):
    if with_next:
        y_ref, lg_ref, lb_ref, sc_ref, sh_ref, xo_ref, mo_ref = refs
    else:
        y_ref, lg_ref, lb_ref, yp_ref, ys_ref = refs
    i = pl.program_id(0)
    r = _cond_row(i, tm)
    y = y_ref[...]
    mu = jnp.mean(y, axis=-1, keepdims=True)
    yc = y - mu
    var = jnp.mean(yc * yc, axis=-1, keepdims=True)
    xn = yc * lax.rsqrt(var + LN_EPS) * lg_ref[...] + lb_ref[...]
    if with_next:
        xo_ref[...] = xn
        sc = sc_ref[pl.ds(r, 1), :]
        sh = sh_ref[pl.ds(r, 1), :]
        mo_ref[...] = (xn * (1.0 + sc) + sh).astype(mo_ref.dtype)
    else:
        @pl.when(i < M_PROMPT // tm)
        def _():
            yp_ref[...] = xn

        @pl.when(i >= M_PROMPT // tm)
        def _():
            ys_ref[...] = xn


def _layer_norm(y, ln_g, ln_b, next_mod=None, k_scale=0, k_shift=0):
    tm = 256
    m = y.shape[0]
    n_p = M_PROMPT // tm
    with_next = next_mod is not None
    row = pl.BlockSpec((tm, D_MODEL), lambda i: (i, 0))
    vec = pl.BlockSpec((1, D_MODEL), lambda i: (0, 0))
    in_specs = [row, vec, vec]
    args = [y, ln_g.reshape(1, D_MODEL), ln_b.reshape(1, D_MODEL)]
    if with_next:
        in_specs += [_mod_spec(k_scale), _mod_spec(k_shift)]
        args += [next_mod, next_mod]
        out_specs = [row, row]
        out_shape = [jax.ShapeDtypeStruct((m, D_MODEL), jnp.float32), jax.ShapeDtypeStruct((m, D_MODEL), MXU_DTYPE)]
    else:
        out_specs = [pl.BlockSpec((tm, D_MODEL), lambda i: (jnp.minimum(i, n_p - 1), 0)),
                     pl.BlockSpec((tm, D_MODEL), lambda i: (jnp.maximum(i - n_p, 0), 0))]
        out_shape = [jax.ShapeDtypeStruct((M_PROMPT, D_MODEL), jnp.float32),
                     jax.ShapeDtypeStruct((M_SAMPLE, D_MODEL), jnp.float32)]
    out = pl.pallas_call(
        functools.partial(_ln_kernel, tm=tm, with_next=with_next),
        grid=(m // tm,),
        in_specs=in_specs,
        out_specs=out_specs,
        out_shape=out_shape,
        compiler_params=_params(1, 48 << 20),
        name="layer_norm",
    )(*args)
    return out[0], out[1]


def _row_iota(shape):
    return lax.broadcasted_iota(jnp.int32, shape, 0)


def _lane_iota(shape):
    return lax.broadcasted_iota(jnp.int32, shape, 1)


def _shift_rows(x, s, fill, up=False):
    n = x.shape[0]
    t = _row_iota(x.shape)
    if up:
        return jnp.where(t < n - s, pltpu.roll(x, n - s, 0), fill)
    return jnp.where(t >= s, pltpu.roll(x, s, 0), fill)


def _softplus(z):
    return jnp.maximum(z, 0.0) + jnp.log1p(jnp.exp(-jnp.abs(z)))


def _silu(z):
    return z * jax.nn.sigmoid(z)


def _dot(a, b):
    return jnp.dot(a, b, preferred_element_type=jnp.float32)


def _dot_nt(a, b):
    return lax.dot_general(a, b, (((1,), (1,)), ((), ())), preferred_element_type=jnp.float32)


def _dot_tn(a, b):
    return lax.dot_general(a, b, (((0,), (0,)), ((), ())), preferred_element_type=jnp.float32)


def _split3(x):
    hi = x.astype(MXU_DTYPE)
    r1 = x - hi.astype(jnp.float32)
    mid = r1.astype(MXU_DTYPE)
    lo = (r1 - mid.astype(jnp.float32)).astype(MXU_DTYPE)
    return hi, mid, lo


def _dot01_right(x, e):
    hi, mid, lo = _split3(x)
    return _dot(hi, e) + _dot(mid, e) + _dot(lo, e)


def _conv4(x, w_ref, b_ref):
    acc = _shift_rows(x, 2, 0.0) * w_ref[0:1, :]
    acc = acc + _shift_rows(x, 1, 0.0) * w_ref[1:2, :]
    acc = acc + x * w_ref[2:3, :]
    acc = acc + _shift_rows(x, 1, 0.0, up=True) * w_ref[3:4, :]
    return acc + b_ref[...]


def _rms_rows(x, w):
    return x * lax.rsqrt(jnp.mean(x * x, axis=-1, keepdims=True) + RMS_EPS) * w


ATT_STACK = 2


def _rope_tables(n_tok):
    rows = n_tok // GRID_W
    row = jnp.repeat(jnp.arange(rows, dtype=jnp.float32), GRID_W)
    col = jnp.tile(jnp.arange(GRID_W, dtype=jnp.float32), rows)
    n_freq = HEAD_DIM // 4
    inv = ROPE_THETA ** (-jnp.arange(n_freq, dtype=jnp.float32) / n_freq)
    ar, ac = row[:, None] * inv, col[:, None] * inv
    cos = jnp.concatenate([jnp.cos(ar), jnp.cos(ar), jnp.cos(ac), jnp.cos(ac)], axis=1)
    sin = jnp.concatenate([-jnp.sin(ar), jnp.sin(ar), -jnp.sin(ac), jnp.sin(ac)], axis=1)
    return cos, sin


def _rope(x, cos, sin):
    quarter = HEAD_DIM // 4
    lane = _lane_iota(x.shape)
    partner = jnp.where(lane % (2 * quarter) < quarter,
                        pltpu.roll(x, HEAD_DIM - quarter, 1), pltpu.roll(x, quarter, 1))
    return x * cos + partner * sin


def _attn_kernel(*refs, n_tok, tq, n_ctx, rope):
    if rope:
        (q_ref, k_ref, v_ref, qn_ref, kn_ref, cq_ref, sq_ref, ck_ref, sk_ref, xk_ref, xv_ref,
         o_ref, ks_ref, vs_ref) = refs
    else:
        q_ref, k_ref, v_ref, qn_ref, kn_ref, o_ref, ko_ref, ks_ref, vs_ref = refs
    rep = ATT_HEADS // ATT_KV_HEADS

    @pl.when(pl.program_id(2) == 0)
    def _():
        kn = _rms_rows(k_ref[...], kn_ref[...])
        if rope:
            kn = _rope(kn, ck_ref[...], sk_ref[...])
            ks_ref[n_tok:n_tok + n_ctx, :] = xk_ref[...].astype(MXU_DTYPE)
            vs_ref[n_tok:n_tok + n_ctx, :] = xv_ref[...].astype(MXU_DTYPE)
        else:
            ko_ref[...] = kn
        ks_ref[0:n_tok, :] = kn.astype(MXU_DTYPE)
        vs_ref[0:n_tok, :] = v_ref[...].astype(MXU_DTYPE)

    heads = []
    for r in range(rep):
        qh = _rms_rows(q_ref[:, r * HEAD_DIM:(r + 1) * HEAD_DIM], qn_ref[...])
        if rope:
            qh = _rope(qh, cq_ref[...], sq_ref[...])
        heads.append(qh.astype(MXU_DTYPE))
    for r0 in range(0, rep, ATT_STACK):
        qs = jnp.concatenate(heads[r0:r0 + ATT_STACK], axis=0)
        s = _dot_nt(qs, ks_ref[...]) * (HEAD_DIM ** -0.5)
        e = jnp.exp(s - jnp.max(s, axis=-1, keepdims=True))
        o = _dot(e.astype(MXU_DTYPE), vs_ref[...]) / jnp.sum(e, axis=-1, keepdims=True)
        for r in range(ATT_STACK):
            o_ref[:, (r0 + r) * HEAD_DIM:(r0 + r + 1) * HEAD_DIM] = o[r * tq:(r + 1) * tq].astype(o_ref.dtype)


def _attention_call(p, row_off, bsz, n_tok, q_norm, k_norm, ctx_kv, out_prev):
    m = p.shape[0]
    rope = ctx_kv is not None
    tq = 128 if rope else n_tok
    nq = n_tok // tq
    rep = ATT_HEADS // ATT_KV_HEADS
    qw = rep * HEAD_DIM
    n_ctx = ctx_kv[0].shape[1] if rope else 0
    assert row_off % n_tok == 0 and n_tok % tq == 0
    rb, sb = row_off // tq, row_off // n_tok

    def seq_spec(col0):
        return pl.BlockSpec((n_tok, HEAD_DIM), lambda b, g, i: (sb + b, col0 // HEAD_DIM + g))

    vec = pl.BlockSpec((1, HEAD_DIM), lambda b, g, i: (0, 0))
    in_specs = [pl.BlockSpec((tq, qw), lambda b, g, i: (rb + b * nq + i, P_AQ // qw + g)),
                seq_spec(P_AK), seq_spec(P_AV), vec, vec]
    args = [p, p, p, q_norm.reshape(1, HEAD_DIM), k_norm.reshape(1, HEAD_DIM)]
    out_block = pl.BlockSpec((tq, qw), lambda b, g, i: (rb + b * nq + i, g))
    out_specs = [out_block]
    out_shape = [jax.ShapeDtypeStruct((m, GROUP_W), MXU_DTYPE)]
    aliases = {}
    if rope:
        cos, sin = _rope_tables(n_tok)
        in_specs += [pl.BlockSpec((tq, HEAD_DIM), lambda b, g, i: (i, 0))] * 2
        in_specs += [pl.BlockSpec((n_tok, HEAD_DIM), lambda b, g, i: (0, 0))] * 2
        in_specs += [pl.BlockSpec((None, n_ctx, HEAD_DIM), lambda b, g, i: (b, 0, g))] * 2
        args += [cos, sin, cos, sin, ctx_kv[0], ctx_kv[1]]
    else:
        out_specs.append(pl.BlockSpec((n_tok, HEAD_DIM), lambda b, g, i: (b, g)))
        out_shape.append(jax.ShapeDtypeStruct((bsz * n_tok, ATT_KV_HEADS * HEAD_DIM), jnp.float32))
    if out_prev is not None:
        in_specs.append(pl.BlockSpec(memory_space=pl.ANY))
        args.append(out_prev)
        aliases = {len(args) - 1: 0}
    n_keys = n_tok + n_ctx

    def body(*refs):
        if out_prev is not None:
            refs = refs[:len(args) - 1] + refs[len(args):]
        _attn_kernel(*refs, n_tok=n_tok, tq=tq, n_ctx=n_ctx, rope=rope)

    out = pl.pallas_call(
        body,
        grid=(bsz, ATT_KV_HEADS, nq),
        in_specs=in_specs,
        out_specs=out_specs,
        out_shape=out_shape,
        scratch_shapes=[pltpu.VMEM((n_keys, HEAD_DIM), MXU_DTYPE), pltpu.VMEM((n_keys, HEAD_DIM), MXU_DTYPE)],
        input_output_aliases=aliases,
        compiler_params=_params(3, 48 << 20),
        name="attention",
    )(*args)
    return out


def _lru_kernel(*refs, n_tok, with_state):
    if with_state:
        (x_ref, g_ref, cw_ref, cb_ref, wa_ref, ba_ref, wx_ref, bx_ref, lam_ref, h0_ref, o_ref) = refs
    else:
        (x_ref, g_ref, cw_ref, cb_ref, wa_ref, ba_ref, wx_ref, bx_ref, lam_ref, o_ref, hT_ref) = refs
    xl = _conv4(x_ref[...], cw_ref, cb_ref)
    xb = xl.astype(MXU_DTYPE)
    t = _row_iota(xl.shape)
    h_sum = None
    for d in range(2):
        up = d == 1
        r = jax.nn.sigmoid(_dot(xb, wa_ref[d]) + ba_ref[d])
        i = jax.nn.sigmoid(_dot(xb, wx_ref[d]) + bx_ref[d])
        log_a = -LRU_C * r * _softplus(-lam_ref[d])
        a = jnp.exp(log_a)
        th = jnp.tanh(log_a)
        u = jnp.sqrt(-2.0 * th / (1.0 - th)) * i * xl
        if with_state:
            first = (t == n_tok - 1) if up else (t == 0)
            u = jnp.where(first, u + a * h0_ref[d], u)
        s = 1
        while s < n_tok:
            u = u + a * _shift_rows(u, s, 0.0, up=up)
            a = a * _shift_rows(a, s, 1.0, up=up)
            s *= 2
        h_sum = u if h_sum is None else h_sum + u
        if not with_state:
            hT_ref[d] = u[0:1, :] if up else u[n_tok - 1:n_tok, :]
    o_ref[...] = (h_sum * jax.nn.gelu(g_ref[...])).astype(o_ref.dtype)


def _lru_call(p, row_off, bsz, n_tok, lp, h0, out_prev):
    m = p.shape[0]
    with_state = h0 is not None
    sb = row_off // n_tok
    assert row_off % n_tok == 0

    def seq_spec(col0):
        return pl.BlockSpec((n_tok, LRU_BW), lambda b, n: (sb + b, col0 // LRU_BW + n))

    def par2(shape):
        return pl.BlockSpec((2,) + shape + (LRU_BW,), lambda b, n: (0,) + (0,) * len(shape) + (n,))

    in_specs = [seq_spec(P_LX), seq_spec(P_LG),
                pl.BlockSpec((CONV_W, LRU_BW), lambda b, n: (0, n)),
                pl.BlockSpec((1, LRU_BW), lambda b, n: (0, n)),
                pl.BlockSpec((2, None, LRU_BW, LRU_BW), lambda b, n: (0, n, 0, 0)), par2((1,)),
                pl.BlockSpec((2, None, LRU_BW, LRU_BW), lambda b, n: (0, n, 0, 0)), par2((1,)), par2((1,))]
    args = [p, p, lp['lru_conv_w'], lp['lru_conv_b'].reshape(1, LRU_W),
            lp['lru_wa'].astype(MXU_DTYPE), lp['lru_ba'].reshape(2, 1, LRU_W),
            lp['lru_wx'].astype(MXU_DTYPE), lp['lru_bx'].reshape(2, 1, LRU_W), lp['lru_lam'].reshape(2, 1, LRU_W)]
    out_specs = [pl.BlockSpec((n_tok, LRU_BW), lambda b, n: (sb + b, n))]
    out_shape = [jax.ShapeDtypeStruct((m, GROUP_W), MXU_DTYPE)]
    if with_state:
        in_specs.append(pl.BlockSpec((None, 2, 1, LRU_BW), lambda b, n: (b, 0, 0, n)))
        args.append(h0.reshape(bsz, 2, 1, LRU_W))
    else:
        out_specs.append(pl.BlockSpec((None, 2, 1, LRU_BW), lambda b, n: (b, 0, 0, n)))
        out_shape.append(jax.ShapeDtypeStruct((bsz, 2, 1, LRU_W), jnp.float32))
    n_in = len(args)
    aliases = {}
    if out_prev is not None:
        in_specs.append(pl.BlockSpec(memory_space=pl.ANY))
        args.append(out_prev)
        aliases = {n_in: 0}

    def body(*refs):
        if out_prev is not None:
            refs = refs[:n_in] + refs[n_in + 1:]
        _lru_kernel(*refs, n_tok=n_tok, with_state=with_state)

    return pl.pallas_call(
        body,
        grid=(bsz, LRU_BLOCKS),
        in_specs=in_specs,
        out_specs=out_specs,
        out_shape=out_shape,
        input_output_aliases=aliases,
        compiler_params=_params(2, 48 << 20),
        name="rg_lru",
    )(*args)


SSD_L = 128
SSD_GH = SSD_HEADS // SSD_GROUPS
SSD_GW = SSD_GH * SSD_HEADDIM


def _ssd_kernel(*refs, n_tok, with_state):
    if with_state:
        (x_ref, b_ref, c_ref, s_ref, cwx_ref, cbx_ref, cwb_ref, cbb_ref, cwc_ref, cbc_ref, dtb_ref, alog_ref,
         dskip_ref, h0_ref, y_ref, xc_s, xdt_s, xte_s, dec_s, bc_s, cc_s, cum_s, cumt_s, h_s) = refs
    else:
        (x_ref, b_ref, c_ref, s_ref, cwx_ref, cbx_ref, cwb_ref, cbb_ref, cwc_ref, cbc_ref, dtb_ref, alog_ref,
         dskip_ref, y_ref, hT_ref, xc_s, xdt_s, xte_s, dec_s, bc_s, cc_s, cum_s, cumt_s, h_s) = refs
    L = SSD_L
    n_chunk = n_tok // L
    g = pl.program_id(1)

    xc_s[...] = _silu(_conv4(x_ref[...], cwx_ref, cbx_ref))
    bc_s[...] = _silu(_conv4(b_ref[...], cwb_ref, cbb_ref)).astype(MXU_DTYPE)
    cc_s[...] = _silu(_conv4(c_ref[...], cwc_ref, cbc_ref)).astype(MXU_DTYPE)
    dt = _softplus(s_ref[...] + dtb_ref[...])
    y_ref[...] = xc_s[...] * dskip_ref[...]

    e8 = (_row_iota((S_W, SSD_GW)) == _lane_iota((S_W, SSD_GW)) // SSD_HEADDIM).astype(MXU_DTYPE)
    ri, ci = _row_iota((L, L)), _lane_iota((L, L))
    lane_w = _lane_iota((L, 2 * SSD_HEADDIM))

    t = _row_iota((n_tok, S_W))
    for d in range(2):
        up = d == 1
        end = 0 if up else L - 1
        sel = (_row_iota((S_W, S_W)) == _lane_iota((S_W, S_W)) + (d * SSD_HEADS + g * SSD_GH)).astype(MXU_DTYPE)
        sel = jnp.where(_lane_iota((S_W, S_W)) < SSD_GH, sel, jnp.zeros_like(sel))
        dt_sel = _dot01_right(dt, sel)
        a_row = -jnp.exp(_dot01_right(alog_ref[...], sel))
        cum = dt_sel * a_row
        s = 1
        while s < L:
            ok = (t % L < L - s) if up else (t % L >= s)
            cum = cum + jnp.where(ok, pltpu.roll(cum, (n_tok - s) if up else s, 0), 0.0)
            s *= 2
        cum_s[...] = cum
        cumt_s[...] = jnp.swapaxes(cum.reshape(n_chunk, L, S_W), 1, 2).reshape(n_chunk * S_W, L)
        cum_i = _dot01_right(cum, e8).reshape(n_chunk, L, SSD_GW)
        xdt = xc_s[...] * _dot01_right(dt_sel, e8)
        xdt_s[...] = xdt.astype(MXU_DTYPE)
        to_end = jnp.exp(cum_i[:, end:end + 1, :] - cum_i)
        xte_s[...] = (xdt.reshape(n_chunk, L, SSD_GW) * to_end).reshape(n_tok, SSD_GW).astype(MXU_DTYPE)
        dec_s[...] = jnp.exp(cum_i).reshape(n_tok, SSD_GW)
        if with_state:
            h_s[...] = h0_ref[d].reshape(SSD_GW, SSD_STATE)
        else:
            h_s[...] = jnp.zeros_like(h_s)
        tri = ((ci >= ri) if up else (ci <= ri))

        def chunk(i, carry):
            c = (n_chunk - 1 - i) if up else i
            t0 = pl.multiple_of(c * L, L)
            rows = pl.ds(t0, L)
            cum = cum_s[rows, :]
            cum_t = cumt_s[rows, :]
            bcv, ccv = bc_s[rows, :], cc_s[rows, :]
            cb = _dot_nt(ccv, bcv)
            h_b16 = h_s[...].astype(MXU_DTYPE)
            y = _dot_nt(ccv, h_b16) * dec_s[rows, :]
            xdt_b16 = xdt_s[rows, :]
            pieces = []
            for pair in range(SSD_GH // 2):
                xp = xdt_b16[:, pair * 2 * SSD_HEADDIM:(pair + 1) * 2 * SSD_HEADDIM]
                acc = None
                for sub in range(2):
                    hh = 2 * pair + sub
                    seg = jnp.where(tri, jnp.exp(cum[:, hh:hh + 1] - cum_t[hh:hh + 1, :]), 0.0)
                    sc = (cb * seg).astype(MXU_DTYPE)
                    half = (lane_w // SSD_HEADDIM) == sub
                    part = _dot(sc, jnp.where(half, xp, jnp.zeros_like(xp)))
                    acc = part if acc is None else acc + part
                pieces.append(acc)
            y = y + jnp.concatenate(pieces, axis=1)
            y_ref[rows, :] += y
            upd = _dot_tn(xte_s[rows, :], bcv)
            for hh in range(SSD_GH):
                blk = slice(hh * SSD_HEADDIM, (hh + 1) * SSD_HEADDIM)
                h_s[blk, :] = h_s[blk, :] * jnp.exp(cum_t[hh:hh + 1, end:end + 1]) + upd[blk, :]
            return carry

        lax.fori_loop(0, n_chunk, chunk, 0, unroll=2)
        if not with_state:
            hT_ref[d] = h_s[...].reshape(SSD_GH, SSD_HEADDIM, SSD_STATE)


def _ssd_call(p, s, row_off, bsz, n_tok, lp, h0):
    with_state = h0 is not None
    sb = row_off // n_tok
    assert row_off % n_tok == 0 and n_tok % SSD_L == 0
    assert SSD_L == S_W

    def seq_spec(col0, w):
        return pl.BlockSpec((n_tok, w), lambda b, g: (sb + b, col0 // w + g))

    def conv_specs(col0, w):
        return [pl.BlockSpec((CONV_W, w), lambda b, g: (0, col0 // w + g)),
                pl.BlockSpec((1, w), lambda b, g: (0, col0 // w + g))]

    small = pl.BlockSpec((1, S_W), lambda b, g: (0, 0))
    pad = jnp.zeros((S_W - 2 * SSD_HEADS,), jnp.float32)
    dtb = jnp.concatenate([lp['ssd_dt_bias'].reshape(-1), pad]).reshape(1, S_W)
    alog = jnp.concatenate([lp['ssd_a_log'].reshape(-1), pad]).reshape(1, S_W)
    dskip = jnp.repeat(lp['ssd_d'][0] + lp['ssd_d'][1], SSD_HEADDIM).reshape(1, GROUP_W)
    cw, cb = lp['ssd_conv_w'], lp['ssd_conv_b'].reshape(1, -1)
    in_specs = ([seq_spec(P_SX, SSD_GW), seq_spec(P_SB, SSD_STATE), seq_spec(P_SC, SSD_STATE),
                 pl.BlockSpec((n_tok, S_W), lambda b, g: (sb + b, 0))]
                + conv_specs(0, SSD_GW) + conv_specs(GROUP_W, SSD_STATE)
                + conv_specs(GROUP_W + SSD_GROUPS * SSD_STATE, SSD_STATE)
                + [small, small, pl.BlockSpec((1, SSD_GW), lambda b, g: (0, g))])
    args = [p, p, p, s, cw, cb, cw, cb, cw, cb, dtb, alog, dskip]
    state_block = pl.BlockSpec((None, 2, SSD_GH, SSD_HEADDIM, SSD_STATE), lambda b, g: (b, 0, g, 0, 0))
    out_specs = [pl.BlockSpec((n_tok, SSD_GW), lambda b, g: (b, g))]
    out_shape = [jax.ShapeDtypeStruct((bsz * n_tok, GROUP_W), jnp.float32)]
    if with_state:
        in_specs.append(state_block)
        args.append(h0)
    else:
        out_specs.append(state_block)
        out_shape.append(jax.ShapeDtypeStruct((bsz, 2, SSD_HEADS, SSD_HEADDIM, SSD_STATE), jnp.float32))
    return pl.pallas_call(
        functools.partial(_ssd_kernel, n_tok=n_tok, with_state=with_state),
        grid=(bsz, SSD_GROUPS),
        in_specs=in_specs,
        out_specs=out_specs,
        out_shape=out_shape,
        scratch_shapes=[pltpu.VMEM((n_tok, SSD_GW), jnp.float32), pltpu.VMEM((n_tok, SSD_GW), MXU_DTYPE),
                        pltpu.VMEM((n_tok, SSD_GW), MXU_DTYPE), pltpu.VMEM((n_tok, SSD_GW), jnp.float32),
                        pltpu.VMEM((n_tok, SSD_STATE), MXU_DTYPE), pltpu.VMEM((n_tok, SSD_STATE), MXU_DTYPE),
                        pltpu.VMEM((n_tok, S_W), jnp.float32), pltpu.VMEM((n_tok, SSD_L), jnp.float32),
                        pltpu.VMEM((SSD_GW, SSD_STATE), jnp.float32)],
        compiler_params=_params(2, VMEM_LIMIT_V7X),
        name="ssd_scan",
    )(*args)


def _gated_norm_kernel(y_ref, z_ref, w_ref, o_ref):
    o_ref[...] = _rms_rows(y_ref[...] * _silu(z_ref[...]), w_ref[...]).astype(o_ref.dtype)


def _ssd_finish(y, p, row_off, norm_w, out_prev):
    tm = 256
    m, n = p.shape[0], y.shape[0]
    rb = row_off // tm
    in_specs = [pl.BlockSpec((tm, GROUP_W), lambda i: (i, 0)),
                pl.BlockSpec((tm, GROUP_W), lambda i: (rb + i, P_SZ // GROUP_W)),
                pl.BlockSpec((1, GROUP_W), lambda i: (0, 0))]
    args = [y, p, norm_w.reshape(1, GROUP_W)]
    aliases = {}
    if out_prev is not None:
        in_specs.append(pl.BlockSpec(memory_space=pl.ANY))
        args.append(out_prev)
        aliases = {3: 0}

    def body(y_ref, z_ref, w_ref, *rest):
        _gated_norm_kernel(y_ref, z_ref, w_ref, rest[-1])

    return pl.pallas_call(
        body,
        grid=(n // tm,),
        in_specs=in_specs,
        out_specs=pl.BlockSpec((tm, GROUP_W), lambda i: (rb + i, 0)),
        out_shape=jax.ShapeDtypeStruct((m, GROUP_W), MXU_DTYPE),
        input_output_aliases=aliases,
        compiler_params=_params(1, 32 << 20),
        name="ssd_gated_norm",
    )(*args)


GLA_L = 16
GLA_GROUP = 16
LOG2_E = 1.4426950408889634


def _gla_kernel(*refs, n_tok, with_state):
    if with_state:
        (q_ref, k_ref, v_ref, gg_ref, s_ref, gw_ref, gb_ref, nw_ref, s0_ref, o_ref,
         dec_s, qe_s, ke_s, att_s, o_s, st_s) = refs
    else:
        (q_ref, k_ref, v_ref, gg_ref, s_ref, gw_ref, gb_ref, nw_ref, o_ref, sT_ref,
         dec_s, qe_s, ke_s, att_s, o_s, st_s) = refs
    L = GLA_L
    n_blk = n_tok // L
    blk3 = (n_blk, L, GLA_DK)
    q3 = (q_ref[...] * (GLA_DK ** -0.5)).reshape(blk3)
    k3 = k_ref[...].reshape(blk3)
    sb16 = s_ref[...].astype(MXU_DTYPE)
    t = _row_iota((n_tok, GLA_DK))
    ri3 = lax.broadcasted_iota(jnp.int32, blk3, 1)
    lane3 = lax.broadcasted_iota(jnp.int32, blk3, 2)

    for d in range(2):
        up = d == 1
        end = 0 if up else L - 1
        gate = _dot(sb16, gw_ref[d]) + gb_ref[d]
        b = -_softplus(-gate) / GLA_NORMALIZER
        s = 1
        while s < L:
            ok = (t % L < L - s) if up else (t % L >= s)
            b = b + jnp.where(ok, pltpu.roll(b, (n_tok - s) if up else s, 0), 0.0)
            s *= 2
        b3 = b.reshape(blk3)
        b_end = b3[:, end:end + 1, :]
        dec_s[d] = jnp.exp(jnp.broadcast_to(b_end, blk3)).reshape(n_tok, GLA_DK)
        qe_s[d] = (q3 * jnp.exp(b3)).reshape(n_tok, GLA_DK).astype(MXU_DTYPE)
        ke_s[d] = (k3 * jnp.exp(b_end - b3)).reshape(n_tok, GLA_DK).astype(MXU_DTYPE)
        code = jnp.where((ri3 <= lane3) if up else (ri3 >= lane3), lane3, -1)
        b2 = b3 * LOG2_E
        half = L // 2
        att_h = [jnp.zeros((n_blk, half, GLA_DK), jnp.float32) for _ in range(2)]
        for j in range(L):
            for h in range(2):
                if (h == 0 and j >= half) if not up else (h == 1 and j < half):
                    continue
                rows = slice(h * half, (h + 1) * half)
                w = jnp.exp2(b2[:, rows, :] - b2[:, j:j + 1, :]) * q3[:, rows, :] * k3[:, j:j + 1, :]
                att_h[h] = jnp.where(code[:, rows, :] == j, jnp.sum(w, axis=-1, keepdims=True), att_h[h])
        att = jnp.concatenate(att_h, axis=1)
        att_s[d] = att.reshape(n_tok, GLA_DK).astype(MXU_DTYPE)
        if with_state:
            st_s[d] = s0_ref[d].T
        else:
            st_s[d] = jnp.zeros((GLA_DV, GLA_DK), jnp.float32)

    o_s[...] = jnp.zeros_like(o_s)
    n_grp = n_blk // GLA_GROUP
    rows_per_trip = GLA_GROUP * L

    def group(i, carry):
        for d in range(2):
            up = d == 1
            c = (n_grp - 1 - i) if up else i
            rows = pl.ds(pl.multiple_of(c * rows_per_trip, rows_per_trip), rows_per_trip)
            vg = v_ref[rows, :].astype(MXU_DTYPE)
            keg, attg, qeg, decg = ke_s[d, rows, :], att_s[d, rows, 0:L], qe_s[d, rows, :], dec_s[d, rows, :]
            order = range(GLA_GROUP - 1, -1, -1) if up else range(GLA_GROUP)
            upd = {k: _dot_tn(vg[k * L:(k + 1) * L], keg[k * L:(k + 1) * L]) for k in order}
            intra = {k: _dot(attg[k * L:(k + 1) * L], vg[k * L:(k + 1) * L]) for k in order}
            st = st_s[d]
            outs = {}
            for k in order:
                outs[k] = _dot_nt(qeg[k * L:(k + 1) * L], st.astype(MXU_DTYPE)) + intra[k]
                st = st * decg[k * L:k * L + 1, :] + upd[k]
            st_s[d] = st
            o_s[rows, :] += jnp.concatenate([outs[k] for k in range(GLA_GROUP)], axis=0)
        return carry

    lax.fori_loop(0, n_grp, group, 0)
    if not with_state:
        for d in range(2):
            sT_ref[d] = st_s[d].T
    o = _rms_rows(o_s[...], nw_ref[...]) * _silu(gg_ref[...])
    o_ref[...] = o.astype(o_ref.dtype)


def _gla_call(p, s, row_off, bsz, n_tok, lp, s0, out_prev):
    m = p.shape[0]
    with_state = s0 is not None
    sb = row_off // n_tok
    assert row_off % n_tok == 0 and n_tok % (GLA_L * GLA_GROUP) == 0

    def seq_spec(col0, w):
        return pl.BlockSpec((n_tok, w), lambda b, h: (sb + b, col0 // w + h))

    gw = lp['gla_gate_w'].reshape(2, GLA_RANK, GLA_HEADS, GLA_DK).transpose(0, 2, 1, 3)
    gw_rows = jnp.zeros((2, GLA_HEADS, S_W, GLA_DK), jnp.float32)
    for d in range(2):
        r0 = 2 * SSD_HEADS + d * GLA_RANK
        gw_rows = gw_rows.at[d, :, r0:r0 + GLA_RANK, :].set(gw[d])
    in_specs = [seq_spec(P_GQ, GLA_DK), seq_spec(P_GK, GLA_DK), seq_spec(P_GV, GLA_DV), seq_spec(P_GG, GLA_DV),
                pl.BlockSpec((n_tok, S_W), lambda b, h: (sb + b, 0)),
                pl.BlockSpec((2, None, S_W, GLA_DK), lambda b, h: (0, h, 0, 0)),
                pl.BlockSpec((2, None, 1, GLA_DK), lambda b, h: (0, h, 0, 0)),
                pl.BlockSpec((1, GLA_DV), lambda b, h: (0, 0))]
    args = [p, p, p, p, s, gw_rows.astype(MXU_DTYPE), lp['gla_gate_b'].reshape(2, GLA_HEADS, 1, GLA_DK),
            lp['gla_norm_w'].reshape(1, GLA_DV)]
    state_block = pl.BlockSpec((None, 2, None, GLA_DK, GLA_DV), lambda b, h: (b, 0, h, 0, 0))
    out_specs = [pl.BlockSpec((n_tok, GLA_DV), lambda b, h: (sb + b, h))]
    out_shape = [jax.ShapeDtypeStruct((m, GROUP_W), MXU_DTYPE)]
    if with_state:
        in_specs.append(state_block)
        args.append(s0)
    else:
        out_specs.append(state_block)
        out_shape.append(jax.ShapeDtypeStruct((bsz, 2, GLA_HEADS, GLA_DK, GLA_DV), jnp.float32))
    n_in = len(args)
    aliases = {}
    if out_prev is not None:
        in_specs.append(pl.BlockSpec(memory_space=pl.ANY))
        args.append(out_prev)
        aliases = {n_in: 0}

    def body(*refs):
        if out_prev is not None:
            refs = refs[:n_in] + refs[n_in + 1:]
        _gla_kernel(*refs, n_tok=n_tok, with_state=with_state)

    return pl.pallas_call(
        body,
        grid=(bsz, GLA_HEADS),
        in_specs=in_specs,
        out_specs=out_specs,
        out_shape=out_shape,
        scratch_shapes=[pltpu.VMEM((2, n_tok, GLA_DK), jnp.float32), pltpu.VMEM((2, n_tok, GLA_DK), MXU_DTYPE),
                        pltpu.VMEM((2, n_tok, GLA_DK), MXU_DTYPE), pltpu.VMEM((2, n_tok, GLA_DK), MXU_DTYPE),
                        pltpu.VMEM((n_tok, GLA_DV), jnp.float32), pltpu.VMEM((2, GLA_DV, GLA_DK), jnp.float32)],
        input_output_aliases=aliases,
        compiler_params=_params(2, 48 << 20),
        name="gla_scan",
    )(*args)


def _mixers(p, s, lp, cached, donors):
    ctx_k, ctx_v, ssd0, gla0, lru0 = cached
    kv_w = ATT_KV_HEADS * HEAD_DIM
    att, k_new = _attention_call(p, 0, BATCH, SEQ, lp['q_norm'], lp['k_norm'], None, donors[0])
    att, = _attention_call(p, M_PROMPT, DEC_BATCH, DEC_SEQ, lp['q_norm'], lp['k_norm'],
                           (ctx_k.reshape(DEC_BATCH, PAST_LEN, kv_w), ctx_v.reshape(DEC_BATCH, PAST_LEN, kv_w)), att)
    v_new = p[:M_PROMPT, P_AV:P_AV + kv_w]

    y_p, ssd_new = _ssd_call(p, s, 0, BATCH, SEQ, lp, None)
    y_s, = _ssd_call(p, s, M_PROMPT, DEC_BATCH, DEC_SEQ, lp, ssd0)
    ssd = _ssd_finish(y_p, p, 0, lp['ssd_norm_w'], donors[1])
    ssd = _ssd_finish(y_s, p, M_PROMPT, lp['ssd_norm_w'], ssd)

    gla, gla_new = _gla_call(p, s, 0, BATCH, SEQ, lp, None, donors[2])
    gla, = _gla_call(p, s, M_PROMPT, DEC_BATCH, DEC_SEQ, lp, gla0, gla)

    lru, lru_new = _lru_call(p, 0, BATCH, SEQ, lp, None, donors[3])
    lru, = _lru_call(p, M_PROMPT, DEC_BATCH, DEC_SEQ, lp, lru0, lru)

    new_ctx = (k_new.reshape(BATCH, SEQ, ATT_KV_HEADS, HEAD_DIM), v_new.reshape(BATCH, SEQ, ATT_KV_HEADS, HEAD_DIM),
               ssd_new, gla_new, lru_new.reshape(BATCH, 2, LRU_W))
    return [att, ssd, gla, lru], new_ctx


IN_W = 9280
_W_IN_MAIN = ((0, P_AQ, 1024), (1024, P_AK, 256), (1280, P_AV, 256), (1536, P_SX, 1024), (2560, P_SZ, 1024),
              (3584, P_SB, 256), (3840, P_SC, 256), (4128, P_GQ, 512), (4640, P_GK, 512), (5152, P_GV, 1024),
              (6208, P_GG, 1024), (7232, P_LX, 1024), (8256, P_LG, 1024))
_W_IN_SMALL = ((4096, 0, 2 * SSD_HEADS), (6176, 2 * SSD_HEADS, 2 * GLA_RANK))


def _w_in_kernel(w_ref, main_ref, small_ref):
    for src, dst, width in _W_IN_MAIN:
        main_ref[dst:dst + width, :] = w_ref[src:src + width, :].astype(main_ref.dtype)
    small_ref[...] = jnp.zeros_like(small_ref)
    for src, dst, width in _W_IN_SMALL:
        small_ref[dst:dst + width, :] = w_ref[src:src + width, :].astype(small_ref.dtype)


def _reorder_w_in(w_t, l):
    tk = 256
    return pl.pallas_call(
        _w_in_kernel,
        grid=(D_MODEL // tk,),
        in_specs=[pl.BlockSpec((None, IN_W, tk), lambda i: (l, 0, i))],
        out_specs=[pl.BlockSpec((P_W, tk), lambda i: (0, i)), pl.BlockSpec((S_W, tk), lambda i: (0, i))],
        out_shape=[jax.ShapeDtypeStruct((P_W, D_MODEL), MXU_DTYPE), jax.ShapeDtypeStruct((S_W, D_MODEL), MXU_DTYPE)],
        compiler_params=_params(1, 48 << 20),
        name="w_in_regroup",
    )(w_t)


def kernel(x_prompt, x_sample, c, cache_attn_k, cache_attn_v, state_ssd, state_gla, state_lru, c_ctx,
           mod_w, mod_b, ln_g, ln_b, ffn_w_gate, ffn_w_up, ffn_w_down, w_in, w_out, q_norm, k_norm,
           ssd_conv_w, ssd_conv_b, ssd_a_log, ssd_dt_bias, ssd_d, ssd_norm_w,
           gla_gate_w, gla_gate_b, gla_norm_w,
           lru_conv_w, lru_conv_b, lru_wa, lru_ba, lru_wx, lru_bx, lru_lam):
    cond = jnp.concatenate([c_ctx[None], c, jnp.zeros((COND_ROWS - N_COND, D_MODEL), jnp.float32)], axis=0)
    mod_all = _mod_table(cond, mod_w, mod_b)

    def ffn(x, xm, mod, k_gate, l, half):
        up, rows0 = dict(tm=2048, tn=256, out_dtype=MXU_DTYPE, swiglu=True), 2048
        h0, wg, wu = _dense([xm], [ffn_w_gate, ffn_w_up], (l, half), tile0=0, n_tiles=1, cast=True, out_rows=rows0,
                            name="ffn_up_first", **up)
        h1 = _dense([xm], [wg, wu], (), tile0=1, n_tiles=(M_TOK - rows0) // up['tm'], out_tile0=0,
                    out_rows=M_TOK - rows0, name="ffn_up", **up)
        down = dict(tm=512, out_dtype=jnp.float32)
        split = rows0 // down['tm']
        y, wd = _dense([h0], [ffn_w_down], (l, half), tile0=0, n_tiles=1, tn=256, cast=True, out_rows=M_TOK,
                       resid=(x, mod, k_gate, 0.5), name="ffn_down_first", **down)
        y = _dense([h0], [wd], (), tile0=1, n_tiles=split - 1, tn=512, out_rows=M_TOK,
                   resid=(y, mod, k_gate, 0.5), name="ffn_down_head", **down)
        return _dense([h1], [wd], (), tile0=split, a_tile0=0, n_tiles=M_TOK // down['tm'] - split, tn=512,
                      out_rows=M_TOK, resid=(y, mod, k_gate, 0.5), name="ffn_down", **down)

    spare_mix = [jnp.zeros((M_TOK, GROUP_W), MXU_DTYPE)] * 4
    x, xm = _modulate(x_prompt, x_sample, mod_all[0], 1, 0)
    w_in_t = jnp.swapaxes(w_in, 1, 2)
    ctx_out = []
    for l in range(DEPTH):
        mod = mod_all[l]
        lp = dict(q_norm=q_norm[l], k_norm=k_norm[l], ssd_conv_w=ssd_conv_w[l], ssd_conv_b=ssd_conv_b[l],
                  ssd_a_log=ssd_a_log[l], ssd_dt_bias=ssd_dt_bias[l], ssd_d=ssd_d[l], ssd_norm_w=ssd_norm_w[l],
                  gla_gate_w=gla_gate_w[l], gla_gate_b=gla_gate_b[l], gla_norm_w=gla_norm_w[l],
                  lru_conv_w=lru_conv_w[l], lru_conv_b=lru_conv_b[l], lru_wa=lru_wa[l], lru_ba=lru_ba[l],
                  lru_wx=lru_wx[l], lru_bx=lru_bx[l], lru_lam=lru_lam[l])
        y = ffn(x, xm, mod, 2, l, 0)
        x, xm = _layer_norm(y, ln_g[l, 0], ln_b[l, 0], mod, 4, 3)
        w_main, w_small = _reorder_w_in(w_in_t, l)
        dense = dict(tile0=0, n_tiles=M_TOK // 1024, tm=1024, out_dtype=jnp.float32, w_t=True)
        p = _dense([xm], [w_main], (), tn=1024, name="proj_in", **dense)
        s = _dense([xm], [w_small], (), tn=S_W, name="proj_in_small", **dense)
        cached = (cache_attn_k[:, l], cache_attn_v[:, l], state_ssd[:, l], state_gla[:, l], state_lru[:, l])
        mix, new_ctx = _mixers(p, s, lp, cached, spare_mix)
        ctx_out.append(new_ctx)
        out = dict(tm=1024, out_dtype=jnp.float32)
        y, wo = _dense(mix, [w_out], (l,), tile0=0, n_tiles=1, tn=512, cast=True, resid=(x, mod, 5, 1.0),
                       name="proj_out_first", **out)
        y = _dense(mix, [wo], (), tile0=1, n_tiles=M_TOK // out['tm'] - 1, tn=1024, resid=(y, mod, 5, 1.0),
                   name="proj_out", **out)
        spare_mix = mix
        x, xm = _layer_norm(y, ln_g[l, 1], ln_b[l, 1], mod, 7, 6)
        y = ffn(x, xm, mod, 8, l, 1)
        if l + 1 < DEPTH:
            x, xm = _layer_norm(y, ln_g[l, 2], ln_b[l, 2], mod_all[l + 1], 1, 0)
        else:
            y_prompt, y_sample = _layer_norm(y, ln_g[l, 2], ln_b[l, 2])

    y_prompt = y_prompt.reshape(BATCH, SEQ, D_MODEL)
    y_sample = y_sample.reshape(DEC_BATCH, DEC_SEQ, D_MODEL)
    new_k, new_v, new_ssd, new_gla, new_lru = (jnp.stack([s_[i] for s_ in ctx_out], axis=1) for i in range(5))
    return (y_prompt, y_sample, new_k, new_v, new_ssd, new_gla, new_lru)
```

```python
import functools

import jax
import jax.numpy as jnp
from jax import lax
from jax.experimental import pallas as pl
from jax.experimental.pallas import tpu as pltpu

D_MODEL = 4096
BATCH = 16
SEQ = 256
DEPTH = 2
DEC_BATCH = 4
DEC_SEQ = 2048
PAST_LEN = 256
GRID_W = 64
GROUP_W = 1024
HEAD_DIM = 128
ATT_HEADS = 8
ATT_KV_HEADS = 2
ROPE_THETA = 10000.0
SSD_HEADDIM = 64
SSD_HEADS = 16
SSD_STATE = 128
SSD_GROUPS = 2
CONV_W = 4
GLA_HEADS = 4
GLA_DK = 128
GLA_DV = 256
GLA_RANK = 16
GLA_NORMALIZER = 16.0
LRU_W = 1024
LRU_BLOCKS = 8
LRU_BW = 128
LRU_C = 8.0
N_MOD = 9
LN_EPS = 1e-5
RMS_EPS = 1e-6
ALPHA = (2.0 * DEPTH) ** 0.25

M_PROMPT = BATCH * SEQ
M_SAMPLE = DEC_BATCH * DEC_SEQ
M_TOK = M_PROMPT + M_SAMPLE
N_COND = 1 + DEC_BATCH
COND_ROWS = 8

VMEM_LIMIT_V7X = 56 * 1024 * 1024

MXU_DTYPE = jnp.bfloat16

P_AQ, P_SX, P_SZ, P_GV, P_GG, P_LX, P_LG = 0, 1024, 2048, 3072, 4096, 5120, 6144
P_GQ, P_GK = 7168, 7680
P_AK, P_AV, P_SB, P_SC = 8192, 8448, 8704, 8960
P_W = 9216
S_W = 128


def _params(n_axes, vmem_bytes):
    return pltpu.CompilerParams(dimension_semantics=("arbitrary",) * n_axes,
                                vmem_limit_bytes=min(int(vmem_bytes), VMEM_LIMIT_V7X))


def _cond_row(tile_idx, tile_rows):
    row0 = tile_idx * tile_rows
    return jnp.where(row0 < M_PROMPT, 0, 1 + (row0 - M_PROMPT) // DEC_SEQ)


def _mod_kernel(c_ref, w_ref, b_ref, o_ref):
    c = c_ref[...]
    a = (c * jax.nn.sigmoid(c)).astype(MXU_DTYPE)
    w = w_ref[...].astype(MXU_DTYPE)
    o_ref[...] = jnp.dot(a, w, preferred_element_type=jnp.float32) + b_ref[...]


def _mod_table(cond, mod_w, mod_b):
    tn = 1024
    n = N_MOD * D_MODEL
    return pl.pallas_call(
        _mod_kernel,
        grid=(DEPTH, n // tn),
        in_specs=[pl.BlockSpec((COND_ROWS, D_MODEL), lambda l, j: (0, 0)),
                  pl.BlockSpec((None, D_MODEL, tn), lambda l, j: (l, 0, j)),
                  pl.BlockSpec((None, 1, tn), lambda l, j: (l, 0, j))],
        out_specs=pl.BlockSpec((None, COND_ROWS, tn), lambda l, j: (l, 0, j)),
        out_shape=jax.ShapeDtypeStruct((DEPTH, COND_ROWS, n), jnp.float32),
        compiler_params=_params(2, 3 * D_MODEL * tn * 4 + (4 << 20)),
        name="mod_table",
    )(cond, mod_w, mod_b.reshape(DEPTH, 1, n))


def _dense_kernel(*refs, n_a, n_w, cast, swiglu, resid, w_t):
    a_refs, w_refs = refs[:n_a], refs[n_a:n_a + n_w]
    n_in = n_a + n_w + (2 if resid else 0)
    o_ref = refs[n_in]
    cast_refs = refs[n_in + 1:n_in + n_w + 1] if cast else ()
    accs = []
    for wi, w_ref in enumerate(w_refs):
        if cast:
            w_narrow = w_ref[...].astype(MXU_DTYPE)
            cast_refs[wi][...] = w_narrow
        k0, acc = 0, None
        for a_ref in a_refs:
            kp = a_ref.shape[1]
            if w_t:
                part = _dot_nt(a_ref[...], w_ref[:, k0:k0 + kp])
            else:
                w_rows = w_narrow[k0:k0 + kp, :] if cast else w_ref[k0:k0 + kp, :]
                part = jnp.dot(a_ref[...], w_rows, preferred_element_type=jnp.float32)
            acc = part if acc is None else acc + part
            k0 += kp
        accs.append(acc)
    out = accs[0] * jax.nn.sigmoid(accs[0]) * accs[1] if swiglu else accs[0]
    if resid:
        x_ref, g_ref = refs[n_a + n_w], refs[n_a + n_w + 1]
        tile0, tm, gate_scale = resid
        g = g_ref[pl.ds(_cond_row(tile0 + pl.program_id(0), tm), 1), :]
        out = ALPHA * x_ref[...] + (gate_scale * g) * out
    o_ref[...] = out.astype(o_ref.dtype)


def _dense(a_list, w_list, w_index, *, tile0, n_tiles, tm, tn, out_dtype, name, swiglu=False, cast=False,
           resid=None, w_t=False, a_tile0=None, out_tile0=None, out_rows=None):
    a_tile0 = tile0 if a_tile0 is None else a_tile0
    out_tile0 = tile0 if out_tile0 is None else out_tile0
    out_rows = a_list[0].shape[0] if out_rows is None else out_rows
    k, n = w_list[0].shape[-2:][::-1] if w_t else w_list[0].shape[-2:]
    n_a, n_w, lead = len(a_list), len(w_list), len(w_index)
    assert sum(a.shape[1] for a in a_list) == k and n % tn == 0
    assert (a_tile0 + n_tiles) * tm <= a_list[0].shape[0] and (out_tile0 + n_tiles) * tm <= out_rows
    assert not (w_t and cast)
    a_mode = dict(pipeline_mode=pl.Buffered(1)) if n_tiles == 1 else {}
    in_specs = [pl.BlockSpec((tm, a.shape[1]), lambda i, j: (a_tile0 + i, 0), **a_mode) for a in a_list]
    if w_t:
        in_specs += [pl.BlockSpec((None,) * lead + (tn, k), lambda i, j: tuple(w_index) + (j, 0)) for _ in w_list]
    else:
        in_specs += [pl.BlockSpec((None,) * lead + (k, tn), lambda i, j: tuple(w_index) + (0, j)) for _ in w_list]
    args = list(a_list) + list(w_list)
    aliases = {}
    if resid is not None:
        x, mod, k_gate, gate_scale = resid
        assert x.shape == (out_rows, n) and x.dtype == out_dtype and out_tile0 == tile0
        in_specs += [pl.BlockSpec((tm, tn), lambda i, j: (tile0 + i, j)),
                     pl.BlockSpec((COND_ROWS, tn), lambda i, j: (0, k_gate * (n // tn) + j))]
        args += [x, mod]
        aliases = {n_a + n_w: 0}
    out_specs = [pl.BlockSpec((tm, tn), lambda i, j: (out_tile0 + i, j))]
    out_shape = [jax.ShapeDtypeStruct((out_rows, n), out_dtype)]
    if cast:
        out_specs += [pl.BlockSpec((k, tn), lambda i, j: (0, j)) for _ in w_list]
        out_shape += [jax.ShapeDtypeStruct((k, n), MXU_DTYPE) for _ in w_list]
    esz, wsz, osz = jnp.dtype(MXU_DTYPE).itemsize, w_list[0].dtype.itemsize, jnp.dtype(out_dtype).itemsize
    vmem = ((1 if n_tiles == 1 else 2) * tm * k * esz + 2 * n_w * k * tn * wsz + 2 * tm * tn * osz
            + (n_w + 1) * tm * tn * 4 + (3 * n_w * k * tn * esz if cast else 0)
            + (2 * tm * tn * 4 if resid is not None else 0))
    out = pl.pallas_call(
        functools.partial(_dense_kernel, n_a=n_a, n_w=n_w, cast=cast, swiglu=swiglu,
                          resid=None if resid is None else (tile0, tm, resid[3]), w_t=w_t),
        grid=(n_tiles, n // tn),
        in_specs=in_specs,
        out_specs=out_specs,
        out_shape=out_shape,
        input_output_aliases=aliases,
        compiler_params=_params(2, vmem + (4 << 20)),
        name=name,
    )(*args)
    return out if cast else out[0]


def _modulate_kernel(xp_ref, xs_ref, sc_ref, sh_ref, x_ref, o_ref, *, tm):
    i = pl.program_id(0)
    r = _cond_row(i, tm)
    sc = sc_ref[pl.ds(r, 1), :]
    sh = sh_ref[pl.ds(r, 1), :]

    def emit(x):
        x_ref[...] = x
        o_ref[...] = (x * (1.0 + sc) + sh).astype(o_ref.dtype)

    pl.when(i < M_PROMPT // tm)(lambda: emit(xp_ref[...]))
    pl.when(i >= M_PROMPT // tm)(lambda: emit(xs_ref[...]))


def _mod_spec(k):
    return pl.BlockSpec((COND_ROWS, D_MODEL), lambda i: (0, k))


def _modulate(x_prompt, x_sample, mod, k_scale, k_shift):
    tm = 256
    n_p = M_PROMPT // tm
    row = pl.BlockSpec((tm, D_MODEL), lambda i: (i, 0))
    return pl.pallas_call(
        functools.partial(_modulate_kernel, tm=tm),
        grid=(M_TOK // tm,),
        in_specs=[pl.BlockSpec((tm, D_MODEL), lambda i: (jnp.minimum(i, n_p - 1), 0)),
                  pl.BlockSpec((tm, D_MODEL), lambda i: (jnp.maximum(i - n_p, 0), 0)),
                  _mod_spec(k_scale), _mod_spec(k_shift)],
        out_specs=[row, row],
        out_shape=[jax.ShapeDtypeStruct((M_TOK, D_MODEL), jnp.float32),
                   jax.ShapeDtypeStruct((M_TOK, D_MODEL), MXU_DTYPE)],
        compiler_params=_params(1, 32 << 20),
        name="modulate",
    )(x_prompt.reshape(M_PROMPT, D_MODEL), x_sample.reshape(M_SAMPLE, D_MODEL), mod, mod)


def _ln_kernel(*refs, tm, with_next):
    if with_next:
        y_ref, lg_ref, lb_ref, sc_ref, sh_ref, xo_ref, mo_ref = refs
    else:
        y_ref, lg_ref, lb_ref, yp_ref, ys_ref = refs
    i = pl.program_id(0)
    r = _cond_row(i, tm)
    y = y_ref[...]
    mu = jnp.mean(y, axis=-1, keepdims=True)
    yc = y - mu
    var = jnp.mean(yc * yc, axis=-1, keepdims=True)
    xn = yc * lax.rsqrt(var + LN_EPS) * lg_ref[...] + lb_ref[...]
    if with_next:
        xo_ref[...] = xn
        sc = sc_ref[pl.ds(r, 1), :]
        sh = sh_ref[pl.ds(r, 1), :]
        mo_ref[...] = (xn * (1.0 + sc) + sh).astype(mo_ref.dtype)
    else:
        @pl.when(i < M_PROMPT // tm)
        def _():
            yp_ref[...] = xn

        @pl.when(i >= M_PROMPT // tm)
        def _():
            ys_ref[...] = xn


def _layer_norm(y, ln_g, ln_b, next_mod=None, k_scale=0, k_shift=0):
    tm = 256
    m = y.shape[0]
    n_p = M_PROMPT // tm
    with_next = next_mod is not None
    row = pl.BlockSpec((tm, D_MODEL), lambda i: (i, 0))
    vec = pl.BlockSpec((1, D_MODEL), lambda i: (0, 0))
    in_specs = [row, vec, vec]
    args = [y, ln_g.reshape(1, D_MODEL), ln_b.reshape(1, D_MODEL)]
    if with_next:
        in_specs += [_mod_spec(k_scale), _mod_spec(k_shift)]
        args += [next_mod, next_mod]
        out_specs = [row, row]
        out_shape = [jax.ShapeDtypeStruct((m, D_MODEL), jnp.float32), jax.ShapeDtypeStruct((m, D_MODEL), MXU_DTYPE)]
    else:
        out_specs = [pl.BlockSpec((tm, D_MODEL), lambda i: (jnp.minimum(i, n_p - 1), 0)),
                     pl.BlockSpec((tm, D_MODEL), lambda i: (jnp.maximum(i - n_p, 0), 0))]
        out_shape = [jax.ShapeDtypeStruct((M_PROMPT, D_MODEL), jnp.float32),
                     jax.ShapeDtypeStruct((M_SAMPLE, D_MODEL), jnp.float32)]
    out = pl.pallas_call(
        functools.partial(_ln_kernel, tm=tm, with_next=with_next),
        grid=(m // tm,),
        in_specs=in_specs,
        out_specs=out_specs,
        out_shape=out_shape,
        compiler_params=_params(1, 48 << 20),
        name="layer_norm",
    )(*args)
    return out[0], out[1]


def _row_iota(shape):
    return lax.broadcasted_iota(jnp.int32, shape, 0)


def _lane_iota(shape):
    return lax.broadcasted_iota(jnp.int32, shape, 1)


def _shift_rows(x, s, fill, up=False):
    n = x.shape[0]
    t = _row_iota(x.shape)
    if up:
        return jnp.where(t < n - s, pltpu.roll(x, n - s, 0), fill)
    return jnp.where(t >= s, pltpu.roll(x, s, 0), fill)


def _softplus(z):
    return jnp.maximum(z, 0.0) + jnp.log1p(jnp.exp(-jnp.abs(z)))


def _silu(z):
    return z * jax.nn.sigmoid(z)


def _dot(a, b):
    return jnp.dot(a, b, preferred_element_type=jnp.float32)


def _dot_nt(a, b):
    return lax.dot_general(a, b, (((1,), (1,)), ((), ())), preferred_element_type=jnp.float32)


def _dot_tn(a, b):
    return lax.dot_general(a, b, (((0,), (0,)), ((), ())), preferred_element_type=jnp.float32)


def _split3(x):
    hi = x.astype(MXU_DTYPE)
    r1 = x - hi.astype(jnp.float32)
    mid = r1.astype(MXU_DTYPE)
    lo = (r1 - mid.astype(jnp.float32)).astype(MXU_DTYPE)
    return hi, mid, lo


def _dot01_right(x, e):
    hi, mid, lo = _split3(x)
    return _dot(hi, e) + _dot(mid, e) + _dot(lo, e)


def _conv4(x, w_ref, b_ref):
    acc = _shift_rows(x, 2, 0.0) * w_ref[0:1, :]
    acc = acc + _shift_rows(x, 1, 0.0) * w_ref[1:2, :]
    acc = acc + x * w_ref[2:3, :]
    acc = acc + _shift_rows(x, 1, 0.0, up=True) * w_ref[3:4, :]
    return acc + b_ref[...]


def _rms_rows(x, w):
    return x * lax.rsqrt(jnp.mean(x * x, axis=-1, keepdims=True) + RMS_EPS) * w


ATT_STACK = 2


def _rope_tables(n_tok):
    rows = n_tok // GRID_W
    row = jnp.repeat(jnp.arange(rows, dtype=jnp.float32), GRID_W)
    col = jnp.tile(jnp.arange(GRID_W, dtype=jnp.float32), rows)
    n_freq = HEAD_DIM // 4
    inv = ROPE_THETA ** (-jnp.arange(n_freq, dtype=jnp.float32) / n_freq)
    ar, ac = row[:, None] * inv, col[:, None] * inv
    cos = jnp.concatenate([jnp.cos(ar), jnp.cos(ar), jnp.cos(ac), jnp.cos(ac)], axis=1)
    sin = jnp.concatenate([-jnp.sin(ar), jnp.sin(ar), -jnp.sin(ac), jnp.sin(ac)], axis=1)
    return cos, sin


def _rope(x, cos, sin):
    quarter = HEAD_DIM // 4
    lane = _lane_iota(x.shape)
    partner = jnp.where(lane % (2 * quarter) < quarter,
                        pltpu.roll(x, HEAD_DIM - quarter, 1), pltpu.roll(x, quarter, 1))
    return x * cos + partner * sin


def _attn_kernel(*refs, n_tok, tq, n_ctx, rope):
    if rope:
        (q_ref, k_ref, v_ref, qn_ref, kn_ref, cq_ref, sq_ref, ck_ref, sk_ref, xk_ref, xv_ref,
         o_ref, ks_ref, vs_ref) = refs
    else:
        q_ref, k_ref, v_ref, qn_ref, kn_ref, o_ref, ko_ref, ks_ref, vs_ref = refs
    rep = ATT_HEADS // ATT_KV_HEADS

    @pl.when(pl.program_id(2) == 0)
    def _():
        kn = _rms_rows(k_ref[...], kn_ref[...])
        if rope:
            kn = _rope(kn, ck_ref[...], sk_ref[...])
            ks_ref[n_tok:n_tok + n_ctx, :] = xk_ref[...].astype(MXU_DTYPE)
            vs_ref[n_tok:n_tok + n_ctx, :] = xv_ref[...].astype(MXU_DTYPE)
        else:
            ko_ref[...] = kn
        ks_ref[0:n_tok, :] = kn.astype(MXU_DTYPE)
        vs_ref[0:n_tok, :] = v_ref[...].astype(MXU_DTYPE)

    heads = []
    for r in range(rep):
        qh = _rms_rows(q_ref[:, r * HEAD_DIM:(r + 1) * HEAD_DIM], qn_ref[...])
        if rope:
            qh = _rope(qh, cq_ref[...], sq_ref[...])
        heads.append(qh.astype(MXU_DTYPE))
    for r0 in range(0, rep, ATT_STACK):
        qs = jnp.concatenate(heads[r0:r0 + ATT_STACK], axis=0)
        s = _dot_nt(qs, ks_ref[...]) * (HEAD_DIM ** -0.5)
        e = jnp.exp(s - jnp.max(s, axis=-1, keepdims=True))
        o = _dot(e.astype(MXU_DTYPE), vs_ref[...]) / jnp.sum(e, axis=-1, keepdims=True)
        for r in range(ATT_STACK):
            o_ref[:, (r0 + r) * HEAD_DIM:(r0 + r + 1) * HEAD_DIM] = o[r * tq:(r + 1) * tq].astype(o_ref.dtype)


def _attention_call(p, row_off, bsz, n_tok, q_norm, k_norm, ctx_kv, out_prev):
    m = p.shape[0]
    rope = ctx_kv is not None
    tq = 128 if rope else n_tok
    nq = n_tok // tq
    rep = ATT_HEADS // ATT_KV_HEADS
    qw = rep * HEAD_DIM
    n_ctx = ctx_kv[0].shape[1] if rope else 0
    assert row_off % n_tok == 0 and n_tok % tq == 0
    rb, sb = row_off // tq, row_off // n_tok

    def seq_spec(col0):
        return pl.BlockSpec((n_tok, HEAD_DIM), lambda b, g, i: (sb + b, col0 // HEAD_DIM + g))

    vec = pl.BlockSpec((1, HEAD_DIM), lambda b, g, i: (0, 0))
    in_specs = [pl.BlockSpec((tq, qw), lambda b, g, i: (rb + b * nq + i, P_AQ // qw + g)),
                seq_spec(P_AK), seq_spec(P_AV), vec, vec]
    args = [p, p, p, q_norm.reshape(1, HEAD_DIM), k_norm.reshape(1, HEAD_DIM)]
    out_block = pl.BlockSpec((tq, qw), lambda b, g, i: (rb + b * nq + i, g))
    out_specs = [out_block]
    out_shape = [jax.ShapeDtypeStruct((m, GROUP_W), MXU_DTYPE)]
    aliases = {}
    if rope:
        cos, sin = _rope_tables(n_tok)
        in_specs += [pl.BlockSpec((tq, HEAD_DIM), lambda b, g, i: (i, 0))] * 2
        in_specs += [pl.BlockSpec((n_tok, HEAD_DIM), lambda b, g, i: (0, 0))] * 2
        in_specs += [pl.BlockSpec((None, n_ctx, HEAD_DIM), lambda b, g, i: (b, 0, g))] * 2
        args += [cos, sin, cos, sin, ctx_kv[0], ctx_kv[1]]
    else:
        out_specs.append(pl.BlockSpec((n_tok, HEAD_DIM), lambda b, g, i: (b, g)))
        out_shape.append(jax.ShapeDtypeStruct((bsz * n_tok, ATT_KV_HEADS * HEAD_DIM), jnp.float32))
    if out_prev is not None:
        in_specs.append(pl.BlockSpec(memory_space=pl.ANY))
        args.append(out_prev)
        aliases = {len(args) - 1: 0}
    n_keys = n_tok + n_ctx

    def body(*refs):
        if out_prev is not None:
            refs = refs[:len(args) - 1] + refs[len(args):]
        _attn_kernel(*refs, n_tok=n_tok, tq=tq, n_ctx=n_ctx, rope=rope)

    out = pl.pallas_call(
        body,
        grid=(bsz, ATT_KV_HEADS, nq),
        in_specs=in_specs,
        out_specs=out_specs,
        out_shape=out_shape,
        scratch_shapes=[pltpu.VMEM((n_keys, HEAD_DIM), MXU_DTYPE), pltpu.VMEM((n_keys, HEAD_DIM), MXU_DTYPE)],
        input_output_aliases=aliases,
        compiler_params=_params(3, 48 << 20),
        name="attention",
    )(*args)
    return out


def _lru_kernel(*refs, n_tok, with_state):
    if with_state:
        (x_ref, g_ref, cw_ref, cb_ref, wa_ref, ba_ref, wx_ref, bx_ref, lam_ref, h0_ref, o_ref) = refs
    else:
        (x_ref, g_ref, cw_ref, cb_ref, wa_ref, ba_ref, wx_ref, bx_ref, lam_ref, o_ref, hT_ref) = refs
    xl = _conv4(x_ref[...], cw_ref, cb_ref)
    xb = xl.astype(MXU_DTYPE)
    t = _row_iota(xl.shape)
    h_sum = None
    for d in range(2):
        up = d == 1
        r = jax.nn.sigmoid(_dot(xb, wa_ref[d]) + ba_ref[d])
        i = jax.nn.sigmoid(_dot(xb, wx_ref[d]) + bx_ref[d])
        log_a = -LRU_C * r * _softplus(-lam_ref[d])
        a = jnp.exp(log_a)
        th = jnp.tanh(log_a)
        u = jnp.sqrt(-2.0 * th / (1.0 - th)) * i * xl
        if with_state:
            first = (t == n_tok - 1) if up else (t == 0)
            u = jnp.where(first, u + a * h0_ref[d], u)
        s = 1
        while s < n_tok:
            u = u + a * _shift_rows(u, s, 0.0, up=up)
            a = a * _shift_rows(a, s, 1.0, up=up)
            s *= 2
        h_sum = u if h_sum is None else h_sum + u
        if not with_state:
            hT_ref[d] = u[0:1, :] if up else u[n_tok - 1:n_tok, :]
    o_ref[...] = (h_sum * jax.nn.gelu(g_ref[...])).astype(o_ref.dtype)


def _lru_call(p, row_off, bsz, n_tok, lp, h0, out_prev):
    m = p.shape[0]
    with_state = h0 is not None
    sb = row_off // n_tok
    assert row_off % n_tok == 0

    def seq_spec(col0):
        return pl.BlockSpec((n_tok, LRU_BW), lambda b, n: (sb + b, col0 // LRU_BW + n))

    def par2(shape):
        return pl.BlockSpec((2,) + shape + (LRU_BW,), lambda b, n: (0,) + (0,) * len(shape) + (n,))

    in_specs = [seq_spec(P_LX), seq_spec(P_LG),
                pl.BlockSpec((CONV_W, LRU_BW), lambda b, n: (0, n)),
                pl.BlockSpec((1, LRU_BW), lambda b, n: (0, n)),
                pl.BlockSpec((2, None, LRU_BW, LRU_BW), lambda b, n: (0, n, 0, 0)), par2((1,)),
                pl.BlockSpec((2, None, LRU_BW, LRU_BW), lambda b, n: (0, n, 0, 0)), par2((1,)), par2((1,))]
    args = [p, p, lp['lru_conv_w'], lp['lru_conv_b'].reshape(1, LRU_W),
            lp['lru_wa'].astype(MXU_DTYPE), lp['lru_ba'].reshape(2, 1, LRU_W),
            lp['lru_wx'].astype(MXU_DTYPE), lp['lru_bx'].reshape(2, 1, LRU_W), lp['lru_lam'].reshape(2, 1, LRU_W)]
    out_specs = [pl.BlockSpec((n_tok, LRU_BW), lambda b, n: (sb + b, n))]
    out_shape = [jax.ShapeDtypeStruct((m, GROUP_W), MXU_DTYPE)]
    if with_state:
        in_specs.append(pl.BlockSpec((None, 2, 1, LRU_BW), lambda b, n: (b, 0, 0, n)))
        args.append(h0.reshape(bsz, 2, 1, LRU_W))
    else:
        out_specs.append(pl.BlockSpec((None, 2, 1, LRU_BW), lambda b, n: (b, 0, 0, n)))
        out_shape.append(jax.ShapeDtypeStruct((bsz, 2, 1, LRU_W), jnp.float32))
    n_in = len(args)
    aliases = {}
    if out_prev is not None:
        in_specs.append(pl.BlockSpec(memory_space=pl.ANY))
        args.append(out_prev)
        aliases = {n_in: 0}

    def body(*refs):
        if out_prev is not None:
            refs = refs[:n_in] + refs[n_in + 1:]
        _lru_kernel(*refs, n_tok=n_tok, with_state=with_state)

    return pl.pallas_call(
        body,
        grid=(bsz, LRU_BLOCKS),
        in_specs=in_specs,
        out_specs=out_specs,
        out_shape=out_shape,
        input_output_aliases=aliases,
        compiler_params=_params(2, 48 << 20),
        name="rg_lru",
    )(*args)


SSD_L = 128
SSD_GH = SSD_HEADS // SSD_GROUPS
SSD_GW = SSD_GH * SSD_HEADDIM


def _ssd_kernel(*refs, n_tok, with_state):
    if with_state:
        (x_ref, b_ref, c_ref, s_ref, cwx_ref, cbx_ref, cwb_ref, cbb_ref, cwc_ref, cbc_ref, dtb_ref, alog_ref,
         dskip_ref, h0_ref, y_ref, xc_s, xdt_s, xte_s, dec_s, bc_s, cc_s, cum_s, cumt_s, h_s) = refs
    else:
        (x_ref, b_ref, c_ref, s_ref, cwx_ref, cbx_ref, cwb_ref, cbb_ref, cwc_ref, cbc_ref, dtb_ref, alog_ref,
         dskip_ref, y_ref, hT_ref, xc_s, xdt_s, xte_s, dec_s, bc_s, cc_s, cum_s, cumt_s, h_s) = refs
    L = SSD_L
    n_chunk = n_tok // L
    g = pl.program_id(1)

    xc_s[...] = _silu(_conv4(x_ref[...], cwx_ref, cbx_ref))
    bc_s[...] = _silu(_conv4(b_ref[...], cwb_ref, cbb_ref)).astype(MXU_DTYPE)
    cc_s[...] = _silu(_conv4(c_ref[...], cwc_ref, cbc_ref)).astype(MXU_DTYPE)
    dt = _softplus(s_ref[...] + dtb_ref[...])
    y_ref[...] = xc_s[...] * dskip_ref[...]

    e8 = (_row_iota((S_W, SSD_GW)) == _lane_iota((S_W, SSD_GW)) // SSD_HEADDIM).astype(MXU_DTYPE)
    ri, ci = _row_iota((L, L)), _lane_iota((L, L))
    lane_w = _lane_iota((L, 2 * SSD_HEADDIM))

    t = _row_iota((n_tok, S_W))
    for d in range(2):
        up = d == 1
        end = 0 if up else L - 1
        sel = (_row_iota((S_W, S_W)) == _lane_iota((S_W, S_W)) + (d * SSD_HEADS + g * SSD_GH)).astype(MXU_DTYPE)
        sel = jnp.where(_lane_iota((S_W, S_W)) < SSD_GH, sel, jnp.zeros_like(sel))
        dt_sel = _dot01_right(dt, sel)
        a_row = -jnp.exp(_dot01_right(alog_ref[...], sel))
        cum = dt_sel * a_row
        s = 1
        while s < L:
            ok = (t % L < L - s) if up else (t % L >= s)
            cum = cum + jnp.where(ok, pltpu.roll(cum, (n_tok - s) if up else s, 0), 0.0)
            s *= 2
        cum_s[...] = cum
        cumt_s[...] = jnp.swapaxes(cum.reshape(n_chunk, L, S_W), 1, 2).reshape(n_chunk * S_W, L)
        cum_i = _dot01_right(cum, e8).reshape(n_chunk, L, SSD_GW)
        xdt = xc_s[...] * _dot01_right(dt_sel, e8)
        xdt_s[...] = xdt.astype(MXU_DTYPE)
        to_end = jnp.exp(cum_i[:, end:end + 1, :] - cum_i)
        xte_s[...] = (xdt.reshape(n_chunk, L, SSD_GW) * to_end).reshape(n_tok, SSD_GW).astype(MXU_DTYPE)
        dec_s[...] = jnp.exp(cum_i).reshape(n_tok, SSD_GW)
        if with_state:
            h_s[...] = h0_ref[d].reshape(SSD_GW, SSD_STATE)
        else:
            h_s[...] = jnp.zeros_like(h_s)
        tri = ((ci >= ri) if up else (ci <= ri))

        def chunk(i, carry):
            c = (n_chunk - 1 - i) if up else i
            t0 = pl.multiple_of(c * L, L)
            rows = pl.ds(t0, L)
            cum = cum_s[rows, :]
            cum_t = cumt_s[rows, :]
            bcv, ccv = bc_s[rows, :], cc_s[rows, :]
            cb = _dot_nt(ccv, bcv)
            h_b16 = h_s[...].astype(MXU_DTYPE)
            y = _dot_nt(ccv, h_b16) * dec_s[rows, :]
            xdt_b16 = xdt_s[rows, :]
            pieces = []
            for pair in range(SSD_GH // 2):
                xp = xdt_b16[:, pair * 2 * SSD_HEADDIM:(pair + 1) * 2 * SSD_HEADDIM]
                acc = None
                for sub in range(2):
                    hh = 2 * pair + sub
                    seg = jnp.where(tri, jnp.exp(cum[:, hh:hh + 1] - cum_t[hh:hh + 1, :]), 0.0)
                    sc = (cb * seg).astype(MXU_DTYPE)
                    half = (lane_w // SSD_HEADDIM) == sub
                    part = _dot(sc, jnp.where(half, xp, jnp.zeros_like(xp)))
                    acc = part if acc is None else acc + part
                pieces.append(acc)
            y = y + jnp.concatenate(pieces, axis=1)
            y_ref[rows, :] += y
            upd = _dot_tn(xte_s[rows, :], bcv)
            for hh in range(SSD_GH):
                blk = slice(hh * SSD_HEADDIM, (hh + 1) * SSD_HEADDIM)
                h_s[blk, :] = h_s[blk, :] * jnp.exp(cum_t[hh:hh + 1, end:end + 1]) + upd[blk, :]
            return carry

        lax.fori_loop(0, n_chunk, chunk, 0, unroll=4)
        if not with_state:
            hT_ref[d] = h_s[...].reshape(SSD_GH, SSD_HEADDIM, SSD_STATE)


def _ssd_call(p, s, row_off, bsz, n_tok, lp, h0):
    with_state = h0 is not None
    sb = row_off // n_tok
    assert row_off % n_tok == 0 and n_tok % SSD_L == 0
    assert SSD_L == S_W

    def seq_spec(col0, w):
        return pl.BlockSpec((n_tok, w), lambda b, g: (sb + b, col0 // w + g))

    def conv_specs(col0, w):
        return [pl.BlockSpec((CONV_W, w), lambda b, g: (0, col0 // w + g)),
                pl.BlockSpec((1, w), lambda b, g: (0, col0 // w + g))]

    small = pl.BlockSpec((1, S_W), lambda b, g: (0, 0))
    pad = jnp.zeros((S_W - 2 * SSD_HEADS,), jnp.float32)
    dtb = jnp.concatenate([lp['ssd_dt_bias'].reshape(-1), pad]).reshape(1, S_W)
    alog = jnp.concatenate([lp['ssd_a_log'].reshape(-1), pad]).reshape(1, S_W)
    dskip = jnp.repeat(lp['ssd_d'][0] + lp['ssd_d'][1], SSD_HEADDIM).reshape(1, GROUP_W)
    cw, cb = lp['ssd_conv_w'], lp['ssd_conv_b'].reshape(1, -1)
    in_specs = ([seq_spec(P_SX, SSD_GW), seq_spec(P_SB, SSD_STATE), seq_spec(P_SC, SSD_STATE),
                 pl.BlockSpec((n_tok, S_W), lambda b, g: (sb + b, 0))]
                + conv_specs(0, SSD_GW) + conv_specs(GROUP_W, SSD_STATE)
                + conv_specs(GROUP_W + SSD_GROUPS * SSD_STATE, SSD_STATE)
                + [small, small, pl.BlockSpec((1, SSD_GW), lambda b, g: (0, g))])
    args = [p, p, p, s, cw, cb, cw, cb, cw, cb, dtb, alog, dskip]
    state_block = pl.BlockSpec((None, 2, SSD_GH, SSD_HEADDIM, SSD_STATE), lambda b, g: (b, 0, g, 0, 0))
    out_specs = [pl.BlockSpec((n_tok, SSD_GW), lambda b, g: (b, g))]
    out_shape = [jax.ShapeDtypeStruct((bsz * n_tok, GROUP_W), jnp.float32)]
    if with_state:
        in_specs.append(state_block)
        args.append(h0)
    else:
        out_specs.append(state_block)
        out_shape.append(jax.ShapeDtypeStruct((bsz, 2, SSD_HEADS, SSD_HEADDIM, SSD_STATE), jnp.float32))
    return pl.pallas_call(
        functools.partial(_ssd_kernel, n_tok=n_tok, with_state=with_state),
        grid=(bsz, SSD_GROUPS),
        in_specs=in_specs,
        out_specs=out_specs,
        out_shape=out_shape,
        scratch_shapes=[pltpu.VMEM((n_tok, SSD_GW), jnp.float32), pltpu.VMEM((n_tok, SSD_GW), MXU_DTYPE),
                        pltpu.VMEM((n_tok, SSD_GW), MXU_DTYPE), pltpu.VMEM((n_tok, SSD_GW), jnp.float32),
                        pltpu.VMEM((n_tok, SSD_STATE), MXU_DTYPE), pltpu.VMEM((n_tok, SSD_STATE), MXU_DTYPE),
                        pltpu.VMEM((n_tok, S_W), jnp.float32), pltpu.VMEM((n_tok, SSD_L), jnp.float32),
                        pltpu.VMEM((SSD_GW, SSD_STATE), jnp.float32)],
        compiler_params=_params(2, VMEM_LIMIT_V7X),
        name="ssd_scan",
    )(*args)


def _gated_norm_kernel(y_ref, z_ref, w_ref, o_ref):
    o_ref[...] = _rms_rows(y_ref[...] * _silu(z_ref[...]), w_ref[...]).astype(o_ref.dtype)


def _ssd_finish(y, p, row_off, norm_w, out_prev):
    tm = 256
    m, n = p.shape[0], y.shape[0]
    rb = row_off // tm
    in_specs = [pl.BlockSpec((tm, GROUP_W), lambda i: (i, 0)),
                pl.BlockSpec((tm, GROUP_W), lambda i: (rb + i, P_SZ // GROUP_W)),
                pl.BlockSpec((1, GROUP_W), lambda i: (0, 0))]
    args = [y, p, norm_w.reshape(1, GROUP_W)]
    aliases = {}
    if out_prev is not None:
        in_specs.append(pl.BlockSpec(memory_space=pl.ANY))
        args.append(out_prev)
        aliases = {3: 0}

    def body(y_ref, z_ref, w_ref, *rest):
        _gated_norm_kernel(y_ref, z_ref, w_ref, rest[-1])

    return pl.pallas_call(
        body,
        grid=(n // tm,),
        in_specs=in_specs,
        out_specs=pl.BlockSpec((tm, GROUP_W), lambda i: (rb + i, 0)),
        out_shape=jax.ShapeDtypeStruct((m, GROUP_W), MXU_DTYPE),
        input_output_aliases=aliases,
        compiler_params=_params(1, 32 << 20),
        name="ssd_gated_norm",
    )(*args)


GLA_L = 16
GLA_GROUP = 16
LOG2_E = 1.4426950408889634


def _gla_kernel(*refs, n_tok, with_state):
    if with_state:
        (q_ref, k_ref, v_ref, gg_ref, s_ref, gw_ref, gb_ref, nw_ref, s0_ref, o_ref,
         dec_s, qe_s, ke_s, att_s, o_s, st_s) = refs
    else:
        (q_ref, k_ref, v_ref, gg_ref, s_ref, gw_ref, gb_ref, nw_ref, o_ref, sT_ref,
         dec_s, qe_s, ke_s, att_s, o_s, st_s) = refs
    L = GLA_L
    n_blk = n_tok // L
    blk3 = (n_blk, L, GLA_DK)
    q3 = (q_ref[...] * (GLA_DK ** -0.5)).reshape(blk3)
    k3 = k_ref[...].reshape(blk3)
    sb16 = s_ref[...].astype(MXU_DTYPE)
    t = _row_iota((n_tok, GLA_DK))
    ri3 = lax.broadcasted_iota(jnp.int32, blk3, 1)
    lane3 = lax.broadcasted_iota(jnp.int32, blk3, 2)

    for d in range(2):
        up = d == 1
        end = 0 if up else L - 1
        gate = _dot(sb16, gw_ref[d]) + gb_ref[d]
        b = -_softplus(-gate) / GLA_NORMALIZER
        s = 1
        while s < L:
            ok = (t % L < L - s) if up else (t % L >= s)
            b = b + jnp.where(ok, pltpu.roll(b, (n_tok - s) if up else s, 0), 0.0)
            s *= 2
        b3 = b.reshape(blk3)
        b_end = b3[:, end:end + 1, :]
        dec_s[d] = jnp.exp(jnp.broadcast_to(b_end, blk3)).reshape(n_tok, GLA_DK)
        qe_s[d] = (q3 * jnp.exp(b3)).reshape(n_tok, GLA_DK).astype(MXU_DTYPE)
        ke_s[d] = (k3 * jnp.exp(b_end - b3)).reshape(n_tok, GLA_DK).astype(MXU_DTYPE)
        code = jnp.where((ri3 <= lane3) if up else (ri3 >= lane3), lane3, -1)
        b2 = b3 * LOG2_E
        half = L // 2
        att_h = [jnp.zeros((n_blk, half, GLA_DK), jnp.float32) for _ in range(2)]
        for j in range(L):
            for h in range(2):
                if (h == 0 and j >= half) if not up else (h == 1 and j < half):
                    continue
                rows = slice(h * half, (h + 1) * half)
                w = jnp.exp2(b2[:, rows, :] - b2[:, j:j + 1, :]) * q3[:, rows, :] * k3[:, j:j + 1, :]
                att_h[h] = jnp.where(code[:, rows, :] == j, jnp.sum(w, axis=-1, keepdims=True), att_h[h])
        att = jnp.concatenate(att_h, axis=1)
        att_s[d] = att.reshape(n_tok, GLA_DK).astype(MXU_DTYPE)
        if with_state:
            st_s[d] = s0_ref[d].T
        else:
            st_s[d] = jnp.zeros((GLA_DV, GLA_DK), jnp.float32)

    o_s[...] = jnp.zeros_like(o_s)
    n_grp = n_blk // GLA_GROUP
    rows_per_trip = GLA_GROUP * L

    def group(i, carry):
        for d in range(2):
            up = d == 1
            c = (n_grp - 1 - i) if up else i
            rows = pl.ds(pl.multiple_of(c * rows_per_trip, rows_per_trip), rows_per_trip)
            vg = v_ref[rows, :].astype(MXU_DTYPE)
            keg, attg, qeg, decg = ke_s[d, rows, :], att_s[d, rows, 0:L], qe_s[d, rows, :], dec_s[d, rows, :]
            order = range(GLA_GROUP - 1, -1, -1) if up else range(GLA_GROUP)
            upd = {k: _dot_tn(vg[k * L:(k + 1) * L], keg[k * L:(k + 1) * L]) for k in order}
            intra = {k: _dot(attg[k * L:(k + 1) * L], vg[k * L:(k + 1) * L]) for k in order}
            st = st_s[d]
            outs = {}
            for k in order:
                outs[k] = _dot_nt(qeg[k * L:(k + 1) * L], st.astype(MXU_DTYPE)) + intra[k]
                st = st * decg[k * L:k * L + 1, :] + upd[k]
            st_s[d] = st
            o_s[rows, :] += jnp.concatenate([outs[k] for k in range(GLA_GROUP)], axis=0)
        return carry

    lax.fori_loop(0, n_grp, group, 0)
    if not with_state:
        for d in range(2):
            sT_ref[d] = st_s[d].T
    o = _rms_rows(o_s[...], nw_ref[...]) * _silu(gg_ref[...])
    o_ref[...] = o.astype(o_ref.dtype)


def _gla_call(p, s, row_off, bsz, n_tok, lp, s0, out_prev):
    m = p.shape[0]
    with_state = s0 is not None
    sb = row_off // n_tok
    assert row_off % n_tok == 0 and n_tok % (GLA_L * GLA_GROUP) == 0

    def seq_spec(col0, w):
        return pl.BlockSpec((n_tok, w), lambda b, h: (sb + b, col0 // w + h))

    gw = lp['gla_gate_w'].reshape(2, GLA_RANK, GLA_HEADS, GLA_DK).transpose(0, 2, 1, 3)
    gw_rows = jnp.zeros((2, GLA_HEADS, S_W, GLA_DK), jnp.float32)
    for d in range(2):
        r0 = 2 * SSD_HEADS + d * GLA_RANK
        gw_rows = gw_rows.at[d, :, r0:r0 + GLA_RANK, :].set(gw[d])
    in_specs = [seq_spec(P_GQ, GLA_DK), seq_spec(P_GK, GLA_DK), seq_spec(P_GV, GLA_DV), seq_spec(P_GG, GLA_DV),
                pl.BlockSpec((n_tok, S_W), lambda b, h: (sb + b, 0)),
                pl.BlockSpec((2, None, S_W, GLA_DK), lambda b, h: (0, h, 0, 0)),
                pl.BlockSpec((2, None, 1, GLA_DK), lambda b, h: (0, h, 0, 0)),
                pl.BlockSpec((1, GLA_DV), lambda b, h: (0, 0))]
    args = [p, p, p, p, s, gw_rows.astype(MXU_DTYPE), lp['gla_gate_b'].reshape(2, GLA_HEADS, 1, GLA_DK),
            lp['gla_norm_w'].reshape(1, GLA_DV)]
    state_block = pl.BlockSpec((None, 2, None, GLA_DK, GLA_DV), lambda b, h: (b, 0, h, 0, 0))
    out_specs = [pl.BlockSpec((n_tok, GLA_DV), lambda b, h: (sb + b, h))]
    out_shape = [jax.ShapeDtypeStruct((m, GROUP_W), MXU_DTYPE)]
    if with_state:
        in_specs.append(state_block)
        args.append(s0)
    else:
        out_specs.append(state_block)
        out_shape.append(jax.ShapeDtypeStruct((bsz, 2, GLA_HEADS, GLA_DK, GLA_DV), jnp.float32))
    n_in = len(args)
    aliases = {}
    if out_prev is not None:
        in_specs.append(pl.BlockSpec(memory_space=pl.ANY))
        args.append(out_prev)
        aliases = {n_in: 0}

    def body(*refs):
        if out_prev is not None:
            refs = refs[:n_in] + refs[n_in + 1:]
        _gla_kernel(*refs, n_tok=n_tok, with_state=with_state)

    return pl.pallas_call(
        body,
        grid=(bsz, GLA_HEADS),
        in_specs=in_specs,
        out_specs=out_specs,
        out_shape=out_shape,
        scratch_shapes=[pltpu.VMEM((2, n_tok, GLA_DK), jnp.float32), pltpu.VMEM((2, n_tok, GLA_DK), MXU_DTYPE),
                        pltpu.VMEM((2, n_tok, GLA_DK), MXU_DTYPE), pltpu.VMEM((2, n_tok, GLA_DK), MXU_DTYPE),
                        pltpu.VMEM((n_tok, GLA_DV), jnp.float32), pltpu.VMEM((2, GLA_DV, GLA_DK), jnp.float32)],
        input_output_aliases=aliases,
        compiler_params=_params(2, 48 << 20),
        name="gla_scan",
    )(*args)


def _mixers(p, s, lp, cached, donors):
    ctx_k, ctx_v, ssd0, gla0, lru0 = cached
    kv_w = ATT_KV_HEADS * HEAD_DIM
    att, k_new = _attention_call(p, 0, BATCH, SEQ, lp['q_norm'], lp['k_norm'], None, donors[0])
    att, = _attention_call(p, M_PROMPT, DEC_BATCH, DEC_SEQ, lp['q_norm'], lp['k_norm'],
                           (ctx_k.reshape(DEC_BATCH, PAST_LEN, kv_w), ctx_v.reshape(DEC_BATCH, PAST_LEN, kv_w)), att)
    v_new = p[:M_PROMPT, P_AV:P_AV + kv_w]

    y_p, ssd_new = _ssd_call(p, s, 0, BATCH, SEQ, lp, None)
    y_s, = _ssd_call(p, s, M_PROMPT, DEC_BATCH, DEC_SEQ, lp, ssd0)
    ssd = _ssd_finish(y_p, p, 0, lp['ssd_norm_w'], donors[1])
    ssd = _ssd_finish(y_s, p, M_PROMPT, lp['ssd_norm_w'], ssd)

    gla, gla_new = _gla_call(p, s, 0, BATCH, SEQ, lp, None, donors[2])
    gla, = _gla_call(p, s, M_PROMPT, DEC_BATCH, DEC_SEQ, lp, gla0, gla)

    lru, lru_new = _lru_call(p, 0, BATCH, SEQ, lp, None, donors[3])
    lru, = _lru_call(p, M_PROMPT, DEC_BATCH, DEC_SEQ, lp, lru0, lru)

    new_ctx = (k_new.reshape(BATCH, SEQ, ATT_KV_HEADS, HEAD_DIM), v_new.reshape(BATCH, SEQ, ATT_KV_HEADS, HEAD_DIM),
               ssd_new, gla_new, lru_new.reshape(BATCH, 2, LRU_W))
    return [att, ssd, gla, lru], new_ctx


IN_W = 9280
_W_IN_MAIN = ((0, P_AQ, 1024), (1024, P_AK, 256), (1280, P_AV, 256), (1536, P_SX, 1024), (2560, P_SZ, 1024),
              (3584, P_SB, 256), (3840, P_SC, 256), (4128, P_GQ, 512), (4640, P_GK, 512), (5152, P_GV, 1024),
              (6208, P_GG, 1024), (7232, P_LX, 1024), (8256, P_LG, 1024))
_W_IN_SMALL = ((4096, 0, 2 * SSD_HEADS), (6176, 2 * SSD_HEADS, 2 * GLA_RANK))


def _w_in_kernel(w_ref, main_ref, small_ref):
    for src, dst, width in _W_IN_MAIN:
        main_ref[dst:dst + width, :] = w_ref[src:src + width, :].astype(main_ref.dtype)
    small_ref[...] = jnp.zeros_like(small_ref)
    for src, dst, width in _W_IN_SMALL:
        small_ref[dst:dst + width, :] = w_ref[src:src + width, :].astype(small_ref.dtype)


def _reorder_w_in(w_t, l):
    tk = 256
    return pl.pallas_call(
        _w_in_kernel,
        grid=(D_MODEL // tk,),
        in_specs=[pl.BlockSpec((None, IN_W, tk), lambda i: (l, 0, i))],
        out_specs=[pl.BlockSpec((P_W, tk), lambda i: (0, i)), pl.BlockSpec((S_W, tk), lambda i: (0, i))],
        out_shape=[jax.ShapeDtypeStruct((P_W, D_MODEL), MXU_DTYPE), jax.ShapeDtypeStruct((S_W, D_MODEL), MXU_DTYPE)],
        compiler_params=_params(1, 48 << 20),
        name="w_in_regroup",
    )(w_t)


def kernel(x_prompt, x_sample, c, cache_attn_k, cache_attn_v, state_ssd, state_gla, state_lru, c_ctx,
           mod_w, mod_b, ln_g, ln_b, ffn_w_gate, ffn_w_up, ffn_w_down, w_in, w_out, q_norm, k_norm,
           ssd_conv_w, ssd_conv_b, ssd_a_log, ssd_dt_bias, ssd_d, ssd_norm_w,
           gla_gate_w, gla_gate_b, gla_norm_w,
           lru_conv_w, lru_conv_b, lru_wa, lru_ba, lru_wx, lru_bx, lru_lam):
    cond = jnp.concatenate([c_ctx[None], c, jnp.zeros((COND_ROWS - N_COND, D_MODEL), jnp.float32)], axis=0)
    mod_all = _mod_table(cond, mod_w, mod_b)

    def ffn(x, xm, mod, k_gate, l, half):
        up, rows0 = dict(tm=2048, tn=256, out_dtype=MXU_DTYPE, swiglu=True), 2048
        h0, wg, wu = _dense([xm], [ffn_w_gate, ffn_w_up], (l, half), tile0=0, n_tiles=1, cast=True, out_rows=rows0,
                            name="ffn_up_first", **up)
        h1 = _dense([xm], [wg, wu], (), tile0=1, n_tiles=(M_TOK - rows0) // up['tm'], out_tile0=0,
                    out_rows=M_TOK - rows0, name="ffn_up", **up)
        down = dict(tm=512, out_dtype=jnp.float32)
        split = rows0 // down['tm']
        y, wd = _dense([h0], [ffn_w_down], (l, half), tile0=0, n_tiles=1, tn=256, cast=True, out_rows=M_TOK,
                       resid=(x, mod, k_gate, 0.5), name="ffn_down_first", **down)
        y = _dense([h0], [wd], (), tile0=1, n_tiles=split - 1, tn=512, out_rows=M_TOK,
                   resid=(y, mod, k_gate, 0.5), name="ffn_down_head", **down)
        return _dense([h1], [wd], (), tile0=split, a_tile0=0, n_tiles=M_TOK // down['tm'] - split, tn=512,
                      out_rows=M_TOK, resid=(y, mod, k_gate, 0.5), name="ffn_down", **down)

    spare_mix = [jnp.zeros((M_TOK, GROUP_W), MXU_DTYPE)] * 4
    x, xm = _modulate(x_prompt, x_sample, mod_all[0], 1, 0)
    w_in_t = jnp.swapaxes(w_in, 1, 2)
    ctx_out = []
    for l in range(DEPTH):
        mod = mod_all[l]
        lp = dict(q_norm=q_norm[l], k_norm=k_norm[l], ssd_conv_w=ssd_conv_w[l], ssd_conv_b=ssd_conv_b[l],
                  ssd_a_log=ssd_a_log[l], ssd_dt_bias=ssd_dt_bias[l], ssd_d=ssd_d[l], ssd_norm_w=ssd_norm_w[l],
                  gla_gate_w=gla_gate_w[l], gla_gate_b=gla_gate_b[l], gla_norm_w=gla_norm_w[l],
                  lru_conv_w=lru_conv_w[l], lru_conv_b=lru_conv_b[l], lru_wa=lru_wa[l], lru_ba=lru_ba[l],
                  lru_wx=lru_wx[l], lru_bx=lru_bx[l], lru_lam=lru_lam[l])
        y = ffn(x, xm, mod, 2, l, 0)
        x, xm = _layer_norm(y, ln_g[l, 0], ln_b[l, 0], mod, 4, 3)
        w_main, w_small = _reorder_w_in(w_in_t, l)
        dense = dict(tile0=0, n_tiles=M_TOK // 1024, tm=1024, out_dtype=jnp.float32, w_t=True)
        p = _dense([xm], [w_main], (), tn=1024, name="proj_in", **dense)
        s = _dense([xm], [w_small], (), tn=S_W, name="proj_in_small", **dense)
        cached = (cache_attn_k[:, l], cache_attn_v[:, l], state_ssd[:, l], state_gla[:, l], state_lru[:, l])
        mix, new_ctx = _mixers(p, s, lp, cached, spare_mix)
        ctx_out.append(new_ctx)
        out = dict(tm=1024, out_dtype=jnp.float32)
        y, wo = _dense(mix, [w_out], (l,), tile0=0, n_tiles=1, tn=512, cast=True, resid=(x, mod, 5, 1.0),
                       name="proj_out_first", **out)
        y = _dense(mix, [wo], (), tile0=1, n_tiles=M_TOK // out['tm'] - 1, tn=1024, resid=(y, mod, 5, 1.0),
                   name="proj_out", **out)
        spare_mix = mix
        x, xm = _layer_norm(y, ln_g[l, 1], ln_b[l, 1], mod, 7, 6)
        y = ffn(x, xm, mod, 8, l, 1)
        if l + 1 < DEPTH:
            x, xm = _layer_norm(y, ln_g[l, 2], ln_b[l, 2], mod_all[l + 1], 1, 0)
        else:
            y_prompt, y_sample = _layer_norm(y, ln_g[l, 2], ln_b[l, 2])

    y_prompt = y_prompt.reshape(BATCH, SEQ, D_MODEL)
    y_sample = y_sample.reshape(DEC_BATCH, DEC_SEQ, D_MODEL)
    new_k, new_v, new_ssd, new_gla, new_lru = (jnp.stack([s_[i] for s_ in ctx_out], axis=1) for i in range(5))
    return (y_prompt, y_sample, new_k, new_v, new_ssd, new_gla, new_lru)
```
